```python
import jax, jax.numpy as jnp
from jax import lax
import numpy as np

D_MODEL = 1024
BATCH = 8
SEQ = 4096
DEPTH = 4

GRID_W = 64
CTX_LEN = 256
HEAD_DIM = 64
ATTN_HEADS = 8
KV_HEADS = 2
GQA_GROUP = ATTN_HEADS // KV_HEADS
ATTN_WIDTH = ATTN_HEADS * HEAD_DIM
KV_WIDTH = KV_HEADS * HEAD_DIM
MLP_HEADS = 8
MLP_WIDTH = MLP_HEADS * HEAD_DIM
CHUNK = 128
Q_BLOCK = 128
MIX_WIDTH = ATTN_WIDTH + MLP_WIDTH
IN_WIDTH = ATTN_WIDTH + 2 * KV_WIDTH + 2 * MLP_WIDTH
FFN_HIDDEN = -(-8 * D_MODEL // (3 * 256)) * 256
N_MOD = 6
ROPE_THETA = 10000.0
ROPE_AXIS_DIM = HEAD_DIM // 2
EPS = 1e-6

kernel_name = "hybrid_gqa_gmlp_diffusion_trunk"


def rms_norm(x, g):
    xf = x.astype(jnp.float32)
    y = xf * lax.rsqrt(jnp.mean(xf * xf, axis=-1, keepdims=True) + EPS)
    return (y * g.astype(jnp.float32)).astype(x.dtype)


def layer_norm(x, g, b):
    xf = x.astype(jnp.float32)
    mu = jnp.mean(xf, axis=-1, keepdims=True)
    var = jnp.mean(jnp.square(xf - mu), axis=-1, keepdims=True)
    y = (xf - mu) * lax.rsqrt(var + EPS)
    return (y * g.astype(jnp.float32) + b.astype(jnp.float32)).astype(x.dtype)


def modulate(h, shift, scale):
    return h * (1 + scale) + shift


def axial_rope_tables(n):
    rows = n // GRID_W
    pos_row = jnp.broadcast_to(jnp.arange(rows, dtype=jnp.float32)[:, None], (rows, GRID_W)).reshape(-1)
    pos_col = jnp.broadcast_to(jnp.arange(GRID_W, dtype=jnp.float32)[None, :], (rows, GRID_W)).reshape(-1)
    inv = ROPE_THETA ** (-jnp.arange(0, ROPE_AXIS_DIM, 2, dtype=jnp.float32) / ROPE_AXIS_DIM)
    ang = jnp.concatenate([pos_row[:, None] * inv, pos_col[:, None] * inv], axis=-1)
    return jnp.cos(ang), jnp.sin(ang)


def apply_rope(x, cos, sin):
    b, n, h, d = x.shape
    half = ROPE_AXIS_DIM // 2
    xr = x.reshape(b, n, h, 2, 2, half)
    c = cos.reshape(n, 2, half)[None, :, None].astype(x.dtype)
    s = sin.reshape(n, 2, half)[None, :, None].astype(x.dtype)
    x1, x2 = xr[..., 0, :], xr[..., 1, :]
    out = jnp.stack([x1 * c - x2 * s, x1 * s + x2 * c], axis=-2)
    return out.reshape(b, n, h, d)


def split_proj(p):
    b, n = p.shape[:2]
    q, k, v, z = jnp.split(p, [ATTN_WIDTH, ATTN_WIDTH + KV_WIDTH, ATTN_WIDTH + 2 * KV_WIDTH], axis=-1)
    return (q.reshape(b, n, ATTN_HEADS, HEAD_DIM), k.reshape(b, n, KV_HEADS, HEAD_DIM),
            v.reshape(b, n, KV_HEADS, HEAD_DIM), z)


def latent_attention(q, k_all, v_all):
    b, n = q.shape[:2]
    nblk = n // Q_BLOCK
    scale = 1.0 / np.sqrt(HEAD_DIM)
    qb = q.reshape(b, nblk, Q_BLOCK, KV_HEADS, GQA_GROUP, HEAD_DIM).transpose(1, 0, 3, 4, 2, 5)
    kt = k_all.transpose(0, 2, 1, 3)
    vt = v_all.transpose(0, 2, 1, 3)

    def one_block(qblk):
        s = jnp.einsum("bkgqd,bksd->bkgqs", qblk, kt, preferred_element_type=jnp.float32) * scale
        p = jax.nn.softmax(s, axis=-1).astype(vt.dtype)
        return jnp.einsum("bkgqs,bksd->bkgqd", p, vt)

    ob = lax.map(one_block, qb)
    return ob.transpose(1, 0, 4, 2, 3, 5).reshape(b, n, ATTN_WIDTH)


def context_attention(q, k, v):
    b, L = q.shape[:2]
    scale = 1.0 / np.sqrt(HEAD_DIM)
    qg = q.reshape(b, L, KV_HEADS, GQA_GROUP, HEAD_DIM)
    s = jnp.einsum("blkgd,bmkd->bkglm", qg, k, preferred_element_type=jnp.float32) * scale
    p = jax.nn.softmax(s, axis=-1).astype(v.dtype)
    o = jnp.einsum("bkglm,bmkd->blkgd", p, v)
    return o.reshape(b, L, ATTN_WIDTH)


def chunk_spatial_gating(z, g_sg, b_sg, w_s, b_s):
    b, n = z.shape[:2]
    z = jax.nn.gelu(z)
    u, v = jnp.split(z, 2, axis=-1)
    v = layer_norm(v, g_sg, b_sg)
    v = v.reshape(b, n // CHUNK, CHUNK, MLP_HEADS, HEAD_DIM)
    s = jnp.einsum("hpq,bcqhd->bcphd", w_s, v) + b_s.T[None, None, :, :, None]
    return u * s.reshape(b, n, MLP_WIDTH)


def swiglu(h, w_ffn_in, w_ffn_out):
    gate, up = jnp.split(h @ w_ffn_in, 2, axis=-1)
    return (jax.nn.silu(gate) * up) @ w_ffn_out


def setup_inputs(seed: int = 0) -> dict:
    key = jax.random.key(seed)
    ks = jax.random.split(key, 20)
    f32 = jnp.float32
    nrm = lambda k, shape, s: jax.random.normal(k, shape, f32) * s
    return {
        "x": nrm(ks[0], (BATCH, SEQ, D_MODEL), 1.0),
        "c": nrm(ks[1], (BATCH, D_MODEL), 1.0),
        "ctx": nrm(ks[2], (BATCH, CTX_LEN, D_MODEL), 1.0),
        "c_ctx": nrm(ks[3], (D_MODEL,), 1.0),
        "w_mod": nrm(ks[4], (DEPTH, D_MODEL, N_MOD * D_MODEL), 0.3 * D_MODEL ** -0.5),
        "b_mod": nrm(ks[5], (DEPTH, N_MOD * D_MODEL), 0.02),
        "g_pre_mix": 1.0 + nrm(ks[6], (DEPTH, D_MODEL), 0.02),
        "g_post_mix": 1.0 + nrm(ks[7], (DEPTH, D_MODEL), 0.02),
        "g_pre_ffn": 1.0 + nrm(ks[8], (DEPTH, D_MODEL), 0.02),
        "g_post_ffn": 1.0 + nrm(ks[9], (DEPTH, D_MODEL), 0.02),
        "w_in": nrm(ks[10], (DEPTH, D_MODEL, IN_WIDTH), D_MODEL ** -0.5),
        "g_q": 1.0 + nrm(ks[11], (DEPTH, HEAD_DIM), 0.02),
        "g_k": 1.0 + nrm(ks[12], (DEPTH, HEAD_DIM), 0.02),
        "g_sg": 1.0 + nrm(ks[13], (DEPTH, MLP_WIDTH), 0.02),
        "b_sg": nrm(ks[14], (DEPTH, MLP_WIDTH), 0.02),
        "w_s": nrm(ks[15], (DEPTH, MLP_HEADS, CHUNK, CHUNK), CHUNK ** -0.5),
        "b_s": 1.0 + nrm(ks[16], (DEPTH, MLP_HEADS, CHUNK), 0.02),
        "w_out": nrm(ks[17], (DEPTH, MIX_WIDTH, D_MODEL), MIX_WIDTH ** -0.5),
        "w_ffn_in": nrm(ks[18], (DEPTH, D_MODEL, 2 * FFN_HIDDEN), D_MODEL ** -0.5),
        "w_ffn_out": nrm(ks[19], (DEPTH, FFN_HIDDEN, D_MODEL), FFN_HIDDEN ** -0.5),
    }


def reference(x, c, ctx, c_ctx, w_mod, b_mod, g_pre_mix, g_post_mix, g_pre_ffn, g_post_ffn,
              w_in, g_q, g_k, g_sg, b_sg, w_s, b_s, w_out, w_ffn_in, w_ffn_out):
    b, n = x.shape[:2]
    cos, sin = axial_rope_tables(n)
    silu_c = jax.nn.silu(c)
    silu_cc = jax.nn.silu(c_ctx)
    xc = ctx
    for l in range(DEPTH):
        last = l == DEPTH - 1
        m_x = (silu_c @ w_mod[l] + b_mod[l]).reshape(b, N_MOD, D_MODEL)
        mx = [m_x[:, i, None, :] for i in range(N_MOD)]
        m_c = (silu_cc @ w_mod[l] + b_mod[l]).reshape(N_MOD, D_MODEL)
        mc = [m_c[i] for i in range(N_MOD)]

        hx = modulate(rms_norm(x, g_pre_mix[l]), mx[0], mx[1])
        hc = modulate(rms_norm(xc, g_pre_mix[l]), mc[0], mc[1])
        qx, kx, vx, zx = split_proj(hx @ w_in[l])
        qx = apply_rope(rms_norm(qx, g_q[l]), cos, sin)
        kx = apply_rope(rms_norm(kx, g_k[l]), cos, sin)
        if last:
            kv_c = hc @ w_in[l][:, ATTN_WIDTH:ATTN_WIDTH + 2 * KV_WIDTH]
            kc, vc = jnp.split(kv_c, 2, axis=-1)
            kc = kc.reshape(b, -1, KV_HEADS, HEAD_DIM)
            vc = vc.reshape(b, -1, KV_HEADS, HEAD_DIM)
        else:
            qc, kc, vc, zc = split_proj(hc @ w_in[l])
            qc = rms_norm(qc, g_q[l])
        kc = rms_norm(kc, g_k[l])
        k_all = jnp.concatenate([kc, kx], axis=1)
        v_all = jnp.concatenate([vc, vx], axis=1)
        attn_x = latent_attention(qx, k_all, v_all)
        mlp_x = chunk_spatial_gating(zx, g_sg[l], b_sg[l], w_s[l], b_s[l])
        out_x = jnp.concatenate([attn_x, mlp_x], axis=-1) @ w_out[l]
        x = x + mx[2] * rms_norm(out_x, g_post_mix[l])
        if not last:
            attn_c = context_attention(qc, kc, vc)
            mlp_c = chunk_spatial_gating(zc, g_sg[l], b_sg[l], w_s[l], b_s[l])
            out_c = jnp.concatenate([attn_c, mlp_c], axis=-1) @ w_out[l]
            xc = xc + mc[2] * rms_norm(out_c, g_post_mix[l])

        fx = swiglu(modulate(rms_norm(x, g_pre_ffn[l]), mx[3], mx[4]), w_ffn_in[l], w_ffn_out[l])
        x = x + mx[5] * rms_norm(fx, g_post_ffn[l])
        if not last:
            fc = swiglu(modulate(rms_norm(xc, g_pre_ffn[l]), mc[3], mc[4]), w_ffn_in[l], w_ffn_out[l])
            xc = xc + mc[5] * rms_norm(fc, g_post_ffn[l])
    return x
```

```python
import functools
import math

import jax
import jax.numpy as jnp
from jax import lax
from jax.experimental import pallas as pl
from jax.experimental.pallas import tpu as pltpu

D_MODEL = 1024
DEPTH = 4
GRID_W = 64
HEAD_DIM = 64
ATTN_HEADS = 8
KV_HEADS = 2
GQA_GROUP = ATTN_HEADS // KV_HEADS
ATTN_WIDTH = ATTN_HEADS * HEAD_DIM
KV_WIDTH = KV_HEADS * HEAD_DIM
MLP_HEADS = 8
MLP_WIDTH = MLP_HEADS * HEAD_DIM
CHUNK = 128
IN_WIDTH = ATTN_WIDTH + 2 * KV_WIDTH + 2 * MLP_WIDTH
FFN_HIDDEN = 2816
N_MOD = 6
ROPE_THETA = 10000.0
ROPE_AXIS_DIM = HEAD_DIM // 2
EPS = 1e-6

LANES = 128
MXU_DIM = 256
MOD_ROWS = 16
MOD_SLOTS = 8
FFN_CHUNK = MXU_DIM
N_FFN_CHUNKS = FFN_HIDDEN // FFN_CHUNK
LOG2E = math.log2(math.e)
VMEM_LIMIT = 56 * 1024 * 1024

F32 = jnp.float32
BF16 = jnp.bfloat16


def _const_spec(shape):
    zeros = (0,) * len(shape)
    return pl.BlockSpec(shape, lambda *_: zeros, pipeline_mode=pl.Buffered(1))


def _rms(t, g):
    return t * lax.rsqrt(jnp.mean(t * t, axis=-1, keepdims=True) + EPS) * g


def _mod_kernel(c_ref, w_ref, b_ref, o_ref):
    cv = c_ref[...]
    act = (cv * jax.nn.sigmoid(cv)).astype(BF16)
    o_ref[0] = jnp.dot(act, w_ref[0].astype(BF16), preferred_element_type=F32) + b_ref[0]


def _modulation(cvec, w_mod, b_mod):
    tn = 1536
    n_out = N_MOD * D_MODEL
    return pl.pallas_call(
        _mod_kernel,
        grid=(DEPTH, n_out // tn),
        in_specs=[
            pl.BlockSpec((MOD_ROWS, D_MODEL), lambda l, j: (0, 0)),
            pl.BlockSpec((1, D_MODEL, tn), lambda l, j: (l, 0, j)),
            pl.BlockSpec((1, 1, tn), lambda l, j: (l, 0, j)),
        ],
        out_specs=pl.BlockSpec((1, MOD_ROWS, tn), lambda l, j: (l, 0, j)),
        out_shape=jax.ShapeDtypeStruct((DEPTH, MOD_ROWS, n_out), F32),
        compiler_params=pltpu.CompilerParams(
            dimension_semantics=("arbitrary", "arbitrary"), vmem_limit_bytes=VMEM_LIMIT),
        name="modulation",
    )(cvec, w_mod, b_mod.reshape(DEPTH, 1, n_out))


def _group_sum(t2, ones_bd):
    hi = t2.astype(BF16)
    lo = (t2 - hi.astype(F32)).astype(BF16)
    return (jnp.dot(hi, ones_bd, preferred_element_type=F32)
            + jnp.dot(lo, ones_bd, preferred_element_type=F32))


def _swap_halves16(t):
    lane = lax.broadcasted_iota(jnp.int32, t.shape, 1)
    return jnp.where((lane % 32) < 16, pltpu.roll(t, LANES - 16, 1), pltpu.roll(t, 16, 1))


def _inproj_kernel(x_ref, mod_ref, gpre_ref, w_ref, gq_ref, gk_ref, cos_ref, sin_ref,
                   gsg_ref, bsg_ref, ws_ref, bs_ref,
                   q_ref, kt_ref, v_ref, mlp_ref, *, tm, use_rope):
    x = x_ref[0]
    mod = mod_ref[0, 0]
    shift, scale = mod[0:1], mod[1:2]
    h = (_rms(x, gpre_ref[...]) * (1.0 + scale) + shift).astype(BF16)
    p = jnp.dot(h, w_ref[...], preferred_element_type=F32)

    r = lax.broadcasted_iota(jnp.int32, (MXU_DIM, MXU_DIM), 0) // HEAD_DIM
    c = lax.broadcasted_iota(jnp.int32, (MXU_DIM, MXU_DIM), 1) // HEAD_DIM
    ones_bd = jnp.where(r == c, 1.0, 0.0).astype(BF16)

    def head_norm(t, g):
        w = t.shape[1]
        ss = _group_sum(t * t, ones_bd[:w, :w])
        return t * lax.rsqrt(ss * (1.0 / HEAD_DIM) + EPS) * g

    def rope(t):
        if not use_rope:
            return t
        return t * cos_ref[...] + _swap_halves16(t) * sin_ref[...]

    q_scale = LOG2E / math.sqrt(HEAD_DIM)
    for half in range(ATTN_WIDTH // MXU_DIM):
        qn = head_norm(p[:, half * MXU_DIM:(half + 1) * MXU_DIM], gq_ref[...])
        for pair in range(MXU_DIM // LANES):
            t = rope(qn[:, pair * LANES:(pair + 1) * LANES]) * q_scale
            h0 = half * (MXU_DIM // HEAD_DIM) + pair * 2
            q_ref[0, h0] = t[:, :HEAD_DIM].astype(BF16)
            q_ref[0, h0 + 1] = pltpu.roll(t, HEAD_DIM, 1)[:, :HEAD_DIM].astype(BF16)

    kn = rope(head_norm(p[:, ATTN_WIDTH:ATTN_WIDTH + KV_WIDTH], gk_ref[...]))
    knt = kn.T
    kt_ref[0, 0] = knt[:HEAD_DIM].astype(BF16)
    kt_ref[0, 1] = knt[HEAD_DIM:].astype(BF16)

    vv = p[:, ATTN_WIDTH + KV_WIDTH:ATTN_WIDTH + 2 * KV_WIDTH]
    lane = lax.broadcasted_iota(jnp.int32, (tm, LANES), 1)
    tail = jnp.where(lane == HEAD_DIM, 1.0, 0.0)
    v_ref[0, 0] = jnp.where(lane < HEAD_DIM, vv, tail).astype(BF16)
    v_ref[0, 1] = jnp.where(lane < HEAD_DIM, pltpu.roll(vv, HEAD_DIM, 1), tail).astype(BF16)

    z = p[:, ATTN_WIDTH + 2 * KV_WIDTH:]
    z = 0.5 * z * (1.0 + jnp.tanh(math.sqrt(2.0 / math.pi) * (z + 0.044715 * (z * z * z))))
    u = z[:, :MLP_WIDTH]
    v2 = z[:, MLP_WIDTH:]
    mu = jnp.mean(v2, axis=-1, keepdims=True)
    vc = v2 - mu
    var = jnp.mean(vc * vc, axis=-1, keepdims=True)
    vln = vc * lax.rsqrt(var + EPS) * gsg_ref[...] + bsg_ref[...]
    lane_w = lax.broadcasted_iota(jnp.int32, (tm, MLP_WIDTH), 1)
    even_head = (lane_w % LANES) < HEAD_DIM
    v_even = jnp.where(even_head, vln, 0.0).astype(BF16)
    v_odd = jnp.where(even_head, 0.0, vln).astype(BF16)
    for ch in range(tm // CHUNK):
        rows = slice(ch * CHUNK, (ch + 1) * CHUNK)
        for jb in range(MLP_WIDTH // LANES):
            cols = slice(jb * LANES, (jb + 1) * LANES)
            rhs = jnp.concatenate([v_even[rows, cols], v_odd[rows, cols]], axis=0)
            s = jnp.dot(ws_ref[jb], rhs, preferred_element_type=F32)
            mlp_ref[0, rows, cols] = (u[rows, cols] * (s + bs_ref[:, cols])).astype(BF16)


def _inproj(x, mods, layer, stream_row, g_pre, w_in, gq_t, gk_t, cos_t, sin_t,
            g_sg, b_sg, ws_cat, bs_full, *, tm, use_rope):
    b, n, _ = x.shape
    grid = (b, n // tm)
    if stream_row is None:
        mod_map = lambda i, j: (layer, i, 0, 0)
    else:
        mod_map = lambda i, j: (layer, stream_row, 0, 0)
    kern = functools.partial(_inproj_kernel, tm=tm, use_rope=use_rope)
    return pl.pallas_call(
        kern,
        grid=grid,
        in_specs=[
            pl.BlockSpec((1, tm, D_MODEL), lambda i, j: (i, j, 0)),
            pl.BlockSpec((1, 1, MOD_SLOTS, D_MODEL), mod_map),
            _const_spec((1, D_MODEL)),
            _const_spec((D_MODEL, IN_WIDTH)),
            _const_spec((1, MXU_DIM)),
            _const_spec((1, LANES)),
            pl.BlockSpec((tm, LANES), lambda i, j: (j, 0)),
            pl.BlockSpec((tm, LANES), lambda i, j: (j, 0)),
            _const_spec((1, MLP_WIDTH)),
            _const_spec((1, MLP_WIDTH)),
            _const_spec((MLP_WIDTH // LANES, CHUNK, 2 * CHUNK)),
            _const_spec((CHUNK, MLP_WIDTH)),
        ],
        out_specs=[
            pl.BlockSpec((1, ATTN_HEADS, tm, HEAD_DIM), lambda i, j: (i, 0, j, 0)),
            pl.BlockSpec((1, KV_HEADS, HEAD_DIM, tm), lambda i, j: (i, 0, 0, j)),
            pl.BlockSpec((1, KV_HEADS, tm, LANES), lambda i, j: (i, 0, j, 0)),
            pl.BlockSpec((1, tm, MLP_WIDTH), lambda i, j: (i, j, 0)),
        ],
        out_shape=[
            jax.ShapeDtypeStruct((b, ATTN_HEADS, n, HEAD_DIM), BF16),
            jax.ShapeDtypeStruct((b, KV_HEADS, HEAD_DIM, n), BF16),
            jax.ShapeDtypeStruct((b, KV_HEADS, n, LANES), BF16),
            jax.ShapeDtypeStruct((b, n, MLP_WIDTH), BF16),
        ],
        compiler_params=pltpu.CompilerParams(
            dimension_semantics=("arbitrary", "arbitrary"), vmem_limit_bytes=VMEM_LIMIT),
        name="inproj",
    )(x, mods, g_pre, w_in, gq_t, gk_t, cos_t, sin_t, g_sg, b_sg, ws_cat, bs_full)


def _attn_kernel(*refs, bq, seg_lens, bk):
    n_seg = len(seg_lens)
    q_ref = refs[0]
    kt_refs = refs[1:1 + n_seg]
    v_refs = refs[1 + n_seg:1 + 2 * n_seg]
    o_ref = refs[1 + 2 * n_seg]
    rows = GQA_GROUP * bq
    q = q_ref[0].reshape(rows, HEAD_DIM)

    def step(kt_blk, v_blk, m, acc):
        s = jnp.dot(q, kt_blk, preferred_element_type=F32)
        m_new = jnp.maximum(m, jnp.max(s, axis=-1, keepdims=True))
        alpha = jnp.exp2(m - m_new)
        pr = jnp.exp2(s - m_new).astype(BF16)
        acc = alpha * acc + jnp.dot(pr, v_blk, preferred_element_type=F32)
        return m_new, acc

    m = jnp.full((rows, 1), -jnp.inf, F32)
    acc = jnp.zeros((rows, LANES), F32)
    for kt_ref, v_ref, seg in zip(kt_refs, v_refs, seg_lens):
        if seg <= bk:
            m, acc = step(kt_ref[0, 0], v_ref[0, 0], m, acc)
        else:
            def body(j, carry, kt_ref=kt_ref, v_ref=v_ref):
                start = pl.multiple_of(j * bk, bk)
                return step(kt_ref[0, 0, :, pl.ds(start, bk)], v_ref[0, 0, pl.ds(start, bk), :], *carry)
            m, acc = lax.fori_loop(0, seg // bk, body, (m, acc))

    on = acc / acc[:, HEAD_DIM:HEAD_DIM + 1]
    lane = lax.broadcasted_iota(jnp.int32, (bq, LANES), 1)
    pairs = []
    for pair in range(GQA_GROUP // 2):
        a = on[(2 * pair) * bq:(2 * pair + 1) * bq]
        bb = on[(2 * pair + 1) * bq:(2 * pair + 2) * bq]
        pairs.append(jnp.where(lane < HEAD_DIM, a, pltpu.roll(bb, HEAD_DIM, 1)))
    o_ref[0] = jnp.concatenate(pairs, axis=1).astype(BF16)


def _attention(q, kts, vs, *, bq, bk):
    b, _, n, _ = q.shape
    seg_lens = tuple(kt.shape[-1] for kt in kts)
    kern = functools.partial(_attn_kernel, bq=bq, seg_lens=seg_lens, bk=bk)
    in_specs = [pl.BlockSpec((1, GQA_GROUP, bq, HEAD_DIM), lambda i, g, j: (i, g, j, 0))]
    in_specs += [pl.BlockSpec((1, 1, HEAD_DIM, s), lambda i, g, j: (i, g, 0, 0)) for s in seg_lens]
    in_specs += [pl.BlockSpec((1, 1, s, LANES), lambda i, g, j: (i, g, 0, 0)) for s in seg_lens]
    return pl.pallas_call(
        kern,
        grid=(b, KV_HEADS, n // bq),
        in_specs=in_specs,
        out_specs=pl.BlockSpec((1, bq, GQA_GROUP * HEAD_DIM), lambda i, g, j: (i, j, g)),
        out_shape=jax.ShapeDtypeStruct((b, n, ATTN_WIDTH), BF16),
        compiler_params=pltpu.CompilerParams(
            dimension_semantics=("arbitrary", "arbitrary", "arbitrary"), vmem_limit_bytes=VMEM_LIMIT),
        name="attention",
    )(q, *kts, *vs)


def _outffn_kernel(attn_ref, mlp_ref, x_ref, mod_ref, gpm_ref, gpf_ref, gqf_ref,
                   wo_ref, wgu_ref, wd_ref, o_ref, acc_ref):
    mod = mod_ref[0, 0]
    gate_mix, shift, scale, gate_ffn = mod[2:3], mod[3:4], mod[4:5], mod[5:6]
    out = (jnp.dot(attn_ref[0], wo_ref[:ATTN_WIDTH], preferred_element_type=F32)
           + jnp.dot(mlp_ref[0], wo_ref[ATTN_WIDTH:], preferred_element_type=F32))
    x1 = x_ref[0] + gate_mix * _rms(out, gpm_ref[...])
    h = (_rms(x1, gpf_ref[...]) * (1.0 + scale) + shift).astype(BF16)

    acc_ref[...] = jnp.zeros_like(acc_ref)

    def body(ci, carry):
        gu = jnp.dot(h, wgu_ref[ci], preferred_element_type=F32)
        g = gu[:, :FFN_CHUNK]
        a = (g * jax.nn.sigmoid(g) * gu[:, FFN_CHUNK:]).astype(BF16)
        acc_ref[...] += jnp.dot(a, wd_ref[ci], preferred_element_type=F32)
        return carry

    lax.fori_loop(0, N_FFN_CHUNKS, body, 0)
    o_ref[0] = x1 + gate_ffn * _rms(acc_ref[...], gqf_ref[...])


def _outffn(attn, mlp, x, mods, layer, stream_row, g_post_mix, g_pre_ffn, g_post_ffn,
            w_out, w_gu, w_down, *, tm):
    b, n, _ = x.shape
    if stream_row is None:
        mod_map = lambda i, j: (layer, i, 0, 0)
    else:
        mod_map = lambda i, j: (layer, stream_row, 0, 0)
    row_spec = lambda w: pl.BlockSpec((1, tm, w), lambda i, j: (i, j, 0))
    return pl.pallas_call(
        _outffn_kernel,
        grid=(b, n // tm),
        in_specs=[
            row_spec(ATTN_WIDTH),
            row_spec(MLP_WIDTH),
            row_spec(D_MODEL),
            pl.BlockSpec((1, 1, MOD_SLOTS, D_MODEL), mod_map),
            _const_spec((1, D_MODEL)),
            _const_spec((1, D_MODEL)),
            _const_spec((1, D_MODEL)),
            _const_spec((ATTN_WIDTH + MLP_WIDTH, D_MODEL)),
            _const_spec((N_FFN_CHUNKS, D_MODEL, 2 * FFN_CHUNK)),
            _const_spec((N_FFN_CHUNKS, FFN_CHUNK, D_MODEL)),
        ],
        out_specs=row_spec(D_MODEL),
        out_shape=jax.ShapeDtypeStruct((b, n, D_MODEL), F32),
        scratch_shapes=[pltpu.VMEM((tm, D_MODEL), F32)],
        compiler_params=pltpu.CompilerParams(
            dimension_semantics=("arbitrary", "arbitrary"), vmem_limit_bytes=VMEM_LIMIT),
        name="outffn",
    )(attn, mlp, x, mods, g_post_mix, g_pre_ffn, g_post_ffn, w_out, w_gu, w_down)


def _rope_tables(n):
    rows = n // GRID_W
    pos_row = jnp.broadcast_to(jnp.arange(rows, dtype=F32)[:, None], (rows, GRID_W)).reshape(-1)
    pos_col = jnp.broadcast_to(jnp.arange(GRID_W, dtype=F32)[None, :], (rows, GRID_W)).reshape(-1)
    inv = ROPE_THETA ** (-jnp.arange(0, ROPE_AXIS_DIM, 2, dtype=F32) / ROPE_AXIS_DIM)
    ang_r = pos_row[:, None] * inv
    ang_c = pos_col[:, None] * inv
    cos64 = jnp.concatenate([jnp.cos(ang_r)] * 2 + [jnp.cos(ang_c)] * 2, axis=-1)
    sin64 = jnp.concatenate([-jnp.sin(ang_r), jnp.sin(ang_r), -jnp.sin(ang_c), jnp.sin(ang_c)], axis=-1)
    return jnp.tile(cos64, (1, 2)), jnp.tile(sin64, (1, 2))


def kernel(x, c, ctx, c_ctx, w_mod, b_mod, g_pre_mix, g_post_mix, g_pre_ffn, g_post_ffn,
           w_in, g_q, g_k, g_sg, b_sg, w_s, b_s, w_out, w_ffn_in, w_ffn_out):
    b, n, _ = x.shape
    n_ctx = ctx.shape[1]
    cos_t, sin_t = _rope_tables(n)
    cos_c, sin_c = cos_t[:n_ctx], sin_t[:n_ctx]

    cvec = jnp.concatenate([c, c_ctx[None], jnp.zeros((MOD_ROWS - b - 1, D_MODEL), F32)], axis=0)
    mods = _modulation(cvec, w_mod, b_mod)
    mods = mods.reshape(DEPTH, MOD_ROWS, N_MOD, D_MODEL)
    mods = jnp.pad(mods, ((0, 0), (0, 0), (0, MOD_SLOTS - N_MOD), (0, 0)))

    w_in_b = w_in.astype(BF16)
    w_out_b = w_out.astype(BF16)
    w_gate, w_up = w_ffn_in[..., :FFN_HIDDEN], w_ffn_in[..., FFN_HIDDEN:]
    w_gu = jnp.concatenate(
        [w_gate.reshape(DEPTH, D_MODEL, N_FFN_CHUNKS, FFN_CHUNK),
         w_up.reshape(DEPTH, D_MODEL, N_FFN_CHUNKS, FFN_CHUNK)], axis=-1)
    w_gu = w_gu.transpose(0, 2, 1, 3).astype(BF16)
    w_down = w_ffn_out.reshape(DEPTH, N_FFN_CHUNKS, FFN_CHUNK, D_MODEL).astype(BF16)
    ws_cat = w_s.reshape(DEPTH, MLP_HEADS // 2, 2, CHUNK, CHUNK).transpose(0, 1, 3, 2, 4)
    ws_cat = ws_cat.reshape(DEPTH, MLP_HEADS // 2, CHUNK, 2 * CHUNK).astype(BF16)
    bs_full = jnp.repeat(b_s.transpose(0, 2, 1), HEAD_DIM, axis=2)
    gq_t = jnp.tile(g_q, (1, MXU_DIM // HEAD_DIM))[:, None]
    gk_t = jnp.tile(g_k, (1, LANES // HEAD_DIM))[:, None]

    xc = ctx
    for l in range(DEPTH):
        last = l == DEPTH - 1
        in_args = (g_pre_mix[l][None], w_in_b[l], gq_t[l], gk_t[l])
        mlp_args = (g_sg[l][None], b_sg[l][None], ws_cat[l], bs_full[l])
        ffn_args = (g_post_mix[l][None], g_pre_ffn[l][None], g_post_ffn[l][None],
                    w_out_b[l], w_gu[l], w_down[l])

        qc, kct, vc, mlp_c = _inproj(xc, mods, l, b, *in_args, cos_c, sin_c, *mlp_args,
                                     tm=n_ctx, use_rope=False)
        qx, kxt, vx, mlp_x = _inproj(x, mods, l, None, *in_args, cos_t, sin_t, *mlp_args,
                                     tm=512, use_rope=True)
        attn_x = _attention(qx, (kct, kxt), (vc, vx), bq=256, bk=512)
        x = _outffn(attn_x, mlp_x, x, mods, l, None, *ffn_args, tm=512)
        if not last:
            attn_c = _attention(qc, (kct,), (vc,), bq=n_ctx, bk=512)
            xc = _outffn(attn_c, mlp_c, xc, mods, l, b, *ffn_args, tm=n_ctx)
    return x
```

```python
import functools
import math

import jax
import jax.numpy as jnp
from jax import lax
from jax.experimental import pallas as pl
from jax.experimental.pallas import tpu as pltpu

D_MODEL = 1024
DEPTH = 4
GRID_W = 64
HEAD_DIM = 64
ATTN_HEADS = 8
KV_HEADS = 2
GQA_GROUP = ATTN_HEADS // KV_HEADS
ATTN_WIDTH = ATTN_HEADS * HEAD_DIM
KV_WIDTH = KV_HEADS * HEAD_DIM
MLP_HEADS = 8
MLP_WIDTH = MLP_HEADS * HEAD_DIM
CHUNK = 128
IN_WIDTH = ATTN_WIDTH + 2 * KV_WIDTH + 2 * MLP_WIDTH
FFN_HIDDEN = 2816
N_MOD = 6
ROPE_THETA = 10000.0
ROPE_AXIS_DIM = HEAD_DIM // 2
EPS = 1e-6

LANES = 128
MXU_DIM = 256
MOD_ROWS = 16
MOD_SLOTS = 8
V_ROWS = HEAD_DIM + 16
FFN_CHUNK = MXU_DIM
N_FFN_CHUNKS = FFN_HIDDEN // FFN_CHUNK
QK_AHEAD = 4
LOG2E = math.log2(math.e)
VMEM_LIMIT = 56 * 1024 * 1024

F32 = jnp.float32
BF16 = jnp.bfloat16


def _const_spec(shape):
    zeros = (0,) * len(shape)
    return pl.BlockSpec(shape, lambda *_: zeros, pipeline_mode=pl.Buffered(1))


def _rms(t, g):
    return t * lax.rsqrt(jnp.mean(t * t, axis=-1, keepdims=True) + EPS) * g


def _mod_kernel(c_ref, w_ref, b_ref, o_ref):
    cv = c_ref[...]
    act = (cv * jax.nn.sigmoid(cv)).astype(BF16)
    o_ref[0] = jnp.dot(act, w_ref[0].astype(BF16), preferred_element_type=F32) + b_ref[0]


def _modulation(cvec, w_mod, b_mod):
    tn = 1536
    n_out = N_MOD * D_MODEL
    return pl.pallas_call(
        _mod_kernel,
        grid=(DEPTH, n_out // tn),
        in_specs=[
            pl.BlockSpec((MOD_ROWS, D_MODEL), lambda l, j: (0, 0)),
            pl.BlockSpec((1, D_MODEL, tn), lambda l, j: (l, 0, j)),
            pl.BlockSpec((1, 1, tn), lambda l, j: (l, 0, j)),
        ],
        out_specs=pl.BlockSpec((1, MOD_ROWS, tn), lambda l, j: (l, 0, j)),
        out_shape=jax.ShapeDtypeStruct((DEPTH, MOD_ROWS, n_out), F32),
        compiler_params=pltpu.CompilerParams(
            dimension_semantics=("arbitrary", "arbitrary"), vmem_limit_bytes=VMEM_LIMIT),
        name="modulation",
    )(cvec, w_mod, b_mod.reshape(DEPTH, 1, n_out))


def _group_sum(t2, ones_bd):
    hi = t2.astype(BF16)
    lo = (t2 - hi.astype(F32)).astype(BF16)
    return (jnp.dot(hi, ones_bd, preferred_element_type=F32)
            + jnp.dot(lo, ones_bd, preferred_element_type=F32))


def _swap_halves16(t):
    lane = lax.broadcasted_iota(jnp.int32, t.shape, 1)
    return jnp.where((lane % 32) < 16, pltpu.roll(t, LANES - 16, 1), pltpu.roll(t, 16, 1))


def _inproj_kernel(x_ref, mod_ref, gpre_ref, w_ref, gq_ref, gk_ref, cos_ref, sin_ref,
                   gsg_ref, bsg_ref, ws_ref, bs_ref,
                   qt_ref, k_ref, vt_ref, mlp_ref, *, tm, use_rope):
    x = x_ref[0]
    mod = mod_ref[0, 0]
    shift, scale = mod[0:1], mod[1:2]
    h = (_rms(x, gpre_ref[...]) * (1.0 + scale) + shift).astype(BF16)
    p = jnp.dot(h, w_ref[...], preferred_element_type=F32)

    r = lax.broadcasted_iota(jnp.int32, (MXU_DIM, MXU_DIM), 0) // HEAD_DIM
    c = lax.broadcasted_iota(jnp.int32, (MXU_DIM, MXU_DIM), 1) // HEAD_DIM
    ones_bd = jnp.where(r == c, 1.0, 0.0).astype(BF16)

    def head_norm(t, g):
        w = t.shape[1]
        ss = _group_sum(t * t, ones_bd[:w, :w])
        return t * lax.rsqrt(ss * (1.0 / HEAD_DIM) + EPS) * g

    def rope(t):
        if not use_rope:
            return t
        return t * cos_ref[...] + _swap_halves16(t) * sin_ref[...]

    q_scale = LOG2E / math.sqrt(HEAD_DIM)
    for half in range(ATTN_WIDTH // MXU_DIM):
        qn = head_norm(p[:, half * MXU_DIM:(half + 1) * MXU_DIM], gq_ref[...])
        for pair in range(MXU_DIM // LANES):
            tt = (rope(qn[:, pair * LANES:(pair + 1) * LANES]) * q_scale).T
            h0 = half * (MXU_DIM // HEAD_DIM) + pair * 2
            qt_ref[0, h0] = tt[:HEAD_DIM].astype(BF16)
            qt_ref[0, h0 + 1] = tt[HEAD_DIM:].astype(BF16)

    kn = rope(head_norm(p[:, ATTN_WIDTH:ATTN_WIDTH + KV_WIDTH], gk_ref[...]))
    k_ref[0, 0] = kn[:, :HEAD_DIM].astype(BF16)
    k_ref[0, 1] = pltpu.roll(kn, HEAD_DIM, 1)[:, :HEAD_DIM].astype(BF16)

    vvt = p[:, ATTN_WIDTH + KV_WIDTH:ATTN_WIDTH + 2 * KV_WIDTH].T
    sub = lax.broadcasted_iota(jnp.int32, (V_ROWS - HEAD_DIM, tm), 0)
    tail = jnp.where(sub == 0, 1.0, 0.0).astype(BF16)
    for kvh in range(KV_HEADS):
        vt_ref[0, kvh, :HEAD_DIM] = vvt[kvh * HEAD_DIM:(kvh + 1) * HEAD_DIM].astype(BF16)
        vt_ref[0, kvh, HEAD_DIM:] = tail

    z = p[:, ATTN_WIDTH + 2 * KV_WIDTH:]
    z = 0.5 * z * (1.0 + jnp.tanh(math.sqrt(2.0 / math.pi) * (z + 0.044715 * (z * z * z))))
    u = z[:, :MLP_WIDTH]
    v2 = z[:, MLP_WIDTH:]
    mu = jnp.mean(v2, axis=-1, keepdims=True)
    vc = v2 - mu
    var = jnp.mean(vc * vc, axis=-1, keepdims=True)
    vln = vc * lax.rsqrt(var + EPS) * gsg_ref[...] + bsg_ref[...]
    lane_w = lax.broadcasted_iota(jnp.int32, (tm, MLP_WIDTH), 1)
    even_head = (lane_w % LANES) < HEAD_DIM
    v_even = jnp.where(even_head, vln, 0.0).astype(BF16)
    v_odd = jnp.where(even_head, 0.0, vln).astype(BF16)
    for ch in range(tm // CHUNK):
        rows = slice(ch * CHUNK, (ch + 1) * CHUNK)
        for jb in range(MLP_WIDTH // LANES):
            cols = slice(jb * LANES, (jb + 1) * LANES)
            rhs = jnp.concatenate([v_even[rows, cols], v_odd[rows, cols]], axis=0)
            s = jnp.dot(ws_ref[jb], rhs, preferred_element_type=F32)
            mlp_ref[0, rows, cols] = (u[rows, cols] * (s + bs_ref[:, cols])).astype(BF16)


def _inproj(x, mods, layer, stream_row, g_pre, w_in, gq_t, gk_t, cos_t, sin_t,
            g_sg, b_sg, ws_cat, bs_full, *, tm, use_rope):
    b, n, _ = x.shape
    grid = (b, n // tm)
    if stream_row is None:
        mod_map = lambda i, j: (layer, i, 0, 0)
    else:
        mod_map = lambda i, j: (layer, stream_row, 0, 0)
    kern = functools.partial(_inproj_kernel, tm=tm, use_rope=use_rope)
    return pl.pallas_call(
        kern,
        grid=grid,
        in_specs=[
            pl.BlockSpec((1, tm, D_MODEL), lambda i, j: (i, j, 0)),
            pl.BlockSpec((1, 1, MOD_SLOTS, D_MODEL), mod_map),
            _const_spec((1, D_MODEL)),
            _const_spec((D_MODEL, IN_WIDTH)),
            _const_spec((1, MXU_DIM)),
            _const_spec((1, LANES)),
            pl.BlockSpec((tm, LANES), lambda i, j: (j, 0)),
            pl.BlockSpec((tm, LANES), lambda i, j: (j, 0)),
            _const_spec((1, MLP_WIDTH)),
            _const_spec((1, MLP_WIDTH)),
            _const_spec((MLP_WIDTH // LANES, CHUNK, 2 * CHUNK)),
            _const_spec((CHUNK, MLP_WIDTH)),
        ],
        out_specs=[
            pl.BlockSpec((1, ATTN_HEADS, HEAD_DIM, tm), lambda i, j: (i, 0, 0, j)),
            pl.BlockSpec((1, KV_HEADS, tm, HEAD_DIM), lambda i, j: (i, 0, j, 0)),
            pl.BlockSpec((1, KV_HEADS, V_ROWS, tm), lambda i, j: (i, 0, 0, j)),
            pl.BlockSpec((1, tm, MLP_WIDTH), lambda i, j: (i, j, 0)),
        ],
        out_shape=[
            jax.ShapeDtypeStruct((b, ATTN_HEADS, HEAD_DIM, n), BF16),
            jax.ShapeDtypeStruct((b, KV_HEADS, n, HEAD_DIM), BF16),
            jax.ShapeDtypeStruct((b, KV_HEADS, V_ROWS, n), BF16),
            jax.ShapeDtypeStruct((b, n, MLP_WIDTH), BF16),
        ],
        compiler_params=pltpu.CompilerParams(
            dimension_semantics=("arbitrary", "arbitrary"), vmem_limit_bytes=VMEM_LIMIT),
        name="inproj",
    )(x, mods, g_pre, w_in, gq_t, gk_t, cos_t, sin_t, g_sg, b_sg, ws_cat, bs_full)


def _attn_kernel(*refs, tq, seg_lens, bk):
    n_seg = len(seg_lens)
    qt_ref = refs[0]
    k_refs = refs[1:1 + n_seg]
    vt_refs = refs[1 + n_seg:1 + 2 * n_seg]
    o_ref = refs[1 + 2 * n_seg]

    m = [jnp.full((1, tq), -jnp.inf, F32) for _ in range(GQA_GROUP)]
    acc = [jnp.zeros((V_ROWS, tq), F32) for _ in range(GQA_GROUP)]

    def finish(h, s, vt_ref, start, size):
        m_new = jnp.maximum(m[h], jnp.max(s, axis=0, keepdims=True))
        alpha = jnp.exp2(m[h] - m_new)
        pt = jnp.exp2(s - m_new).astype(BF16)
        vt_blk = vt_ref[0, 0, :, start:start + size]
        acc[h] = alpha * acc[h] + jnp.dot(vt_blk, pt, preferred_element_type=F32)
        m[h] = m_new

    pending = []
    for k_ref, vt_ref, seg in zip(k_refs, vt_refs, seg_lens):
        for start in range(0, seg, bk):
            size = min(bk, seg - start)
            for h in range(GQA_GROUP):
                k_blk = k_ref[0, 0, start:start + size, :]
                s = jnp.dot(k_blk, qt_ref[0, h], preferred_element_type=F32)
                pending.append((h, s, vt_ref, start, size))
                if len(pending) > QK_AHEAD:
                    finish(*pending.pop(0))
    while pending:
        finish(*pending.pop(0))

    pairs = []
    for pair in range(GQA_GROUP // 2):
        ot = [acc[h][:HEAD_DIM] / acc[h][HEAD_DIM:HEAD_DIM + 1] for h in (2 * pair, 2 * pair + 1)]
        pairs.append(jnp.concatenate(ot, axis=0).T)
    o_ref[0] = jnp.concatenate(pairs, axis=1).astype(BF16)


def _attention(qt, ks, vts, *, tq, bk):
    b, _, _, n = qt.shape
    seg_lens = tuple(k.shape[2] for k in ks)
    kern = functools.partial(_attn_kernel, tq=tq, seg_lens=seg_lens, bk=bk)
    in_specs = [pl.BlockSpec((1, GQA_GROUP, HEAD_DIM, tq), lambda i, g, j: (i, g, 0, j))]
    in_specs += [pl.BlockSpec((1, 1, s, HEAD_DIM), lambda i, g, j: (i, g, 0, 0)) for s in seg_lens]
    in_specs += [pl.BlockSpec((1, 1, V_ROWS, s), lambda i, g, j: (i, g, 0, 0)) for s in seg_lens]
    return pl.pallas_call(
        kern,
        grid=(b, KV_HEADS, n // tq),
        in_specs=in_specs,
        out_specs=pl.BlockSpec((1, tq, GQA_GROUP * HEAD_DIM), lambda i, g, j: (i, j, g)),
        out_shape=jax.ShapeDtypeStruct((b, n, ATTN_WIDTH), BF16),
        compiler_params=pltpu.CompilerParams(
            dimension_semantics=("arbitrary", "arbitrary", "arbitrary"), vmem_limit_bytes=VMEM_LIMIT),
        name="attention",
    )(qt, *ks, *vts)


def _outffn_kernel(attn_ref, mlp_ref, x_ref, mod_ref, gpm_ref, gpf_ref, gqf_ref,
                   wo_ref, wgu_ref, wd_ref, o_ref, acc_ref):
    mod = mod_ref[0, 0]
    gate_mix, shift, scale, gate_ffn = mod[2:3], mod[3:4], mod[4:5], mod[5:6]
    out = (jnp.dot(attn_ref[0], wo_ref[:ATTN_WIDTH], preferred_element_type=F32)
           + jnp.dot(mlp_ref[0], wo_ref[ATTN_WIDTH:], preferred_element_type=F32))
    x1 = x_ref[0] + gate_mix * _rms(out, gpm_ref[...])
    h = (_rms(x1, gpf_ref[...]) * (1.0 + scale) + shift).astype(BF16)

    acc_ref[...] = jnp.zeros_like(acc_ref)

    def body(ci, carry):
        gu = jnp.dot(h, wgu_ref[ci], preferred_element_type=F32)
        g = gu[:, :FFN_CHUNK]
        a = (g * jax.nn.sigmoid(g) * gu[:, FFN_CHUNK:]).astype(BF16)
        acc_ref[...] += jnp.dot(a, wd_ref[ci], preferred_element_type=F32)
        return carry

    lax.fori_loop(0, N_FFN_CHUNKS, body, 0)
    o_ref[0] = x1 + gate_ffn * _rms(acc_ref[...], gqf_ref[...])


def _outffn(attn, mlp, x, mods, layer, stream_row, g_post_mix, g_pre_ffn, g_post_ffn,
            w_out, w_gu, w_down, *, tm):
    b, n, _ = x.shape
    if stream_row is None:
        mod_map = lambda i, j: (layer, i, 0, 0)
    else:
        mod_map = lambda i, j: (layer, stream_row, 0, 0)
    row_spec = lambda w: pl.BlockSpec((1, tm, w), lambda i, j: (i, j, 0))
    return pl.pallas_call(
        _outffn_kernel,
        grid=(b, n // tm),
        in_specs=[
            row_spec(ATTN_WIDTH),
            row_spec(MLP_WIDTH),
            row_spec(D_MODEL),
            pl.BlockSpec((1, 1, MOD_SLOTS, D_MODEL), mod_map),
            _const_spec((1, D_MODEL)),
            _const_spec((1, D_MODEL)),
            _const_spec((1, D_MODEL)),
            _const_spec((ATTN_WIDTH + MLP_WIDTH, D_MODEL)),
            _const_spec((N_FFN_CHUNKS, D_MODEL, 2 * FFN_CHUNK)),
            _const_spec((N_FFN_CHUNKS, FFN_CHUNK, D_MODEL)),
        ],
        out_specs=row_spec(D_MODEL),
        out_shape=jax.ShapeDtypeStruct((b, n, D_MODEL), F32),
        scratch_shapes=[pltpu.VMEM((tm, D_MODEL), F32)],
        compiler_params=pltpu.CompilerParams(
            dimension_semantics=("arbitrary", "arbitrary"), vmem_limit_bytes=VMEM_LIMIT),
        name="outffn",
    )(attn, mlp, x, mods, g_post_mix, g_pre_ffn, g_post_ffn, w_out, w_gu, w_down)


def _rope_tables(n):
    rows = n // GRID_W
    pos_row = jnp.broadcast_to(jnp.arange(rows, dtype=F32)[:, None], (rows, GRID_W)).reshape(-1)
    pos_col = jnp.broadcast_to(jnp.arange(GRID_W, dtype=F32)[None, :], (rows, GRID_W)).reshape(-1)
    inv = ROPE_THETA ** (-jnp.arange(0, ROPE_AXIS_DIM, 2, dtype=F32) / ROPE_AXIS_DIM)
    ang_r = pos_row[:, None] * inv
    ang_c = pos_col[:, None] * inv
    cos64 = jnp.concatenate([jnp.cos(ang_r)] * 2 + [jnp.cos(ang_c)] * 2, axis=-1)
    sin64 = jnp.concatenate([-jnp.sin(ang_r), jnp.sin(ang_r), -jnp.sin(ang_c), jnp.sin(ang_c)], axis=-1)
    return jnp.tile(cos64, (1, 2)), jnp.tile(sin64, (1, 2))


def kernel(x, c, ctx, c_ctx, w_mod, b_mod, g_pre_mix, g_post_mix, g_pre_ffn, g_post_ffn,
           w_in, g_q, g_k, g_sg, b_sg, w_s, b_s, w_out, w_ffn_in, w_ffn_out):
    b, n, _ = x.shape
    n_ctx = ctx.shape[1]
    cos_t, sin_t = _rope_tables(n)
    cos_c, sin_c = cos_t[:n_ctx], sin_t[:n_ctx]

    cvec = jnp.concatenate([c, c_ctx[None], jnp.zeros((MOD_ROWS - b - 1, D_MODEL), F32)], axis=0)
    mods = _modulation(cvec, w_mod, b_mod)
    mods = mods.reshape(DEPTH, MOD_ROWS, N_MOD, D_MODEL)
    mods = jnp.pad(mods, ((0, 0), (0, 0), (0, MOD_SLOTS - N_MOD), (0, 0)))

    w_in_b = w_in.astype(BF16)
    w_out_b = w_out.astype(BF16)
    w_gate, w_up = w_ffn_in[..., :FFN_HIDDEN], w_ffn_in[..., FFN_HIDDEN:]
    w_gu = jnp.concatenate(
        [w_gate.reshape(DEPTH, D_MODEL, N_FFN_CHUNKS, FFN_CHUNK),
         w_up.reshape(DEPTH, D_MODEL, N_FFN_CHUNKS, FFN_CHUNK)], axis=-1)
    w_gu = w_gu.transpose(0, 2, 1, 3).astype(BF16)
    w_down = w_ffn_out.reshape(DEPTH, N_FFN_CHUNKS, FFN_CHUNK, D_MODEL).astype(BF16)
    ws_cat = w_s.reshape(DEPTH, MLP_HEADS // 2, 2, CHUNK, CHUNK).transpose(0, 1, 3, 2, 4)
    ws_cat = ws_cat.reshape(DEPTH, MLP_HEADS // 2, CHUNK, 2 * CHUNK).astype(BF16)
    bs_full = jnp.repeat(b_s.transpose(0, 2, 1), HEAD_DIM, axis=2)
    gq_t = jnp.tile(g_q, (1, MXU_DIM // HEAD_DIM))[:, None]
    gk_t = jnp.tile(g_k, (1, LANES // HEAD_DIM))[:, None]

    xc = ctx
    for l in range(DEPTH):
        last = l == DEPTH - 1
        in_args = (g_pre_mix[l][None], w_in_b[l], gq_t[l], gk_t[l])
        mlp_args = (g_sg[l][None], b_sg[l][None], ws_cat[l], bs_full[l])
        ffn_args = (g_post_mix[l][None], g_pre_ffn[l][None], g_post_ffn[l][None],
                    w_out_b[l], w_gu[l], w_down[l])

        qct, kc, vct, mlp_c = _inproj(xc, mods, l, b, *in_args, cos_c, sin_c, *mlp_args,
                                      tm=n_ctx, use_rope=False)
        qxt, kx, vxt, mlp_x = _inproj(x, mods, l, None, *in_args, cos_t, sin_t, *mlp_args,
                                      tm=512, use_rope=True)
        attn_x = _attention(qxt, (kc, kx), (vct, vxt), tq=256, bk=512)
        x = _outffn(attn_x, mlp_x, x, mods, l, None, *ffn_args, tm=512)
        if not last:
            attn_c = _attention(qct, (kc,), (vct,), tq=n_ctx, bk=512)
            xc = _outffn(attn_c, mlp_c, xc, mods, l, b, *ffn_args, tm=n_ctx)
    return x
```

```python
import functools
import math

import jax
import jax.numpy as jnp
from jax import lax
from jax.experimental import pallas as pl
from jax.experimental.pallas import tpu as pltpu

D_MODEL = 1024
DEPTH = 4
GRID_W = 64
HEAD_DIM = 64
ATTN_HEADS = 8
KV_HEADS = 2
GQA_GROUP = ATTN_HEADS // KV_HEADS
ATTN_WIDTH = ATTN_HEADS * HEAD_DIM
KV_WIDTH = KV_HEADS * HEAD_DIM
MLP_HEADS = 8
MLP_WIDTH = MLP_HEADS * HEAD_DIM
CHUNK = 128
IN_WIDTH = ATTN_WIDTH + 2 * KV_WIDTH + 2 * MLP_WIDTH
FFN_HIDDEN = 2816
N_MOD = 6
ROPE_THETA = 10000.0
ROPE_AXIS_DIM = HEAD_DIM // 2
EPS = 1e-6

LANES = 128
MXU_DIM = 256
MOD_ROWS = 16
MOD_SLOTS = 8
V_ROWS = HEAD_DIM + 16
FFN_CHUNK = MXU_DIM
N_FFN_CHUNKS = FFN_HIDDEN // FFN_CHUNK
SUB_ROWS = MXU_DIM
PROJ_AHEAD = 2
FFN_AHEAD = 2
QK_AHEAD = 4
LOG2E = math.log2(math.e)
VMEM_LIMIT = 56 * 1024 * 1024

F32 = jnp.float32
BF16 = jnp.bfloat16


def _const_spec(shape):
    zeros = (0,) * len(shape)
    return pl.BlockSpec(shape, lambda *_: zeros, pipeline_mode=pl.Buffered(1))


def _rms(t, g):
    return t * lax.rsqrt(jnp.mean(t * t, axis=-1, keepdims=True) + EPS) * g


def _mod_kernel(c_ref, w_ref, b_ref, o_ref):
    cv = c_ref[...]
    act = (cv * jax.nn.sigmoid(cv)).astype(BF16)
    o_ref[0] = jnp.dot(act, w_ref[0].astype(BF16), preferred_element_type=F32) + b_ref[0]


def _modulation(cvec, w_mod, b_mod):
    tn = 1536
    n_out = N_MOD * D_MODEL
    return pl.pallas_call(
        _mod_kernel,
        grid=(DEPTH, n_out // tn),
        in_specs=[
            pl.BlockSpec((MOD_ROWS, D_MODEL), lambda l, j: (0, 0)),
            pl.BlockSpec((1, D_MODEL, tn), lambda l, j: (l, 0, j)),
            pl.BlockSpec((1, 1, tn), lambda l, j: (l, 0, j)),
        ],
        out_specs=pl.BlockSpec((1, MOD_ROWS, tn), lambda l, j: (l, 0, j)),
        out_shape=jax.ShapeDtypeStruct((DEPTH, MOD_ROWS, n_out), F32),
        compiler_params=pltpu.CompilerParams(
            dimension_semantics=("arbitrary", "arbitrary"), vmem_limit_bytes=VMEM_LIMIT),
        name="modulation",
    )(cvec, w_mod, b_mod.reshape(DEPTH, 1, n_out))


def _group_sum(t2, ones_bd):
    return jnp.dot(t2.astype(BF16), ones_bd, preferred_element_type=F32)


def _swap_halves16(t):
    lane = lax.broadcasted_iota(jnp.int32, t.shape, 1)
    return jnp.where((lane % 32) < 16, pltpu.roll(t, LANES - 16, 1), pltpu.roll(t, 16, 1))


def _inproj_kernel(x_ref, mod_ref, gpre_ref, w_ref, gq_ref, gk_ref, cos_ref, sin_ref,
                   gsg_ref, bsg_ref, ws_ref, bs_ref,
                   qt_ref, k_ref, vt_ref, mlp_ref, *, tm, use_rope):
    mod = mod_ref[0, 0]
    shift, scale = mod[0:1], mod[1:2]
    ts = CHUNK

    ri = lax.broadcasted_iota(jnp.int32, (MXU_DIM, MXU_DIM), 0) // HEAD_DIM
    ci = lax.broadcasted_iota(jnp.int32, (MXU_DIM, MXU_DIM), 1) // HEAD_DIM
    ones_bd = jnp.where(ri == ci, 1.0, 0.0).astype(BF16)

    def project(r):
        h = (_rms(x_ref[0, r], gpre_ref[...]) * (1.0 + scale) + shift).astype(BF16)
        return jnp.dot(h, w_ref[...], preferred_element_type=F32)

    def head_norm(t, g):
        w = t.shape[1]
        ss = _group_sum(t * t, ones_bd[:w, :w])
        return t * lax.rsqrt(ss * (1.0 / HEAD_DIM) + EPS) * g

    def finish(r, p):
        def rope(t):
            if not use_rope:
                return t
            return t * cos_ref[r, :] + _swap_halves16(t) * sin_ref[r, :]

        q_scale = LOG2E / math.sqrt(HEAD_DIM)
        for half in range(ATTN_WIDTH // MXU_DIM):
            qn = head_norm(p[:, half * MXU_DIM:(half + 1) * MXU_DIM], gq_ref[...])
            for pair in range(MXU_DIM // LANES):
                tt = (rope(qn[:, pair * LANES:(pair + 1) * LANES]) * q_scale).T
                h0 = half * (MXU_DIM // HEAD_DIM) + pair * 2
                qt_ref[0, h0, :, r] = tt[:HEAD_DIM].astype(BF16)
                qt_ref[0, h0 + 1, :, r] = tt[HEAD_DIM:].astype(BF16)

        kn = rope(head_norm(p[:, ATTN_WIDTH:ATTN_WIDTH + KV_WIDTH], gk_ref[...]))
        k_ref[0, 0, r, :] = kn[:, :HEAD_DIM].astype(BF16)
        k_ref[0, 1, r, :] = pltpu.roll(kn, HEAD_DIM, 1)[:, :HEAD_DIM].astype(BF16)

        vvt = p[:, ATTN_WIDTH + KV_WIDTH:ATTN_WIDTH + 2 * KV_WIDTH].T
        sub = lax.broadcasted_iota(jnp.int32, (V_ROWS - HEAD_DIM, ts), 0)
        tail = jnp.where(sub == 0, 1.0, 0.0).astype(BF16)
        for kvh in range(KV_HEADS):
            vt_ref[0, kvh, :HEAD_DIM, r] = vvt[kvh * HEAD_DIM:(kvh + 1) * HEAD_DIM].astype(BF16)
            vt_ref[0, kvh, HEAD_DIM:, r] = tail

        z = p[:, ATTN_WIDTH + 2 * KV_WIDTH:]
        gc = math.sqrt(2.0 / math.pi)
        hz = 0.5 * z
        z = hz + hz * jnp.tanh(z * (gc + (gc * 0.044715) * (z * z)))
        u = z[:, :MLP_WIDTH]
        v2 = z[:, MLP_WIDTH:]
        mu = jnp.mean(v2, axis=-1, keepdims=True)
        vc = v2 - mu
        var = jnp.mean(vc * vc, axis=-1, keepdims=True)
        vln = vc * lax.rsqrt(var + EPS) * gsg_ref[...] + bsg_ref[...]
        lane_w = lax.broadcasted_iota(jnp.int32, (ts, MLP_WIDTH), 1)
        even_head = (lane_w % LANES) < HEAD_DIM
        v_even = jnp.where(even_head, vln, 0.0).astype(BF16)
        v_odd = jnp.where(even_head, 0.0, vln).astype(BF16)
        for ch in range(ts // CHUNK):
            rows = slice(ch * CHUNK, (ch + 1) * CHUNK)
            out_rows = slice(r.start + ch * CHUNK, r.start + (ch + 1) * CHUNK)
            for jb in range(MLP_WIDTH // LANES):
                cols = slice(jb * LANES, (jb + 1) * LANES)
                rhs = jnp.concatenate([v_even[rows, cols], v_odd[rows, cols]], axis=0)
                s = jnp.dot(ws_ref[jb], rhs, preferred_element_type=F32)
                mlp_ref[0, out_rows, cols] = (u[rows, cols] * (s + bs_ref[:, cols])).astype(BF16)

    subs = [slice(i, i + ts) for i in range(0, tm, ts)]
    pending = [project(r) for r in subs[:PROJ_AHEAD]]
    for i, r in enumerate(subs):
        p_cur = pending.pop(0)
        if i + PROJ_AHEAD < len(subs):
            pending.append(project(subs[i + PROJ_AHEAD]))
        finish(r, p_cur)


def _inproj(x, mods, layer, stream_row, g_pre, w_in, gq_t, gk_t, cos_t, sin_t,
            g_sg, b_sg, ws_cat, bs_full, *, tm, use_rope):
    b, n, _ = x.shape
    grid = (b, n // tm)
    if stream_row is None:
        mod_map = lambda i, j: (layer, i, 0, 0)
    else:
        mod_map = lambda i, j: (layer, stream_row, 0, 0)
    kern = functools.partial(_inproj_kernel, tm=tm, use_rope=use_rope)
    return pl.pallas_call(
        kern,
        grid=grid,
        in_specs=[
            pl.BlockSpec((1, tm, D_MODEL), lambda i, j: (i, j, 0)),
            pl.BlockSpec((1, 1, MOD_SLOTS, D_MODEL), mod_map),
            _const_spec((1, D_MODEL)),
            _const_spec((D_MODEL, IN_WIDTH)),
            _const_spec((1, MXU_DIM)),
            _const_spec((1, LANES)),
            pl.BlockSpec((tm, LANES), lambda i, j: (j, 0)),
            pl.BlockSpec((tm, LANES), lambda i, j: (j, 0)),
            _const_spec((1, MLP_WIDTH)),
            _const_spec((1, MLP_WIDTH)),
            _const_spec((MLP_WIDTH // LANES, CHUNK, 2 * CHUNK)),
            _const_spec((CHUNK, MLP_WIDTH)),
        ],
        out_specs=[
            pl.BlockSpec((1, ATTN_HEADS, HEAD_DIM, tm), lambda i, j: (i, 0, 0, j)),
            pl.BlockSpec((1, KV_HEADS, tm, HEAD_DIM), lambda i, j: (i, 0, j, 0)),
            pl.BlockSpec((1, KV_HEADS, V_ROWS, tm), lambda i, j: (i, 0, 0, j)),
            pl.BlockSpec((1, tm, MLP_WIDTH), lambda i, j: (i, j, 0)),
        ],
        out_shape=[
            jax.ShapeDtypeStruct((b, ATTN_HEADS, HEAD_DIM, n), BF16),
            jax.ShapeDtypeStruct((b, KV_HEADS, n, HEAD_DIM), BF16),
            jax.ShapeDtypeStruct((b, KV_HEADS, V_ROWS, n), BF16),
            jax.ShapeDtypeStruct((b, n, MLP_WIDTH), BF16),
        ],
        compiler_params=pltpu.CompilerParams(
            dimension_semantics=("arbitrary", "arbitrary"), vmem_limit_bytes=VMEM_LIMIT),
        name="inproj",
    )(x, mods, g_pre, w_in, gq_t, gk_t, cos_t, sin_t, g_sg, b_sg, ws_cat, bs_full)


def _attn_kernel(*refs, tq, seg_lens, bk):
    n_seg = len(seg_lens)
    qt_ref = refs[0]
    k_refs = refs[1:1 + n_seg]
    vt_refs = refs[1 + n_seg:1 + 2 * n_seg]
    o_ref = refs[1 + 2 * n_seg]

    m = [jnp.full((1, tq), -jnp.inf, F32) for _ in range(GQA_GROUP)]
    acc = [jnp.zeros((V_ROWS, tq), F32) for _ in range(GQA_GROUP)]

    def finish(h, s, vt_ref, start, size):
        m_new = jnp.maximum(m[h], jnp.max(s, axis=0, keepdims=True))
        alpha = jnp.exp2(m[h] - m_new)
        pt = jnp.exp2(s - m_new).astype(BF16)
        vt_blk = vt_ref[0, 0, :, start:start + size]
        acc[h] = alpha * acc[h] + jnp.dot(vt_blk, pt, preferred_element_type=F32)
        m[h] = m_new

    pending = []
    for k_ref, vt_ref, seg in zip(k_refs, vt_refs, seg_lens):
        for start in range(0, seg, bk):
            size = min(bk, seg - start)
            for h in range(GQA_GROUP):
                k_blk = k_ref[0, 0, start:start + size, :]
                s = jnp.dot(k_blk, qt_ref[0, h], preferred_element_type=F32)
                pending.append((h, s, vt_ref, start, size))
                if len(pending) > QK_AHEAD:
                    finish(*pending.pop(0))
    while pending:
        finish(*pending.pop(0))

    pairs = []
    for pair in range(GQA_GROUP // 2):
        ot = [acc[h][:HEAD_DIM] / acc[h][HEAD_DIM:HEAD_DIM + 1] for h in (2 * pair, 2 * pair + 1)]
        pairs.append(jnp.concatenate(ot, axis=0).T)
    o_ref[0] = jnp.concatenate(pairs, axis=1).astype(BF16)


def _attention(qt, ks, vts, *, tq, bk):
    b, _, _, n = qt.shape
    seg_lens = tuple(k.shape[2] for k in ks)
    kern = functools.partial(_attn_kernel, tq=tq, seg_lens=seg_lens, bk=bk)
    in_specs = [pl.BlockSpec((1, GQA_GROUP, HEAD_DIM, tq), lambda i, g, j: (i, g, 0, j))]
    in_specs += [pl.BlockSpec((1, 1, s, HEAD_DIM), lambda i, g, j: (i, g, 0, 0)) for s in seg_lens]
    in_specs += [pl.BlockSpec((1, 1, V_ROWS, s), lambda i, g, j: (i, g, 0, 0)) for s in seg_lens]
    return pl.pallas_call(
        kern,
        grid=(b, KV_HEADS, n // tq),
        in_specs=in_specs,
        out_specs=pl.BlockSpec((1, tq, GQA_GROUP * HEAD_DIM), lambda i, g, j: (i, j, g)),
        out_shape=jax.ShapeDtypeStruct((b, n, ATTN_WIDTH), BF16),
        compiler_params=pltpu.CompilerParams(
            dimension_semantics=("arbitrary", "arbitrary", "arbitrary"), vmem_limit_bytes=VMEM_LIMIT),
        name="attention",
    )(qt, *ks, *vts)


def _outffn_kernel(attn_ref, mlp_ref, x_ref, mod_ref, gpm_ref, gpf_ref, gqf_ref,
                   wo_ref, wgu_ref, wd_ref, o_ref, *, tm):
    mod = mod_ref[0, 0]
    gate_mix, shift, scale, gate_ffn = mod[2:3], mod[3:4], mod[4:5], mod[5:6]

    ts = min(tm, SUB_ROWS)
    subs = [slice(i, i + ts) for i in range(0, tm, ts)]
    outs = [jnp.dot(attn_ref[0, r], wo_ref[:ATTN_WIDTH], preferred_element_type=F32)
            + jnp.dot(mlp_ref[0, r], wo_ref[ATTN_WIDTH:], preferred_element_type=F32) for r in subs]

    def ffn(h):
        def gate_up(ci):
            return jnp.dot(h, wgu_ref[ci], preferred_element_type=F32)

        pending = [gate_up(ci) for ci in range(FFN_AHEAD)]
        y = None
        for ci in range(N_FFN_CHUNKS):
            gu = pending.pop(0)
            if ci + FFN_AHEAD < N_FFN_CHUNKS:
                pending.append(gate_up(ci + FFN_AHEAD))
            g = gu[:, :FFN_CHUNK]
            a = (g * jax.nn.sigmoid(g) * gu[:, FFN_CHUNK:]).astype(BF16)
            d = jnp.dot(a, wd_ref[ci], preferred_element_type=F32)
            y = d if y is None else y + d
        return y

    for r, out in zip(subs, outs):
        x1 = x_ref[0, r] + gate_mix * _rms(out, gpm_ref[...])
        h = (_rms(x1, gpf_ref[...]) * (1.0 + scale) + shift).astype(BF16)
        o_ref[0, r] = x1 + gate_ffn * _rms(ffn(h), gqf_ref[...])


def _outffn(attn, mlp, x, mods, layer, stream_row, g_post_mix, g_pre_ffn, g_post_ffn,
            w_out, w_gu, w_down, *, tm):
    b, n, _ = x.shape
    if stream_row is None:
        mod_map = lambda i, j: (layer, i, 0, 0)
    else:
        mod_map = lambda i, j: (layer, stream_row, 0, 0)
    row_spec = lambda w: pl.BlockSpec((1, tm, w), lambda i, j: (i, j, 0))
    return pl.pallas_call(
        functools.partial(_outffn_kernel, tm=tm),
        grid=(b, n // tm),
        in_specs=[
            row_spec(ATTN_WIDTH),
            row_spec(MLP_WIDTH),
            row_spec(D_MODEL),
            pl.BlockSpec((1, 1, MOD_SLOTS, D_MODEL), mod_map),
            _const_spec((1, D_MODEL)),
            _const_spec((1, D_MODEL)),
            _const_spec((1, D_MODEL)),
            _const_spec((ATTN_WIDTH + MLP_WIDTH, D_MODEL)),
            _const_spec((N_FFN_CHUNKS, D_MODEL, 2 * FFN_CHUNK)),
            _const_spec((N_FFN_CHUNKS, FFN_CHUNK, D_MODEL)),
        ],
        out_specs=row_spec(D_MODEL),
        out_shape=jax.ShapeDtypeStruct((b, n, D_MODEL), F32),
        compiler_params=pltpu.CompilerParams(
            dimension_semantics=("arbitrary", "arbitrary"), vmem_limit_bytes=VMEM_LIMIT),
        name="outffn",
    )(attn, mlp, x, mods, g_post_mix, g_pre_ffn, g_post_ffn, w_out, w_gu, w_down)


def _rope_tables(n):
    rows = n // GRID_W
    pos_row = jnp.broadcast_to(jnp.arange(rows, dtype=F32)[:, None], (rows, GRID_W)).reshape(-1)
    pos_col = jnp.broadcast_to(jnp.arange(GRID_W, dtype=F32)[None, :], (rows, GRID_W)).reshape(-1)
    inv = ROPE_THETA ** (-jnp.arange(0, ROPE_AXIS_DIM, 2, dtype=F32) / ROPE_AXIS_DIM)
    ang_r = pos_row[:, None] * inv
    ang_c = pos_col[:, None] * inv
    cos64 = jnp.concatenate([jnp.cos(ang_r)] * 2 + [jnp.cos(ang_c)] * 2, axis=-1)
    sin64 = jnp.concatenate([-jnp.sin(ang_r), jnp.sin(ang_r), -jnp.sin(ang_c), jnp.sin(ang_c)], axis=-1)
    return jnp.tile(cos64, (1, 2)), jnp.tile(sin64, (1, 2))


def kernel(x, c, ctx, c_ctx, w_mod, b_mod, g_pre_mix, g_post_mix, g_pre_ffn, g_post_ffn,
           w_in, g_q, g_k, g_sg, b_sg, w_s, b_s, w_out, w_ffn_in, w_ffn_out):
    b, n, _ = x.shape
    n_ctx = ctx.shape[1]
    cos_t, sin_t = _rope_tables(n)
    cos_c, sin_c = cos_t[:n_ctx], sin_t[:n_ctx]

    cvec = jnp.concatenate([c, c_ctx[None], jnp.zeros((MOD_ROWS - b - 1, D_MODEL), F32)], axis=0)
    mods = _modulation(cvec, w_mod, b_mod)
    mods = mods.reshape(DEPTH, MOD_ROWS, N_MOD, D_MODEL)
    mods = jnp.pad(mods, ((0, 0), (0, 0), (0, MOD_SLOTS - N_MOD), (0, 0)))

    w_in_b = w_in.astype(BF16)
    w_out_b = w_out.astype(BF16)
    w_gate, w_up = w_ffn_in[..., :FFN_HIDDEN], w_ffn_in[..., FFN_HIDDEN:]
    w_gu = jnp.concatenate(
        [w_gate.reshape(DEPTH, D_MODEL, N_FFN_CHUNKS, FFN_CHUNK),
         w_up.reshape(DEPTH, D_MODEL, N_FFN_CHUNKS, FFN_CHUNK)], axis=-1)
    w_gu = w_gu.transpose(0, 2, 1, 3).astype(BF16)
    w_down = w_ffn_out.reshape(DEPTH, N_FFN_CHUNKS, FFN_CHUNK, D_MODEL).astype(BF16)
    ws_cat = w_s.reshape(DEPTH, MLP_HEADS // 2, 2, CHUNK, CHUNK).transpose(0, 1, 3, 2, 4)
    ws_cat = ws_cat.reshape(DEPTH, MLP_HEADS // 2, CHUNK, 2 * CHUNK).astype(BF16)
    bs_full = jnp.repeat(b_s.transpose(0, 2, 1), HEAD_DIM, axis=2)
    gq_t = jnp.tile(g_q, (1, MXU_DIM // HEAD_DIM))[:, None]
    gk_t = jnp.tile(g_k, (1, LANES // HEAD_DIM))[:, None]

    xc = ctx
    for l in range(DEPTH):
        last = l == DEPTH - 1
        in_args = (g_pre_mix[l][None], w_in_b[l], gq_t[l], gk_t[l])
        mlp_args = (g_sg[l][None], b_sg[l][None], ws_cat[l], bs_full[l])
        ffn_args = (g_post_mix[l][None], g_pre_ffn[l][None], g_post_ffn[l][None],
                    w_out_b[l], w_gu[l], w_down[l])

        qct, kc, vct, mlp_c = _inproj(xc, mods, l, b, *in_args, cos_c, sin_c, *mlp_args,
                                      tm=n_ctx, use_rope=False)
        qxt, kx, vxt, mlp_x = _inproj(x, mods, l, None, *in_args, cos_t, sin_t, *mlp_args,
                                      tm=1024, use_rope=True)
        attn_x = _attention(qxt, (kc, kx), (vct, vxt), tq=256, bk=512)
        x = _outffn(attn_x, mlp_x, x, mods, l, None, *ffn_args, tm=512)
        if not last:
            attn_c = _attention(qct, (kc,), (vct,), tq=n_ctx, bk=512)
            xc = _outffn(attn_c, mlp_c, xc, mods, l, b, *ffn_args, tm=n_ctx)
    return x
```

```python
import functools
import math

import jax
import jax.numpy as jnp
from jax import lax
from jax.experimental import pallas as pl
from jax.experimental.pallas import tpu as pltpu

D_MODEL = 1024
DEPTH = 4
GRID_W = 64
HEAD_DIM = 64
ATTN_HEADS = 8
KV_HEADS = 2
GQA_GROUP = ATTN_HEADS // KV_HEADS
ATTN_WIDTH = ATTN_HEADS * HEAD_DIM
KV_WIDTH = KV_HEADS * HEAD_DIM
MLP_HEADS = 8
MLP_WIDTH = MLP_HEADS * HEAD_DIM
CHUNK = 128
IN_WIDTH = ATTN_WIDTH + 2 * KV_WIDTH + 2 * MLP_WIDTH
FFN_HIDDEN = 2816
N_MOD = 6
ROPE_THETA = 10000.0
ROPE_AXIS_DIM = HEAD_DIM // 2
EPS = 1e-6

LANES = 128
MXU_DIM = 256
MOD_ROWS = 16
MOD_SLOTS = 8
V_ROWS = HEAD_DIM + 16
FFN_CHUNK = MXU_DIM
N_FFN_CHUNKS = FFN_HIDDEN // FFN_CHUNK
SUB_ROWS = MXU_DIM
PROJ_AHEAD = 2
FFN_AHEAD = 2
QK_AHEAD = 2
LOG2E = math.log2(math.e)
BOUND_SLACK = 1.02
SAFE_EXP2_RANGE = 120.0
VMEM_LIMIT = 56 * 1024 * 1024

F32 = jnp.float32
BF16 = jnp.bfloat16


def _const_spec(shape):
    zeros = (0,) * len(shape)
    return pl.BlockSpec(shape, lambda *_: zeros, pipeline_mode=pl.Buffered(1))


def _rms(t, g):
    return t * lax.rsqrt(jnp.mean(t * t, axis=-1, keepdims=True) + EPS) * g


def _mod_kernel(c_ref, w_ref, b_ref, o_ref):
    cv = c_ref[...]
    act = (cv * jax.nn.sigmoid(cv)).astype(BF16)
    o_ref[0] = jnp.dot(act, w_ref[0].astype(BF16), preferred_element_type=F32) + b_ref[0]


def _modulation(cvec, w_mod, b_mod):
    tn = 1536
    n_out = N_MOD * D_MODEL
    return pl.pallas_call(
        _mod_kernel,
        grid=(DEPTH, n_out // tn),
        in_specs=[
            pl.BlockSpec((MOD_ROWS, D_MODEL), lambda l, j: (0, 0)),
            pl.BlockSpec((1, D_MODEL, tn), lambda l, j: (l, 0, j)),
            pl.BlockSpec((1, 1, tn), lambda l, j: (l, 0, j)),
        ],
        out_specs=pl.BlockSpec((1, MOD_ROWS, tn), lambda l, j: (l, 0, j)),
        out_shape=jax.ShapeDtypeStruct((DEPTH, MOD_ROWS, n_out), F32),
        compiler_params=pltpu.CompilerParams(
            dimension_semantics=("arbitrary", "arbitrary"), vmem_limit_bytes=VMEM_LIMIT),
        name="modulation",
    )(cvec, w_mod, b_mod.reshape(DEPTH, 1, n_out))


def _group_sum(t2, ones_bd):
    return jnp.dot(t2.astype(BF16), ones_bd, preferred_element_type=F32)


def _swap_halves16(t):
    lane = lax.broadcasted_iota(jnp.int32, t.shape, 1)
    return jnp.where((lane % 32) < 16, pltpu.roll(t, LANES - 16, 1), pltpu.roll(t, 16, 1))


def _inproj_kernel(x_ref, mod_ref, gpre_ref, w_ref, gq_ref, gk_ref, cos_ref, sin_ref,
                   gsg_ref, bsg_ref, ws_ref, bs_ref,
                   qt_ref, k_ref, vt_ref, mlp_ref, *, tm, use_rope):
    mod = mod_ref[0, 0]
    shift, scale = mod[0:1], mod[1:2]
    ts = CHUNK

    ri = lax.broadcasted_iota(jnp.int32, (MXU_DIM, MXU_DIM), 0) // HEAD_DIM
    ci = lax.broadcasted_iota(jnp.int32, (MXU_DIM, MXU_DIM), 1) // HEAD_DIM
    ones_bd = jnp.where(ri == ci, 1.0, 0.0).astype(BF16)

    def project(r):
        h = (_rms(x_ref[0, r], gpre_ref[...]) * (1.0 + scale) + shift).astype(BF16)
        return jnp.dot(h, w_ref[...], preferred_element_type=F32)

    def head_norm(t, g):
        w = t.shape[1]
        ss = _group_sum(t * t, ones_bd[:w, :w])
        return t * lax.rsqrt(ss * (1.0 / HEAD_DIM) + EPS) * g

    def finish(r, p):
        def rope(t):
            if not use_rope:
                return t
            return t * cos_ref[r, :] + _swap_halves16(t) * sin_ref[r, :]

        q_scale = LOG2E / math.sqrt(HEAD_DIM)
        for half in range(ATTN_WIDTH // MXU_DIM):
            qn = head_norm(p[:, half * MXU_DIM:(half + 1) * MXU_DIM], gq_ref[...])
            for pair in range(MXU_DIM // LANES):
                tt = (rope(qn[:, pair * LANES:(pair + 1) * LANES]) * q_scale).T
                h0 = half * (MXU_DIM // HEAD_DIM) + pair * 2
                qt_ref[0, h0, :, r] = tt[:HEAD_DIM].astype(BF16)
                qt_ref[0, h0 + 1, :, r] = tt[HEAD_DIM:].astype(BF16)

        kn = rope(head_norm(p[:, ATTN_WIDTH:ATTN_WIDTH + KV_WIDTH], gk_ref[...]))
        k_ref[0, 0, r, :] = kn[:, :HEAD_DIM].astype(BF16)
        k_ref[0, 1, r, :] = pltpu.roll(kn, HEAD_DIM, 1)[:, :HEAD_DIM].astype(BF16)

        vvt = p[:, ATTN_WIDTH + KV_WIDTH:ATTN_WIDTH + 2 * KV_WIDTH].T
        sub = lax.broadcasted_iota(jnp.int32, (V_ROWS - HEAD_DIM, ts), 0)
        tail = jnp.where(sub == 0, 1.0, 0.0).astype(BF16)
        for kvh in range(KV_HEADS):
            vt_ref[0, kvh, :HEAD_DIM, r] = vvt[kvh * HEAD_DIM:(kvh + 1) * HEAD_DIM].astype(BF16)
            vt_ref[0, kvh, HEAD_DIM:, r] = tail

        z = p[:, ATTN_WIDTH + 2 * KV_WIDTH:]
        gc = math.sqrt(2.0 / math.pi)
        hz = 0.5 * z
        z = hz + hz * jnp.tanh(z * (gc + (gc * 0.044715) * (z * z)))
        u = z[:, :MLP_WIDTH]
        v2 = z[:, MLP_WIDTH:]
        mu = jnp.mean(v2, axis=-1, keepdims=True)
        vc = v2 - mu
        var = jnp.mean(vc * vc, axis=-1, keepdims=True)
        vln = vc * lax.rsqrt(var + EPS) * gsg_ref[...] + bsg_ref[...]
        lane_w = lax.broadcasted_iota(jnp.int32, (ts, MLP_WIDTH), 1)
        even_head = (lane_w % LANES) < HEAD_DIM
        v_even = jnp.where(even_head, vln, 0.0).astype(BF16)
        v_odd = jnp.where(even_head, 0.0, vln).astype(BF16)
        for ch in range(ts // CHUNK):
            rows = slice(ch * CHUNK, (ch + 1) * CHUNK)
            out_rows = slice(r.start + ch * CHUNK, r.start + (ch + 1) * CHUNK)
            for jb in range(MLP_WIDTH // LANES):
                cols = slice(jb * LANES, (jb + 1) * LANES)
                rhs = jnp.concatenate([v_even[rows, cols], v_odd[rows, cols]], axis=0)
                s = jnp.dot(ws_ref[jb], rhs, preferred_element_type=F32)
                mlp_ref[0, out_rows, cols] = (u[rows, cols] * (s + bs_ref[:, cols])).astype(BF16)

    subs = [slice(i, i + ts) for i in range(0, tm, ts)]
    pending = [project(r) for r in subs[:PROJ_AHEAD]]
    for i, r in enumerate(subs):
        p_cur = pending.pop(0)
        if i + PROJ_AHEAD < len(subs):
            pending.append(project(subs[i + PROJ_AHEAD]))
        finish(r, p_cur)


def _inproj(x, mods, layer, stream_row, g_pre, w_in, gq_t, gk_t, cos_t, sin_t,
            g_sg, b_sg, ws_cat, bs_full, *, tm, use_rope):
    b, n, _ = x.shape
    grid = (b, n // tm)
    if stream_row is None:
        mod_map = lambda i, j: (layer, i, 0, 0)
    else:
        mod_map = lambda i, j: (layer, stream_row, 0, 0)
    kern = functools.partial(_inproj_kernel, tm=tm, use_rope=use_rope)
    return pl.pallas_call(
        kern,
        grid=grid,
        in_specs=[
            pl.BlockSpec((1, tm, D_MODEL), lambda i, j: (i, j, 0)),
            pl.BlockSpec((1, 1, MOD_SLOTS, D_MODEL), mod_map),
            _const_spec((1, D_MODEL)),
            _const_spec((D_MODEL, IN_WIDTH)),
            _const_spec((1, MXU_DIM)),
            _const_spec((1, LANES)),
            pl.BlockSpec((tm, LANES), lambda i, j: (j, 0)),
            pl.BlockSpec((tm, LANES), lambda i, j: (j, 0)),
            _const_spec((1, MLP_WIDTH)),
            _const_spec((1, MLP_WIDTH)),
            _const_spec((MLP_WIDTH // LANES, CHUNK, 2 * CHUNK)),
            _const_spec((CHUNK, MLP_WIDTH)),
        ],
        out_specs=[
            pl.BlockSpec((1, ATTN_HEADS, HEAD_DIM, tm), lambda i, j: (i, 0, 0, j)),
            pl.BlockSpec((1, KV_HEADS, tm, HEAD_DIM), lambda i, j: (i, 0, j, 0)),
            pl.BlockSpec((1, KV_HEADS, V_ROWS, tm), lambda i, j: (i, 0, 0, j)),
            pl.BlockSpec((1, tm, MLP_WIDTH), lambda i, j: (i, j, 0)),
        ],
        out_shape=[
            jax.ShapeDtypeStruct((b, ATTN_HEADS, HEAD_DIM, n), BF16),
            jax.ShapeDtypeStruct((b, KV_HEADS, n, HEAD_DIM), BF16),
            jax.ShapeDtypeStruct((b, KV_HEADS, V_ROWS, n), BF16),
            jax.ShapeDtypeStruct((b, n, MLP_WIDTH), BF16),
        ],
        compiler_params=pltpu.CompilerParams(
            dimension_semantics=("arbitrary", "arbitrary"), vmem_limit_bytes=VMEM_LIMIT),
        name="inproj",
    )(x, mods, g_pre, w_in, gq_t, gk_t, cos_t, sin_t, g_sg, b_sg, ws_cat, bs_full)


def _attn_kernel(*refs, tq, seg_lens, bk, online):
    n_seg = len(seg_lens)
    bound_ref, qt_ref = refs[0], refs[1]
    k_refs = refs[2:2 + n_seg]
    vt_refs = refs[2 + n_seg:2 + 2 * n_seg]
    o_ref = refs[2 + 2 * n_seg]

    m = [jnp.full((1, tq), -jnp.inf, F32) for _ in range(GQA_GROUP)]
    acc = [jnp.zeros((V_ROWS, tq), F32) for _ in range(GQA_GROUP)]

    def finish(h, s, vt_ref, start, size):
        vt_blk = vt_ref[0, 0, :, start:start + size]
        if online:
            m_new = jnp.maximum(m[h], jnp.max(s, axis=0, keepdims=True))
            alpha = jnp.exp2(m[h] - m_new)
            pt = jnp.exp2(s - m_new).astype(BF16)
            acc[h] = alpha * acc[h] + jnp.dot(vt_blk, pt, preferred_element_type=F32)
            m[h] = m_new
        else:
            pt = jnp.exp2(s - bound_ref[0]).astype(BF16)
            acc[h] = acc[h] + jnp.dot(vt_blk, pt, preferred_element_type=F32)

    pending = []
    for k_ref, vt_ref, seg in zip(k_refs, vt_refs, seg_lens):
        for start in range(0, seg, bk):
            size = min(bk, seg - start)
            for h in range(GQA_GROUP):
                k_blk = k_ref[0, 0, start:start + size, :]
                s = jnp.dot(k_blk, qt_ref[0, h], preferred_element_type=F32)
                pending.append((h, s, vt_ref, start, size))
                if len(pending) > QK_AHEAD:
                    finish(*pending.pop(0))
    while pending:
        finish(*pending.pop(0))

    pairs = []
    for pair in range(GQA_GROUP // 2):
        ot = [acc[h][:HEAD_DIM] / acc[h][HEAD_DIM:HEAD_DIM + 1] for h in (2 * pair, 2 * pair + 1)]
        pairs.append(jnp.concatenate(ot, axis=0).T)
    o_ref[0] = jnp.concatenate(pairs, axis=1).astype(BF16)


def _attention(bound, qt, ks, vts, *, tq, bk, online):
    b, _, _, n = qt.shape
    seg_lens = tuple(k.shape[2] for k in ks)
    kern = functools.partial(_attn_kernel, tq=tq, seg_lens=seg_lens, bk=bk, online=online)
    in_specs = [pl.BlockSpec(memory_space=pltpu.SMEM),
                pl.BlockSpec((1, GQA_GROUP, HEAD_DIM, tq), lambda i, g, j: (i, g, 0, j))]
    in_specs += [pl.BlockSpec((1, 1, s, HEAD_DIM), lambda i, g, j: (i, g, 0, 0)) for s in seg_lens]
    in_specs += [pl.BlockSpec((1, 1, V_ROWS, s), lambda i, g, j: (i, g, 0, 0)) for s in seg_lens]
    return pl.pallas_call(
        kern,
        grid=(b, KV_HEADS, n // tq),
        in_specs=in_specs,
        out_specs=pl.BlockSpec((1, tq, GQA_GROUP * HEAD_DIM), lambda i, g, j: (i, j, g)),
        out_shape=jax.ShapeDtypeStruct((b, n, ATTN_WIDTH), BF16),
        compiler_params=pltpu.CompilerParams(
            dimension_semantics=("arbitrary", "arbitrary", "arbitrary"), vmem_limit_bytes=VMEM_LIMIT),
        name="attention",
    )(bound, qt, *ks, *vts)


def _outffn_kernel(attn_ref, mlp_ref, x_ref, mod_ref, gpm_ref, gpf_ref, gqf_ref,
                   wo_ref, wgu_ref, wd_ref, o_ref, *, tm):
    mod = mod_ref[0, 0]
    gate_mix, shift, scale, gate_ffn = mod[2:3], mod[3:4], mod[4:5], mod[5:6]

    ts = min(tm, SUB_ROWS)
    subs = [slice(i, i + ts) for i in range(0, tm, ts)]
    outs = [jnp.dot(attn_ref[0, r], wo_ref[:ATTN_WIDTH], preferred_element_type=F32)
            + jnp.dot(mlp_ref[0, r], wo_ref[ATTN_WIDTH:], preferred_element_type=F32) for r in subs]

    def ffn(h):
        def gate_up(ci):
            return jnp.dot(h, wgu_ref[ci], preferred_element_type=F32)

        pending = [gate_up(ci) for ci in range(FFN_AHEAD)]
        y = None
        for ci in range(N_FFN_CHUNKS):
            gu = pending.pop(0)
            if ci + FFN_AHEAD < N_FFN_CHUNKS:
                pending.append(gate_up(ci + FFN_AHEAD))
            g = gu[:, :FFN_CHUNK]
            a = (g * jax.nn.sigmoid(g) * gu[:, FFN_CHUNK:]).astype(BF16)
            d = jnp.dot(a, wd_ref[ci], preferred_element_type=F32)
            y = d if y is None else y + d
        return y

    for r, out in zip(subs, outs):
        x1 = x_ref[0, r] + gate_mix * _rms(out, gpm_ref[...])
        h = (_rms(x1, gpf_ref[...]) * (1.0 + scale) + shift).astype(BF16)
        o_ref[0, r] = x1 + gate_ffn * _rms(ffn(h), gqf_ref[...])


def _outffn(attn, mlp, x, mods, layer, stream_row, g_post_mix, g_pre_ffn, g_post_ffn,
            w_out, w_gu, w_down, *, tm):
    b, n, _ = x.shape
    if stream_row is None:
        mod_map = lambda i, j: (layer, i, 0, 0)
    else:
        mod_map = lambda i, j: (layer, stream_row, 0, 0)
    row_spec = lambda w: pl.BlockSpec((1, tm, w), lambda i, j: (i, j, 0))
    return pl.pallas_call(
        functools.partial(_outffn_kernel, tm=tm),
        grid=(b, n // tm),
        in_specs=[
            row_spec(ATTN_WIDTH),
            row_spec(MLP_WIDTH),
            row_spec(D_MODEL),
            pl.BlockSpec((1, 1, MOD_SLOTS, D_MODEL), mod_map),
            _const_spec((1, D_MODEL)),
            _const_spec((1, D_MODEL)),
            _const_spec((1, D_MODEL)),
            _const_spec((ATTN_WIDTH + MLP_WIDTH, D_MODEL)),
            _const_spec((N_FFN_CHUNKS, D_MODEL, 2 * FFN_CHUNK)),
            _const_spec((N_FFN_CHUNKS, FFN_CHUNK, D_MODEL)),
        ],
        out_specs=row_spec(D_MODEL),
        out_shape=jax.ShapeDtypeStruct((b, n, D_MODEL), F32),
        compiler_params=pltpu.CompilerParams(
            dimension_semantics=("arbitrary", "arbitrary"), vmem_limit_bytes=VMEM_LIMIT),
        name="outffn",
    )(attn, mlp, x, mods, g_post_mix, g_pre_ffn, g_post_ffn, w_out, w_gu, w_down)


def _rope_tables(n):
    rows = n // GRID_W
    pos_row = jnp.broadcast_to(jnp.arange(rows, dtype=F32)[:, None], (rows, GRID_W)).reshape(-1)
    pos_col = jnp.broadcast_to(jnp.arange(GRID_W, dtype=F32)[None, :], (rows, GRID_W)).reshape(-1)
    inv = ROPE_THETA ** (-jnp.arange(0, ROPE_AXIS_DIM, 2, dtype=F32) / ROPE_AXIS_DIM)
    ang_r = pos_row[:, None] * inv
    ang_c = pos_col[:, None] * inv
    cos64 = jnp.concatenate([jnp.cos(ang_r)] * 2 + [jnp.cos(ang_c)] * 2, axis=-1)
    sin64 = jnp.concatenate([-jnp.sin(ang_r), jnp.sin(ang_r), -jnp.sin(ang_c), jnp.sin(ang_c)], axis=-1)
    return jnp.tile(cos64, (1, 2)), jnp.tile(sin64, (1, 2))


def kernel(x, c, ctx, c_ctx, w_mod, b_mod, g_pre_mix, g_post_mix, g_pre_ffn, g_post_ffn,
           w_in, g_q, g_k, g_sg, b_sg, w_s, b_s, w_out, w_ffn_in, w_ffn_out):
    b, n, _ = x.shape
    n_ctx = ctx.shape[1]
    cos_t, sin_t = _rope_tables(n)
    cos_c, sin_c = cos_t[:n_ctx], sin_t[:n_ctx]

    cvec = jnp.concatenate([c, c_ctx[None], jnp.zeros((MOD_ROWS - b - 1, D_MODEL), F32)], axis=0)
    mods = _modulation(cvec, w_mod, b_mod)
    mods = mods.reshape(DEPTH, MOD_ROWS, N_MOD, D_MODEL)
    mods = jnp.pad(mods, ((0, 0), (0, 0), (0, MOD_SLOTS - N_MOD), (0, 0)))

    w_in_b = w_in.astype(BF16)
    w_out_b = w_out.astype(BF16)
    w_gate, w_up = w_ffn_in[..., :FFN_HIDDEN], w_ffn_in[..., FFN_HIDDEN:]
    w_gu = jnp.concatenate(
        [w_gate.reshape(DEPTH, D_MODEL, N_FFN_CHUNKS, FFN_CHUNK),
         w_up.reshape(DEPTH, D_MODEL, N_FFN_CHUNKS, FFN_CHUNK)], axis=-1)
    w_gu = w_gu.transpose(0, 2, 1, 3).astype(BF16)
    w_down = w_ffn_out.reshape(DEPTH, N_FFN_CHUNKS, FFN_CHUNK, D_MODEL).astype(BF16)
    ws_cat = w_s.reshape(DEPTH, MLP_HEADS // 2, 2, CHUNK, CHUNK).transpose(0, 1, 3, 2, 4)
    ws_cat = ws_cat.reshape(DEPTH, MLP_HEADS // 2, CHUNK, 2 * CHUNK).astype(BF16)
    bs_full = jnp.repeat(b_s.transpose(0, 2, 1), HEAD_DIM, axis=2)
    gq_t = jnp.tile(g_q, (1, MXU_DIM // HEAD_DIM))[:, None]
    gk_t = jnp.tile(g_k, (1, LANES // HEAD_DIM))[:, None]

    xc = ctx
    for l in range(DEPTH):
        last = l == DEPTH - 1
        in_args = (g_pre_mix[l][None], w_in_b[l], gq_t[l], gk_t[l])
        mlp_args = (g_sg[l][None], b_sg[l][None], ws_cat[l], bs_full[l])
        ffn_args = (g_post_mix[l][None], g_pre_ffn[l][None], g_post_ffn[l][None],
                    w_out_b[l], w_gu[l], w_down[l])

        qct, kc, vct, mlp_c = _inproj(xc, mods, l, b, *in_args, cos_c, sin_c, *mlp_args,
                                      tm=n_ctx, use_rope=False)
        qxt, kx, vxt, mlp_x = _inproj(x, mods, l, None, *in_args, cos_t, sin_t, *mlp_args,
                                      tm=1024, use_rope=True)
        bound = (BOUND_SLACK * LOG2E * math.sqrt(HEAD_DIM)) * jnp.max(jnp.abs(g_q[l])) * jnp.max(jnp.abs(g_k[l]))
        bound = bound.reshape(1)
        attend = functools.partial(_attention, bound, tq=512, bk=512)
        attn_x = lax.cond(2.0 * bound[0] <= SAFE_EXP2_RANGE,
                          lambda: attend(qxt, (kc, kx), (vct, vxt), online=False),
                          lambda: attend(qxt, (kc, kx), (vct, vxt), online=True))
        x = _outffn(attn_x, mlp_x, x, mods, l, None, *ffn_args, tm=512)
        if not last:
            attn_c = _attention(bound, qct, (kc,), (vct,), tq=n_ctx, bk=512, online=True)
            xc = _outffn(attn_c, mlp_c, xc, mods, l, b, *ffn_args, tm=n_ctx)
    return x
```

```python
import functools
import math

import jax
import jax.numpy as jnp
from jax import lax
from jax.experimental import pallas as pl
from jax.experimental.pallas import tpu as pltpu

D_MODEL = 1024
DEPTH = 4
GRID_W = 64
HEAD_DIM = 64
ATTN_HEADS = 8
KV_HEADS = 2
GQA_GROUP = ATTN_HEADS // KV_HEADS
ATTN_WIDTH = ATTN_HEADS * HEAD_DIM
KV_WIDTH = KV_HEADS * HEAD_DIM
MLP_HEADS = 8
MLP_WIDTH = MLP_HEADS * HEAD_DIM
CHUNK = 128
IN_WIDTH = ATTN_WIDTH + 2 * KV_WIDTH + 2 * MLP_WIDTH
FFN_HIDDEN = 2816
N_MOD = 6
ROPE_THETA = 10000.0
ROPE_AXIS_DIM = HEAD_DIM // 2
EPS = 1e-6

LANES = 128
MXU_DIM = 256
MOD_ROWS = 16
MOD_SLOTS = 8
V_ROWS = HEAD_DIM + 16
FFN_CHUNK = MXU_DIM
N_FFN_CHUNKS = FFN_HIDDEN // FFN_CHUNK
SUB_ROWS = MXU_DIM
PROJ_AHEAD = 2
FFN_AHEAD = 2
QK_AHEAD = 2
LOG2E = math.log2(math.e)
BOUND_SLACK = 1.02
SAFE_EXP2_RANGE = 120.0
VMEM_LIMIT = 56 * 1024 * 1024

F32 = jnp.float32
BF16 = jnp.bfloat16


def _const_spec(shape):
    zeros = (0,) * len(shape)
    return pl.BlockSpec(shape, lambda *_: zeros, pipeline_mode=pl.Buffered(1))


def _rms(t, g):
    return t * lax.rsqrt(jnp.mean(t * t, axis=-1, keepdims=True) + EPS) * g


def _mod_kernel(c_ref, w_ref, b_ref, o_ref):
    cv = c_ref[...]
    act = (cv * jax.nn.sigmoid(cv)).astype(BF16)
    o_ref[0] = jnp.dot(act, w_ref[0].astype(BF16), preferred_element_type=F32) + b_ref[0]


def _modulation(cvec, w_mod, b_mod):
    tn = 1536
    n_out = N_MOD * D_MODEL
    return pl.pallas_call(
        _mod_kernel,
        grid=(DEPTH, n_out // tn),
        in_specs=[
            pl.BlockSpec((MOD_ROWS, D_MODEL), lambda l, j: (0, 0)),
            pl.BlockSpec((1, D_MODEL, tn), lambda l, j: (l, 0, j)),
            pl.BlockSpec((1, 1, tn), lambda l, j: (l, 0, j)),
        ],
        out_specs=pl.BlockSpec((1, MOD_ROWS, tn), lambda l, j: (l, 0, j)),
        out_shape=jax.ShapeDtypeStruct((DEPTH, MOD_ROWS, n_out), F32),
        compiler_params=pltpu.CompilerParams(
            dimension_semantics=("arbitrary", "arbitrary"), vmem_limit_bytes=VMEM_LIMIT),
        name="modulation",
    )(cvec, w_mod, b_mod.reshape(DEPTH, 1, n_out))


def _group_sum(t2, ones_bd):
    return jnp.dot(t2.astype(BF16), ones_bd, preferred_element_type=F32)


def _swap_halves16(t):
    lane = lax.broadcasted_iota(jnp.int32, t.shape, 1)
    return jnp.where((lane % 32) < 16, pltpu.roll(t, LANES - 16, 1), pltpu.roll(t, 16, 1))


def _inproj_kernel(x_ref, mod_ref, gpre_ref, w_ref, gq_ref, gk_ref, cos_ref, sin_ref,
                   gsg_ref, bsg_ref, ws_ref, bs_ref,
                   qt_ref, k_ref, vt_ref, mlp_ref, *, tm, use_rope):
    mod = mod_ref[0, 0]
    shift, scale = mod[0:1], mod[1:2]
    ts = CHUNK

    ri = lax.broadcasted_iota(jnp.int32, (MXU_DIM, MXU_DIM), 0) // HEAD_DIM
    ci = lax.broadcasted_iota(jnp.int32, (MXU_DIM, MXU_DIM), 1) // HEAD_DIM
    ones_bd = jnp.where(ri == ci, 1.0, 0.0).astype(BF16)

    def project(r):
        h = (_rms(x_ref[0, r], gpre_ref[...]) * (1.0 + scale) + shift).astype(BF16)
        return jnp.dot(h, w_ref[...], preferred_element_type=F32)

    def head_norm(t, g):
        w = t.shape[1]
        ss = _group_sum(t * t, ones_bd[:w, :w])
        return t * lax.rsqrt(ss * (1.0 / HEAD_DIM) + EPS) * g

    def finish(r, p):
        def rope(t):
            if not use_rope:
                return t
            return t * cos_ref[r, :] + _swap_halves16(t) * sin_ref[r, :]

        q_scale = LOG2E / math.sqrt(HEAD_DIM)
        for half in range(ATTN_WIDTH // MXU_DIM):
            qn = head_norm(p[:, half * MXU_DIM:(half + 1) * MXU_DIM], gq_ref[...])
            for pair in range(MXU_DIM // LANES):
                tt = (rope(qn[:, pair * LANES:(pair + 1) * LANES]) * q_scale).T
                h0 = half * (MXU_DIM // HEAD_DIM) + pair * 2
                qt_ref[0, h0, :, r] = tt[:HEAD_DIM].astype(BF16)
                qt_ref[0, h0 + 1, :, r] = tt[HEAD_DIM:].astype(BF16)

        kn = rope(head_norm(p[:, ATTN_WIDTH:ATTN_WIDTH + KV_WIDTH], gk_ref[...]))
        k_ref[0, 0, r, :] = kn[:, :HEAD_DIM].astype(BF16)
        k_ref[0, 1, r, :] = pltpu.roll(kn, HEAD_DIM, 1)[:, :HEAD_DIM].astype(BF16)

        vvt = p[:, ATTN_WIDTH + KV_WIDTH:ATTN_WIDTH + 2 * KV_WIDTH].T
        sub = lax.broadcasted_iota(jnp.int32, (V_ROWS - HEAD_DIM, ts), 0)
        tail = jnp.where(sub == 0, 1.0, 0.0).astype(BF16)
        for kvh in range(KV_HEADS):
            vt_ref[0, kvh, :HEAD_DIM, r] = vvt[kvh * HEAD_DIM:(kvh + 1) * HEAD_DIM].astype(BF16)
            vt_ref[0, kvh, HEAD_DIM:, r] = tail

        z = p[:, ATTN_WIDTH + 2 * KV_WIDTH:]
        gc = math.sqrt(2.0 / math.pi)
        hz = 0.5 * z
        z = hz + hz * jnp.tanh(z * (gc + (gc * 0.044715) * (z * z)))
        u = z[:, :MLP_WIDTH]
        v2 = z[:, MLP_WIDTH:]
        mu = jnp.mean(v2, axis=-1, keepdims=True)
        vc = v2 - mu
        var = jnp.mean(vc * vc, axis=-1, keepdims=True)
        vln = vc * lax.rsqrt(var + EPS) * gsg_ref[...] + bsg_ref[...]
        lane_w = lax.broadcasted_iota(jnp.int32, (ts, MLP_WIDTH), 1)
        even_head = (lane_w % LANES) < HEAD_DIM
        v_even = jnp.where(even_head, vln, 0.0).astype(BF16)
        v_odd = jnp.where(even_head, 0.0, vln).astype(BF16)
        for ch in range(ts // CHUNK):
            rows = slice(ch * CHUNK, (ch + 1) * CHUNK)
            out_rows = slice(r.start + ch * CHUNK, r.start + (ch + 1) * CHUNK)
            for jb in range(MLP_WIDTH // LANES):
                cols = slice(jb * LANES, (jb + 1) * LANES)
                rhs = jnp.concatenate([v_even[rows, cols], v_odd[rows, cols]], axis=0)
                s = jnp.dot(ws_ref[jb], rhs, preferred_element_type=F32)
                mlp_ref[0, out_rows, cols] = (u[rows, cols] * (s + bs_ref[:, cols])).astype(BF16)

    subs = [slice(i, i + ts) for i in range(0, tm, ts)]
    pending = [project(r) for r in subs[:PROJ_AHEAD]]
    for i, r in enumerate(subs):
        p_cur = pending.pop(0)
        if i + PROJ_AHEAD < len(subs):
            pending.append(project(subs[i + PROJ_AHEAD]))
        finish(r, p_cur)


def _inproj(x, mods, layer, stream_row, g_pre, w_in, gq_t, gk_t, cos_t, sin_t,
            g_sg, b_sg, ws_cat, bs_full, *, tm, use_rope):
    b, n, _ = x.shape
    grid = (b, n // tm)
    if stream_row is None:
        mod_map = lambda i, j: (layer, i, 0, 0)
    else:
        mod_map = lambda i, j: (layer, stream_row, 0, 0)
    kern = functools.partial(_inproj_kernel, tm=tm, use_rope=use_rope)
    return pl.pallas_call(
        kern,
        grid=grid,
        in_specs=[
            pl.BlockSpec((1, tm, D_MODEL), lambda i, j: (i, j, 0)),
            pl.BlockSpec((1, 1, MOD_SLOTS, D_MODEL), mod_map),
            _const_spec((1, D_MODEL)),
            _const_spec((D_MODEL, IN_WIDTH)),
            _const_spec((1, MXU_DIM)),
            _const_spec((1, LANES)),
            pl.BlockSpec((tm, LANES), lambda i, j: (j, 0)),
            pl.BlockSpec((tm, LANES), lambda i, j: (j, 0)),
            _const_spec((1, MLP_WIDTH)),
            _const_spec((1, MLP_WIDTH)),
            _const_spec((MLP_WIDTH // LANES, CHUNK, 2 * CHUNK)),
            _const_spec((CHUNK, MLP_WIDTH)),
        ],
        out_specs=[
            pl.BlockSpec((1, ATTN_HEADS, HEAD_DIM, tm), lambda i, j: (i, 0, 0, j)),
            pl.BlockSpec((1, KV_HEADS, tm, HEAD_DIM), lambda i, j: (i, 0, j, 0)),
            pl.BlockSpec((1, KV_HEADS, V_ROWS, tm), lambda i, j: (i, 0, 0, j)),
            pl.BlockSpec((1, tm, MLP_WIDTH), lambda i, j: (i, j, 0)),
        ],
        out_shape=[
            jax.ShapeDtypeStruct((b, ATTN_HEADS, HEAD_DIM, n), BF16),
            jax.ShapeDtypeStruct((b, KV_HEADS, n, HEAD_DIM), BF16),
            jax.ShapeDtypeStruct((b, KV_HEADS, V_ROWS, n), BF16),
            jax.ShapeDtypeStruct((b, n, MLP_WIDTH), BF16),
        ],
        compiler_params=pltpu.CompilerParams(
            dimension_semantics=("arbitrary", "arbitrary"), vmem_limit_bytes=VMEM_LIMIT),
        name="inproj",
    )(x, mods, g_pre, w_in, gq_t, gk_t, cos_t, sin_t, g_sg, b_sg, ws_cat, bs_full)


def _attn_kernel(*refs, tq, seg_lens, bk, online):
    n_seg = len(seg_lens)
    bound_ref, qt_ref = refs[0], refs[1]
    k_refs = refs[2:2 + n_seg]
    vt_refs = refs[2 + n_seg:2 + 2 * n_seg]
    o_ref = refs[2 + 2 * n_seg]

    m = [jnp.full((1, tq), -jnp.inf, F32) for _ in range(GQA_GROUP)]
    acc = [jnp.zeros((V_ROWS, tq), F32) for _ in range(GQA_GROUP)]

    def finish(h, s, vt_ref, start, size):
        vt_blk = vt_ref[0, 0, :, start:start + size]
        if online:
            m_new = jnp.maximum(m[h], jnp.max(s, axis=0, keepdims=True))
            alpha = jnp.exp2(m[h] - m_new)
            pt = jnp.exp2(s - m_new).astype(BF16)
            acc[h] = alpha * acc[h] + jnp.dot(vt_blk, pt, preferred_element_type=F32)
            m[h] = m_new
        else:
            pt = jnp.exp2(s - bound_ref[0]).astype(BF16)
            acc[h] = acc[h] + jnp.dot(vt_blk, pt, preferred_element_type=F32)

    pending = []
    for k_ref, vt_ref, seg in zip(k_refs, vt_refs, seg_lens):
        for start in range(0, seg, bk):
            size = min(bk, seg - start)
            for h in range(GQA_GROUP):
                k_blk = k_ref[0, 0, start:start + size, :]
                s = jnp.dot(k_blk, qt_ref[0, h], preferred_element_type=F32)
                pending.append((h, s, vt_ref, start, size))
                if len(pending) > QK_AHEAD:
                    finish(*pending.pop(0))
    while pending:
        finish(*pending.pop(0))

    pairs = []
    for pair in range(GQA_GROUP // 2):
        ot = [acc[h][:HEAD_DIM] / acc[h][HEAD_DIM:HEAD_DIM + 1] for h in (2 * pair, 2 * pair + 1)]
        pairs.append(jnp.concatenate(ot, axis=0).T)
    o_ref[0] = jnp.concatenate(pairs, axis=1).astype(BF16)


def _attention(bound, qt, ks, vts, *, tq, bk, online):
    b, _, _, n = qt.shape
    seg_lens = tuple(k.shape[2] for k in ks)
    kern = functools.partial(_attn_kernel, tq=tq, seg_lens=seg_lens, bk=bk, online=online)
    in_specs = [pl.BlockSpec(memory_space=pltpu.SMEM),
                pl.BlockSpec((1, GQA_GROUP, HEAD_DIM, tq), lambda i, g, j: (i, g, 0, j))]
    in_specs += [pl.BlockSpec((1, 1, s, HEAD_DIM), lambda i, g, j: (i, g, 0, 0)) for s in seg_lens]
    in_specs += [pl.BlockSpec((1, 1, V_ROWS, s), lambda i, g, j: (i, g, 0, 0)) for s in seg_lens]
    return pl.pallas_call(
        kern,
        grid=(b, KV_HEADS, n // tq),
        in_specs=in_specs,
        out_specs=pl.BlockSpec((1, tq, GQA_GROUP * HEAD_DIM), lambda i, g, j: (i, j, g)),
        out_shape=jax.ShapeDtypeStruct((b, n, ATTN_WIDTH), BF16),
        compiler_params=pltpu.CompilerParams(
            dimension_semantics=("arbitrary", "arbitrary", "arbitrary"), vmem_limit_bytes=VMEM_LIMIT),
        name="attention",
    )(bound, qt, *ks, *vts)


def _outffn_kernel(attn_ref, mlp_ref, x_ref, mod_ref, gpm_ref, gpf_ref, gqf_ref,
                   wo_ref, wgu_ref, wd_ref, o_ref, *, tm):
    mod = mod_ref[0, 0]
    gate_mix, shift, scale, gate_ffn = mod[2:3], mod[3:4], mod[4:5], mod[5:6]

    ts = min(tm, SUB_ROWS)
    subs = [slice(i, i + ts) for i in range(0, tm, ts)]
    outs = [jnp.dot(attn_ref[0, r], wo_ref[:ATTN_WIDTH], preferred_element_type=F32)
            + jnp.dot(mlp_ref[0, r], wo_ref[ATTN_WIDTH:], preferred_element_type=F32) for r in subs]

    def ffn(h):
        def gate_up(ci):
            cols = slice(ci * FFN_CHUNK, (ci + 1) * FFN_CHUNK)
            up_cols = slice(FFN_HIDDEN + ci * FFN_CHUNK, FFN_HIDDEN + (ci + 1) * FFN_CHUNK)
            return (jnp.dot(h, wgu_ref[:, cols], preferred_element_type=F32),
                    jnp.dot(h, wgu_ref[:, up_cols], preferred_element_type=F32))

        pending = [gate_up(ci) for ci in range(FFN_AHEAD)]
        y = None
        for ci in range(N_FFN_CHUNKS):
            g, u = pending.pop(0)
            if ci + FFN_AHEAD < N_FFN_CHUNKS:
                pending.append(gate_up(ci + FFN_AHEAD))
            a = (g * jax.nn.sigmoid(g) * u).astype(BF16)
            d = jnp.dot(a, wd_ref[ci * FFN_CHUNK:(ci + 1) * FFN_CHUNK], preferred_element_type=F32)
            y = d if y is None else y + d
        return y

    for r, out in zip(subs, outs):
        x1 = x_ref[0, r] + gate_mix * _rms(out, gpm_ref[...])
        h = (_rms(x1, gpf_ref[...]) * (1.0 + scale) + shift).astype(BF16)
        o_ref[0, r] = x1 + gate_ffn * _rms(ffn(h), gqf_ref[...])


def _outffn(attn, mlp, x, mods, layer, stream_row, g_post_mix, g_pre_ffn, g_post_ffn,
            w_out, w_gu, w_down, *, tm):
    b, n, _ = x.shape
    if stream_row is None:
        mod_map = lambda i, j: (layer, i, 0, 0)
    else:
        mod_map = lambda i, j: (layer, stream_row, 0, 0)
    row_spec = lambda w: pl.BlockSpec((1, tm, w), lambda i, j: (i, j, 0))
    return pl.pallas_call(
        functools.partial(_outffn_kernel, tm=tm),
        grid=(b, n // tm),
        in_specs=[
            row_spec(ATTN_WIDTH),
            row_spec(MLP_WIDTH),
            row_spec(D_MODEL),
            pl.BlockSpec((1, 1, MOD_SLOTS, D_MODEL), mod_map),
            _const_spec((1, D_MODEL)),
            _const_spec((1, D_MODEL)),
            _const_spec((1, D_MODEL)),
            _const_spec((ATTN_WIDTH + MLP_WIDTH, D_MODEL)),
            _const_spec((D_MODEL, 2 * FFN_HIDDEN)),
            _const_spec((FFN_HIDDEN, D_MODEL)),
        ],
        out_specs=row_spec(D_MODEL),
        out_shape=jax.ShapeDtypeStruct((b, n, D_MODEL), F32),
        compiler_params=pltpu.CompilerParams(
            dimension_semantics=("arbitrary", "arbitrary"), vmem_limit_bytes=VMEM_LIMIT),
        name="outffn",
    )(attn, mlp, x, mods, g_post_mix, g_pre_ffn, g_post_ffn, w_out, w_gu, w_down)


def _rope_tables(n):
    rows = n // GRID_W
    pos_row = jnp.broadcast_to(jnp.arange(rows, dtype=F32)[:, None], (rows, GRID_W)).reshape(-1)
    pos_col = jnp.broadcast_to(jnp.arange(GRID_W, dtype=F32)[None, :], (rows, GRID_W)).reshape(-1)
    inv = ROPE_THETA ** (-jnp.arange(0, ROPE_AXIS_DIM, 2, dtype=F32) / ROPE_AXIS_DIM)
    ang_r = pos_row[:, None] * inv
    ang_c = pos_col[:, None] * inv
    cos64 = jnp.concatenate([jnp.cos(ang_r)] * 2 + [jnp.cos(ang_c)] * 2, axis=-1)
    sin64 = jnp.concatenate([-jnp.sin(ang_r), jnp.sin(ang_r), -jnp.sin(ang_c), jnp.sin(ang_c)], axis=-1)
    return jnp.tile(cos64, (1, 2)), jnp.tile(sin64, (1, 2))


def kernel(x, c, ctx, c_ctx, w_mod, b_mod, g_pre_mix, g_post_mix, g_pre_ffn, g_post_ffn,
           w_in, g_q, g_k, g_sg, b_sg, w_s, b_s, w_out, w_ffn_in, w_ffn_out):
    b, n, _ = x.shape
    n_ctx = ctx.shape[1]
    cos_t, sin_t = _rope_tables(n)
    cos_c, sin_c = cos_t[:n_ctx], sin_t[:n_ctx]

    cvec = jnp.concatenate([c, c_ctx[None], jnp.zeros((MOD_ROWS - b - 1, D_MODEL), F32)], axis=0)
    mods = _modulation(cvec, w_mod, b_mod)
    mods = mods.reshape(DEPTH, MOD_ROWS, N_MOD, D_MODEL)
    mods = jnp.pad(mods, ((0, 0), (0, 0), (0, MOD_SLOTS - N_MOD), (0, 0)))

    w_in_b = w_in.astype(BF16)
    w_out_b = w_out.astype(BF16)
    w_gu = w_ffn_in.astype(BF16)
    w_down = w_ffn_out.astype(BF16)
    ws_cat = w_s.reshape(DEPTH, MLP_HEADS // 2, 2, CHUNK, CHUNK).transpose(0, 1, 3, 2, 4)
    ws_cat = ws_cat.reshape(DEPTH, MLP_HEADS // 2, CHUNK, 2 * CHUNK).astype(BF16)
    bs_full = jnp.repeat(b_s.transpose(0, 2, 1), HEAD_DIM, axis=2)
    gq_t = jnp.tile(g_q, (1, MXU_DIM // HEAD_DIM))[:, None]
    gk_t = jnp.tile(g_k, (1, LANES // HEAD_DIM))[:, None]

    xc = ctx
    for l in range(DEPTH):
        last = l == DEPTH - 1
        in_args = (g_pre_mix[l][None], w_in_b[l], gq_t[l], gk_t[l])
        mlp_args = (g_sg[l][None], b_sg[l][None], ws_cat[l], bs_full[l])
        ffn_args = (g_post_mix[l][None], g_pre_ffn[l][None], g_post_ffn[l][None],
                    w_out_b[l], w_gu[l], w_down[l])

        qct, kc, vct, mlp_c = _inproj(xc, mods, l, b, *in_args, cos_c, sin_c, *mlp_args,
                                      tm=n_ctx, use_rope=False)
        qxt, kx, vxt, mlp_x = _inproj(x, mods, l, None, *in_args, cos_t, sin_t, *mlp_args,
                                      tm=1024, use_rope=True)
        bound = (BOUND_SLACK * LOG2E * math.sqrt(HEAD_DIM)) * jnp.max(jnp.abs(g_q[l])) * jnp.max(jnp.abs(g_k[l]))
        bound = bound.reshape(1)
        attend = functools.partial(_attention, bound, tq=256, bk=512)
        attn_x = lax.cond(2.0 * bound[0] <= SAFE_EXP2_RANGE,
                          lambda: attend(qxt, (kc, kx), (vct, vxt), online=False),
                          lambda: attend(qxt, (kc, kx), (vct, vxt), online=True))
        x = _outffn(attn_x, mlp_x, x, mods, l, None, *ffn_args, tm=512)
        if not last:
            attn_c = _attention(bound, qct, (kc,), (vct,), tq=n_ctx, bk=512, online=True)
            xc = _outffn(attn_c, mlp_c, xc, mods, l, b, *ffn_args, tm=n_ctx)
    return x
```

```python
import functools
import math

import jax
import jax.numpy as jnp
from jax import lax
from jax.experimental import pallas as pl
from jax.experimental.pallas import tpu as pltpu

D_MODEL = 1024
DEPTH = 4
GRID_W = 64
HEAD_DIM = 64
ATTN_HEADS = 8
KV_HEADS = 2
GQA_GROUP = ATTN_HEADS // KV_HEADS
ATTN_WIDTH = ATTN_HEADS * HEAD_DIM
KV_WIDTH = KV_HEADS * HEAD_DIM
MLP_HEADS = 8
MLP_WIDTH = MLP_HEADS * HEAD_DIM
CHUNK = 128
IN_WIDTH = ATTN_WIDTH + 2 * KV_WIDTH + 2 * MLP_WIDTH
FFN_HIDDEN = 2816
N_MOD = 6
ROPE_THETA = 10000.0
ROPE_AXIS_DIM = HEAD_DIM // 2
EPS = 1e-6

LANES = 128
MXU_DIM = 256
MOD_ROWS = 16
MOD_SLOTS = 8
V_ROWS = HEAD_DIM + 16
FFN_CHUNK = MXU_DIM
N_FFN_CHUNKS = FFN_HIDDEN // FFN_CHUNK
SUB_ROWS = MXU_DIM
PROJ_AHEAD = 2
FFN_AHEAD = 2
QK_AHEAD = 4
LOG2E = math.log2(math.e)
BOUND_SLACK = 1.02
SAFE_EXP2_RANGE = 120.0
VMEM_LIMIT = 56 * 1024 * 1024

F32 = jnp.float32
BF16 = jnp.bfloat16


def _const_spec(shape):
    zeros = (0,) * len(shape)
    return pl.BlockSpec(shape, lambda *_: zeros, pipeline_mode=pl.Buffered(1))


def _rms(t, g):
    return t * lax.rsqrt(jnp.mean(t * t, axis=-1, keepdims=True) + EPS) * g


def _mod_kernel(c_ref, w_ref, b_ref, o_ref):
    cv = c_ref[...]
    act = (cv * jax.nn.sigmoid(cv)).astype(BF16)
    o_ref[0] = jnp.dot(act, w_ref[0].astype(BF16), preferred_element_type=F32) + b_ref[0]


def _modulation(cvec, w_mod, b_mod):
    tn = 1536
    n_out = N_MOD * D_MODEL
    return pl.pallas_call(
        _mod_kernel,
        grid=(DEPTH, n_out // tn),
        in_specs=[
            pl.BlockSpec((MOD_ROWS, D_MODEL), lambda l, j: (0, 0)),
            pl.BlockSpec((1, D_MODEL, tn), lambda l, j: (l, 0, j)),
            pl.BlockSpec((1, 1, tn), lambda l, j: (l, 0, j)),
        ],
        out_specs=pl.BlockSpec((1, MOD_ROWS, tn), lambda l, j: (l, 0, j)),
        out_shape=jax.ShapeDtypeStruct((DEPTH, MOD_ROWS, n_out), F32),
        compiler_params=pltpu.CompilerParams(
            dimension_semantics=("arbitrary", "arbitrary"), vmem_limit_bytes=VMEM_LIMIT),
        name="modulation",
    )(cvec, w_mod, b_mod.reshape(DEPTH, 1, n_out))


def _group_sum(t2, ones_bd):
    return jnp.dot(t2.astype(BF16), ones_bd, preferred_element_type=F32)


def _swap_halves16(t):
    lane = lax.broadcasted_iota(jnp.int32, t.shape, 1)
    return jnp.where((lane % 32) < 16, pltpu.roll(t, LANES - 16, 1), pltpu.roll(t, 16, 1))


def _inproj_kernel(x_ref, mod_ref, gpre_ref, w_ref, gq_ref, gk_ref, cos_ref, sin_ref,
                   gsg_ref, bsg_ref, ws_ref, bs_ref,
                   qt_ref, k_ref, vt_ref, mlp_ref, *, tm, use_rope):
    mod = mod_ref[0, 0]
    shift, scale = mod[0:1], mod[1:2]
    ts = CHUNK

    ri = lax.broadcasted_iota(jnp.int32, (MXU_DIM, MXU_DIM), 0) // HEAD_DIM
    ci = lax.broadcasted_iota(jnp.int32, (MXU_DIM, MXU_DIM), 1) // HEAD_DIM
    ones_bd = jnp.where(ri == ci, 1.0, 0.0).astype(BF16)

    def project(r):
        h = (_rms(x_ref[0, r], gpre_ref[...]) * (1.0 + scale) + shift).astype(BF16)
        return jnp.dot(h, w_ref[...], preferred_element_type=F32)

    def head_norm(t, g):
        w = t.shape[1]
        ss = _group_sum(t * t, ones_bd[:w, :w])
        return t * lax.rsqrt(ss * (1.0 / HEAD_DIM) + EPS) * g

    def finish(r, p):
        def rope(t):
            if not use_rope:
                return t
            return t * cos_ref[r, :] + _swap_halves16(t) * sin_ref[r, :]

        q_scale = LOG2E / math.sqrt(HEAD_DIM)
        for half in range(ATTN_WIDTH // MXU_DIM):
            qn = head_norm(p[:, half * MXU_DIM:(half + 1) * MXU_DIM], gq_ref[...])
            for pair in range(MXU_DIM // LANES):
                tt = (rope(qn[:, pair * LANES:(pair + 1) * LANES]) * q_scale).T
                h0 = half * (MXU_DIM // HEAD_DIM) + pair * 2
                qt_ref[0, h0, :, r] = tt[:HEAD_DIM].astype(BF16)
                qt_ref[0, h0 + 1, :, r] = tt[HEAD_DIM:].astype(BF16)

        kn = rope(head_norm(p[:, ATTN_WIDTH:ATTN_WIDTH + KV_WIDTH], gk_ref[...]))
        k_ref[0, 0, r, :] = kn[:, :HEAD_DIM].astype(BF16)
        k_ref[0, 1, r, :] = pltpu.roll(kn, HEAD_DIM, 1)[:, :HEAD_DIM].astype(BF16)

        vvt = p[:, ATTN_WIDTH + KV_WIDTH:ATTN_WIDTH + 2 * KV_WIDTH].T
        sub = lax.broadcasted_iota(jnp.int32, (V_ROWS - HEAD_DIM, ts), 0)
        tail = jnp.where(sub == 0, 1.0, 0.0).astype(BF16)
        for kvh in range(KV_HEADS):
            vt_ref[0, kvh, :HEAD_DIM, r] = vvt[kvh * HEAD_DIM:(kvh + 1) * HEAD_DIM].astype(BF16)
            vt_ref[0, kvh, HEAD_DIM:, r] = tail

        z = p[:, ATTN_WIDTH + 2 * KV_WIDTH:]
        gc = math.sqrt(2.0 / math.pi)
        hz = 0.5 * z
        z = hz + hz * jnp.tanh(z * (gc + (gc * 0.044715) * (z * z)))
        u = z[:, :MLP_WIDTH]
        v2 = z[:, MLP_WIDTH:]
        mu = jnp.mean(v2, axis=-1, keepdims=True)
        vc = v2 - mu
        var = jnp.mean(vc * vc, axis=-1, keepdims=True)
        vln = vc * lax.rsqrt(var + EPS) * gsg_ref[...] + bsg_ref[...]
        lane_w = lax.broadcasted_iota(jnp.int32, (ts, MLP_WIDTH), 1)
        even_head = (lane_w % LANES) < HEAD_DIM
        v_even = jnp.where(even_head, vln, 0.0).astype(BF16)
        v_odd = jnp.where(even_head, 0.0, vln).astype(BF16)
        for ch in range(ts // CHUNK):
            rows = slice(ch * CHUNK, (ch + 1) * CHUNK)
            out_rows = slice(r.start + ch * CHUNK, r.start + (ch + 1) * CHUNK)
            for jb in range(MLP_WIDTH // LANES):
                cols = slice(jb * LANES, (jb + 1) * LANES)
                rhs = jnp.concatenate([v_even[rows, cols], v_odd[rows, cols]], axis=0)
                s = jnp.dot(ws_ref[jb], rhs, preferred_element_type=F32)
                mlp_ref[0, out_rows, cols] = (u[rows, cols] * (s + bs_ref[:, cols])).astype(BF16)

    subs = [slice(i, i + ts) for i in range(0, tm, ts)]
    pending = [project(r) for r in subs[:PROJ_AHEAD]]
    for i, r in enumerate(subs):
        p_cur = pending.pop(0)
        if i + PROJ_AHEAD < len(subs):
            pending.append(project(subs[i + PROJ_AHEAD]))
        finish(r, p_cur)


def _inproj(x, mods, layer, stream_row, g_pre, w_in, gq_t, gk_t, cos_t, sin_t,
            g_sg, b_sg, ws_cat, bs_full, *, tm, use_rope):
    b, n, _ = x.shape
    grid = (b, n // tm)
    if stream_row is None:
        mod_map = lambda i, j: (layer, i, 0, 0)
    else:
        mod_map = lambda i, j: (layer, stream_row, 0, 0)
    kern = functools.partial(_inproj_kernel, tm=tm, use_rope=use_rope)
    return pl.pallas_call(
        kern,
        grid=grid,
        in_specs=[
            pl.BlockSpec((1, tm, D_MODEL), lambda i, j: (i, j, 0)),
            pl.BlockSpec((1, 1, MOD_SLOTS, D_MODEL), mod_map),
            _const_spec((1, D_MODEL)),
            _const_spec((D_MODEL, IN_WIDTH)),
            _const_spec((1, MXU_DIM)),
            _const_spec((1, LANES)),
            pl.BlockSpec((tm, LANES), lambda i, j: (j, 0)),
            pl.BlockSpec((tm, LANES), lambda i, j: (j, 0)),
            _const_spec((1, MLP_WIDTH)),
            _const_spec((1, MLP_WIDTH)),
            _const_spec((MLP_WIDTH // LANES, CHUNK, 2 * CHUNK)),
            _const_spec((CHUNK, MLP_WIDTH)),
        ],
        out_specs=[
            pl.BlockSpec((1, ATTN_HEADS, HEAD_DIM, tm), lambda i, j: (i, 0, 0, j)),
            pl.BlockSpec((1, KV_HEADS, tm, HEAD_DIM), lambda i, j: (i, 0, j, 0)),
            pl.BlockSpec((1, KV_HEADS, V_ROWS, tm), lambda i, j: (i, 0, 0, j)),
            pl.BlockSpec((1, tm, MLP_WIDTH), lambda i, j: (i, j, 0)),
        ],
        out_shape=[
            jax.ShapeDtypeStruct((b, ATTN_HEADS, HEAD_DIM, n), BF16),
            jax.ShapeDtypeStruct((b, KV_HEADS, n, HEAD_DIM), BF16),
            jax.ShapeDtypeStruct((b, KV_HEADS, V_ROWS, n), BF16),
            jax.ShapeDtypeStruct((b, n, MLP_WIDTH), BF16),
        ],
        compiler_params=pltpu.CompilerParams(
            dimension_semantics=("arbitrary", "arbitrary"), vmem_limit_bytes=VMEM_LIMIT),
        name="inproj",
    )(x, mods, g_pre, w_in, gq_t, gk_t, cos_t, sin_t, g_sg, b_sg, ws_cat, bs_full)


def _attn_kernel(*refs, tq, seg_lens, bk, online):
    n_seg = len(seg_lens)
    bound_ref, qt_ref = refs[0], refs[1]
    k_refs = refs[2:2 + n_seg]
    vt_refs = refs[2 + n_seg:2 + 2 * n_seg]
    o_ref = refs[2 + 2 * n_seg]

    m = [jnp.full((1, tq), -jnp.inf, F32) for _ in range(GQA_GROUP)]
    acc = [jnp.zeros((V_ROWS, tq), F32) for _ in range(GQA_GROUP)]

    def finish(h, s, vt_ref, start, size):
        vt_blk = vt_ref[0, 0, :, start:start + size]
        if online:
            m_new = jnp.maximum(m[h], jnp.max(s, axis=0, keepdims=True))
            alpha = jnp.exp2(m[h] - m_new)
            pt = jnp.exp2(s - m_new).astype(BF16)
            acc[h] = alpha * acc[h] + jnp.dot(vt_blk, pt, preferred_element_type=F32)
            m[h] = m_new
        else:
            pt = jnp.exp2(s - bound_ref[0]).astype(BF16)
            acc[h] = acc[h] + jnp.dot(vt_blk, pt, preferred_element_type=F32)

    pending = []
    for k_ref, vt_ref, seg in zip(k_refs, vt_refs, seg_lens):
        for start in range(0, seg, bk):
            size = min(bk, seg - start)
            for h in range(GQA_GROUP):
                k_blk = k_ref[0, 0, start:start + size, :]
                s = jnp.dot(k_blk, qt_ref[0, h], preferred_element_type=F32)
                pending.append((h, s, vt_ref, start, size))
                if len(pending) > QK_AHEAD:
                    finish(*pending.pop(0))
    while pending:
        finish(*pending.pop(0))

    pairs = []
    for pair in range(GQA_GROUP // 2):
        ot = [acc[h][:HEAD_DIM] / acc[h][HEAD_DIM:HEAD_DIM + 1] for h in (2 * pair, 2 * pair + 1)]
        pairs.append(jnp.concatenate(ot, axis=0).T)
    o_ref[0] = jnp.concatenate(pairs, axis=1).astype(BF16)


def _attention(bound, qt, ks, vts, *, tq, bk, online):
    b, _, _, n = qt.shape
    seg_lens = tuple(k.shape[2] for k in ks)
    kern = functools.partial(_attn_kernel, tq=tq, seg_lens=seg_lens, bk=bk, online=online)
    in_specs = [pl.BlockSpec(memory_space=pltpu.SMEM),
                pl.BlockSpec((1, GQA_GROUP, HEAD_DIM, tq), lambda i, g, j: (i, g, 0, j))]
    in_specs += [pl.BlockSpec((1, 1, s, HEAD_DIM), lambda i, g, j: (i, g, 0, 0)) for s in seg_lens]
    in_specs += [pl.BlockSpec((1, 1, V_ROWS, s), lambda i, g, j: (i, g, 0, 0)) for s in seg_lens]
    return pl.pallas_call(
        kern,
        grid=(b, KV_HEADS, n // tq),
        in_specs=in_specs,
        out_specs=pl.BlockSpec((1, tq, GQA_GROUP * HEAD_DIM), lambda i, g, j: (i, j, g)),
        out_shape=jax.ShapeDtypeStruct((b, n, ATTN_WIDTH), BF16),
        compiler_params=pltpu.CompilerParams(
            dimension_semantics=("arbitrary", "arbitrary", "arbitrary"), vmem_limit_bytes=VMEM_LIMIT),
        name="attention",
    )(bound, qt, *ks, *vts)


def _outffn_kernel(attn_ref, mlp_ref, x_ref, mod_ref, gpm_ref, gpf_ref, gqf_ref,
                   wo_ref, wgu_ref, wd_ref, o_ref, *, tm):
    mod = mod_ref[0, 0]
    gate_mix, shift, scale, gate_ffn = mod[2:3], mod[3:4], mod[4:5], mod[5:6]

    ts = min(tm, SUB_ROWS)
    subs = [slice(i, i + ts) for i in range(0, tm, ts)]
    outs = [jnp.dot(attn_ref[0, r], wo_ref[:ATTN_WIDTH], preferred_element_type=F32)
            + jnp.dot(mlp_ref[0, r], wo_ref[ATTN_WIDTH:], preferred_element_type=F32) for r in subs]

    def ffn(h):
        def gate_up(ci):
            cols = slice(ci * FFN_CHUNK, (ci + 1) * FFN_CHUNK)
            up_cols = slice(FFN_HIDDEN + ci * FFN_CHUNK, FFN_HIDDEN + (ci + 1) * FFN_CHUNK)
            return (jnp.dot(h, wgu_ref[:, cols], preferred_element_type=F32),
                    jnp.dot(h, wgu_ref[:, up_cols], preferred_element_type=F32))

        pending = [gate_up(ci) for ci in range(FFN_AHEAD)]
        y = None
        for ci in range(N_FFN_CHUNKS):
            g, u = pending.pop(0)
            if ci + FFN_AHEAD < N_FFN_CHUNKS:
                pending.append(gate_up(ci + FFN_AHEAD))
            a = (g * jax.nn.sigmoid(g) * u).astype(BF16)
            d = jnp.dot(a, wd_ref[ci * FFN_CHUNK:(ci + 1) * FFN_CHUNK], preferred_element_type=F32)
            y = d if y is None else y + d
        return y

    for r, out in zip(subs, outs):
        x1 = x_ref[0, r] + gate_mix * _rms(out, gpm_ref[...])
        h = (_rms(x1, gpf_ref[...]) * (1.0 + scale) + shift).astype(BF16)
        o_ref[0, r] = x1 + gate_ffn * _rms(ffn(h), gqf_ref[...])


def _outffn(attn, mlp, x, mods, layer, stream_row, g_post_mix, g_pre_ffn, g_post_ffn,
            w_out, w_gu, w_down, *, tm):
    b, n, _ = x.shape
    if stream_row is None:
        mod_map = lambda i, j: (layer, i, 0, 0)
    else:
        mod_map = lambda i, j: (layer, stream_row, 0, 0)
    row_spec = lambda w: pl.BlockSpec((1, tm, w), lambda i, j: (i, j, 0))
    return pl.pallas_call(
        functools.partial(_outffn_kernel, tm=tm),
        grid=(b, n // tm),
        in_specs=[
            row_spec(ATTN_WIDTH),
            row_spec(MLP_WIDTH),
            row_spec(D_MODEL),
            pl.BlockSpec((1, 1, MOD_SLOTS, D_MODEL), mod_map),
            _const_spec((1, D_MODEL)),
            _const_spec((1, D_MODEL)),
            _const_spec((1, D_MODEL)),
            _const_spec((ATTN_WIDTH + MLP_WIDTH, D_MODEL)),
            _const_spec((D_MODEL, 2 * FFN_HIDDEN)),
            _const_spec((FFN_HIDDEN, D_MODEL)),
        ],
        out_specs=row_spec(D_MODEL),
        out_shape=jax.ShapeDtypeStruct((b, n, D_MODEL), F32),
        compiler_params=pltpu.CompilerParams(
            dimension_semantics=("arbitrary", "arbitrary"), vmem_limit_bytes=VMEM_LIMIT),
        name="outffn",
    )(attn, mlp, x, mods, g_post_mix, g_pre_ffn, g_post_ffn, w_out, w_gu, w_down)


def _rope_tables(n):
    rows = n // GRID_W
    pos_row = jnp.broadcast_to(jnp.arange(rows, dtype=F32)[:, None], (rows, GRID_W)).reshape(-1)
    pos_col = jnp.broadcast_to(jnp.arange(GRID_W, dtype=F32)[None, :], (rows, GRID_W)).reshape(-1)
    inv = ROPE_THETA ** (-jnp.arange(0, ROPE_AXIS_DIM, 2, dtype=F32) / ROPE_AXIS_DIM)
    ang_r = pos_row[:, None] * inv
    ang_c = pos_col[:, None] * inv
    cos64 = jnp.concatenate([jnp.cos(ang_r)] * 2 + [jnp.cos(ang_c)] * 2, axis=-1)
    sin64 = jnp.concatenate([-jnp.sin(ang_r), jnp.sin(ang_r), -jnp.sin(ang_c), jnp.sin(ang_c)], axis=-1)
    return jnp.tile(cos64, (1, 2)), jnp.tile(sin64, (1, 2))


def kernel(x, c, ctx, c_ctx, w_mod, b_mod, g_pre_mix, g_post_mix, g_pre_ffn, g_post_ffn,
           w_in, g_q, g_k, g_sg, b_sg, w_s, b_s, w_out, w_ffn_in, w_ffn_out):
    b, n, _ = x.shape
    n_ctx = ctx.shape[1]
    cos_t, sin_t = _rope_tables(n)
    cos_c, sin_c = cos_t[:n_ctx], sin_t[:n_ctx]

    cvec = jnp.concatenate([c, c_ctx[None], jnp.zeros((MOD_ROWS - b - 1, D_MODEL), F32)], axis=0)
    mods = _modulation(cvec, w_mod, b_mod)
    mods = mods.reshape(DEPTH, MOD_ROWS, N_MOD, D_MODEL)
    mods = jnp.pad(mods, ((0, 0), (0, 0), (0, MOD_SLOTS - N_MOD), (0, 0)))

    w_in_b = w_in.astype(BF16)
    w_out_b = w_out.astype(BF16)
    w_gu = w_ffn_in.astype(BF16)
    w_down = w_ffn_out.astype(BF16)
    ws_cat = w_s.reshape(DEPTH, MLP_HEADS // 2, 2, CHUNK, CHUNK).transpose(0, 1, 3, 2, 4)
    ws_cat = ws_cat.reshape(DEPTH, MLP_HEADS // 2, CHUNK, 2 * CHUNK).astype(BF16)
    bs_full = jnp.repeat(b_s.transpose(0, 2, 1), HEAD_DIM, axis=2)
    gq_t = jnp.tile(g_q, (1, MXU_DIM // HEAD_DIM))[:, None]
    gk_t = jnp.tile(g_k, (1, LANES // HEAD_DIM))[:, None]

    xc = ctx
    for l in range(DEPTH):
        last = l == DEPTH - 1
        in_args = (g_pre_mix[l][None], w_in_b[l], gq_t[l], gk_t[l])
        mlp_args = (g_sg[l][None], b_sg[l][None], ws_cat[l], bs_full[l])
        ffn_args = (g_post_mix[l][None], g_pre_ffn[l][None], g_post_ffn[l][None],
                    w_out_b[l], w_gu[l], w_down[l])

        qct, kc, vct, mlp_c = _inproj(xc, mods, l, b, *in_args, cos_c, sin_c, *mlp_args,
                                      tm=n_ctx, use_rope=False)
        qxt, kx, vxt, mlp_x = _inproj(x, mods, l, None, *in_args, cos_t, sin_t, *mlp_args,
                                      tm=1024, use_rope=True)
        bound = (BOUND_SLACK * LOG2E * math.sqrt(HEAD_DIM)) * jnp.max(jnp.abs(g_q[l])) * jnp.max(jnp.abs(g_k[l]))
        bound = bound.reshape(1)
        attend = functools.partial(_attention, bound, tq=256, bk=256)
        attn_x = lax.cond(2.0 * bound[0] <= SAFE_EXP2_RANGE,
                          lambda: attend(qxt, (kc, kx), (vct, vxt), online=False),
                          lambda: attend(qxt, (kc, kx), (vct, vxt), online=True))
        x = _outffn(attn_x, mlp_x, x, mods, l, None, *ffn_args, tm=512)
        if not last:
            attn_c = _attention(bound, qct, (kc,), (vct,), tq=n_ctx, bk=512, online=True)
            xc = _outffn(attn_c, mlp_c, xc, mods, l, b, *ffn_args, tm=n_ctx)
    return x
```

```python
import functools
import math

import jax
import jax.numpy as jnp
from jax import lax
from jax.experimental import pallas as pl
from jax.experimental.pallas import tpu as pltpu

D_MODEL = 1024
DEPTH = 4
GRID_W = 64
HEAD_DIM = 64
ATTN_HEADS = 8
KV_HEADS = 2
GQA_GROUP = ATTN_HEADS // KV_HEADS
ATTN_WIDTH = ATTN_HEADS * HEAD_DIM
KV_WIDTH = KV_HEADS * HEAD_DIM
MLP_HEADS = 8
MLP_WIDTH = MLP_HEADS * HEAD_DIM
CHUNK = 128
IN_WIDTH = ATTN_WIDTH + 2 * KV_WIDTH + 2 * MLP_WIDTH
FFN_HIDDEN = 2816
N_MOD = 6
ROPE_THETA = 10000.0
ROPE_AXIS_DIM = HEAD_DIM // 2
EPS = 1e-6

LANES = 128
MXU_DIM = 256
MOD_ROWS = 16
MOD_SLOTS = 8
V_ROWS = HEAD_DIM + 16
FFN_CHUNK = MXU_DIM
N_FFN_CHUNKS = FFN_HIDDEN // FFN_CHUNK
SUB_ROWS = MXU_DIM
PROJ_AHEAD = 2
FFN_AHEAD = 2
QK_AHEAD = 4
LOG2E = math.log2(math.e)
BOUND_SLACK = 1.02
SAFE_EXP2_RANGE = 120.0
VMEM_LIMIT = 56 * 1024 * 1024

F32 = jnp.float32
BF16 = jnp.bfloat16


def _const_spec(shape):
    zeros = (0,) * len(shape)
    return pl.BlockSpec(shape, lambda *_: zeros, pipeline_mode=pl.Buffered(1))


def _rms(t, g):
    return t * lax.rsqrt(jnp.mean(t * t, axis=-1, keepdims=True) + EPS) * g


def _mod_kernel(c_ref, w_ref, b_ref, o_ref):
    cv = c_ref[...]
    act = (cv * jax.nn.sigmoid(cv)).astype(BF16)
    o_ref[0] = jnp.dot(act, w_ref[0].astype(BF16), preferred_element_type=F32) + b_ref[0]


def _modulation(cvec, w_mod, b_mod):
    tn = 1536
    n_out = N_MOD * D_MODEL
    return pl.pallas_call(
        _mod_kernel,
        grid=(DEPTH, n_out // tn),
        in_specs=[
            pl.BlockSpec((MOD_ROWS, D_MODEL), lambda l, j: (0, 0)),
            pl.BlockSpec((1, D_MODEL, tn), lambda l, j: (l, 0, j)),
            pl.BlockSpec((1, 1, tn), lambda l, j: (l, 0, j)),
        ],
        out_specs=pl.BlockSpec((1, MOD_ROWS, tn), lambda l, j: (l, 0, j)),
        out_shape=jax.ShapeDtypeStruct((DEPTH, MOD_ROWS, n_out), F32),
        compiler_params=pltpu.CompilerParams(
            dimension_semantics=("arbitrary", "arbitrary"), vmem_limit_bytes=VMEM_LIMIT),
        name="modulation",
    )(cvec, w_mod, b_mod.reshape(DEPTH, 1, n_out))


def _group_sum(t2, ones_bd):
    return jnp.dot(t2.astype(BF16), ones_bd, preferred_element_type=F32)


def _swap_halves16(t):
    lane = lax.broadcasted_iota(jnp.int32, t.shape, 1)
    return jnp.where((lane % 32) < 16, pltpu.roll(t, LANES - 16, 1), pltpu.roll(t, 16, 1))


def _inproj_kernel(x_ref, mod_ref, gpre_ref, w_ref, gq_ref, gk_ref, cos_ref, sin_ref,
                   gsg_ref, bsg_ref, ws_ref, bs_ref,
                   qt_ref, k_ref, vt_ref, mlp_ref, *, tm, use_rope):
    mod = mod_ref[0, 0]
    shift, scale = mod[0:1], mod[1:2]
    ts = CHUNK

    ri = lax.broadcasted_iota(jnp.int32, (MXU_DIM, MXU_DIM), 0) // HEAD_DIM
    ci = lax.broadcasted_iota(jnp.int32, (MXU_DIM, MXU_DIM), 1) // HEAD_DIM
    ones_bd = jnp.where(ri == ci, 1.0, 0.0).astype(BF16)

    def project(r):
        h = (_rms(x_ref[0, r], gpre_ref[...]) * (1.0 + scale) + shift).astype(BF16)
        return jnp.dot(h, w_ref[...], preferred_element_type=F32)

    def head_norm(t, g):
        w = t.shape[1]
        ss = _group_sum(t * t, ones_bd[:w, :w])
        return t * lax.rsqrt(ss * (1.0 / HEAD_DIM) + EPS) * g

    def finish(r, p):
        def rope(t):
            if not use_rope:
                return t
            return t * cos_ref[r, :] + _swap_halves16(t) * sin_ref[r, :]

        q_scale = LOG2E / math.sqrt(HEAD_DIM)
        for half in range(ATTN_WIDTH // MXU_DIM):
            qn = head_norm(p[:, half * MXU_DIM:(half + 1) * MXU_DIM], gq_ref[...])
            for pair in range(MXU_DIM // LANES):
                tt = (rope(qn[:, pair * LANES:(pair + 1) * LANES]) * q_scale).T
                h0 = half * (MXU_DIM // HEAD_DIM) + pair * 2
                qt_ref[0, h0, :, r] = tt[:HEAD_DIM].astype(BF16)
                qt_ref[0, h0 + 1, :, r] = tt[HEAD_DIM:].astype(BF16)

        kn = rope(head_norm(p[:, ATTN_WIDTH:ATTN_WIDTH + KV_WIDTH], gk_ref[...]))
        k_ref[0, 0, r, :] = kn[:, :HEAD_DIM].astype(BF16)
        k_ref[0, 1, r, :] = pltpu.roll(kn, HEAD_DIM, 1)[:, :HEAD_DIM].astype(BF16)

        vvt = p[:, ATTN_WIDTH + KV_WIDTH:ATTN_WIDTH + 2 * KV_WIDTH].T
        sub = lax.broadcasted_iota(jnp.int32, (V_ROWS - HEAD_DIM, ts), 0)
        tail = jnp.where(sub == 0, 1.0, 0.0).astype(BF16)
        for kvh in range(KV_HEADS):
            vt_ref[0, kvh, :HEAD_DIM, r] = vvt[kvh * HEAD_DIM:(kvh + 1) * HEAD_DIM].astype(BF16)
            vt_ref[0, kvh, HEAD_DIM:, r] = tail

        z = p[:, ATTN_WIDTH + 2 * KV_WIDTH:]
        gc = math.sqrt(2.0 / math.pi)
        hz = 0.5 * z
        z = hz + hz * jnp.tanh(z * (gc + (gc * 0.044715) * (z * z)))
        u = z[:, :MLP_WIDTH]
        v2 = z[:, MLP_WIDTH:]
        mu = jnp.mean(v2, axis=-1, keepdims=True)
        vc = v2 - mu
        var = jnp.mean(vc * vc, axis=-1, keepdims=True)
        vln = vc * lax.rsqrt(var + EPS) * gsg_ref[...] + bsg_ref[...]
        lane_w = lax.broadcasted_iota(jnp.int32, (ts, MLP_WIDTH), 1)
        even_head = (lane_w % LANES) < HEAD_DIM
        v_even = jnp.where(even_head, vln, 0.0).astype(BF16)
        v_odd = jnp.where(even_head, 0.0, vln).astype(BF16)
        for ch in range(ts // CHUNK):
            rows = slice(ch * CHUNK, (ch + 1) * CHUNK)
            out_rows = slice(r.start + ch * CHUNK, r.start + (ch + 1) * CHUNK)
            for jb in range(MLP_WIDTH // LANES):
                cols = slice(jb * LANES, (jb + 1) * LANES)
                rhs = jnp.concatenate([v_even[rows, cols], v_odd[rows, cols]], axis=0)
                s = jnp.dot(ws_ref[jb], rhs, preferred_element_type=F32)
                mlp_ref[0, out_rows, cols] = (u[rows, cols] * (s + bs_ref[:, cols])).astype(BF16)

    subs = [slice(i, i + ts) for i in range(0, tm, ts)]
    pending = [project(r) for r in subs[:PROJ_AHEAD]]
    for i, r in enumerate(subs):
        p_cur = pending.pop(0)
        if i + PROJ_AHEAD < len(subs):
            pending.append(project(subs[i + PROJ_AHEAD]))
        finish(r, p_cur)


def _inproj(x, mods, layer, stream_row, g_pre, w_in, gq_t, gk_t, cos_t, sin_t,
            g_sg, b_sg, ws_cat, bs_full, *, tm, use_rope):
    b, n, _ = x.shape
    grid = (b, n // tm)
    if stream_row is None:
        mod_map = lambda i, j: (layer, i, 0, 0)
    else:
        mod_map = lambda i, j: (layer, stream_row, 0, 0)
    kern = functools.partial(_inproj_kernel, tm=tm, use_rope=use_rope)
    return pl.pallas_call(
        kern,
        grid=grid,
        in_specs=[
            pl.BlockSpec((1, tm, D_MODEL), lambda i, j: (i, j, 0)),
            pl.BlockSpec((1, 1, MOD_SLOTS, D_MODEL), mod_map),
            _const_spec((1, D_MODEL)),
            _const_spec((D_MODEL, IN_WIDTH)),
            _const_spec((1, MXU_DIM)),
            _const_spec((1, LANES)),
            pl.BlockSpec((tm, LANES), lambda i, j: (j, 0)),
            pl.BlockSpec((tm, LANES), lambda i, j: (j, 0)),
            _const_spec((1, MLP_WIDTH)),
            _const_spec((1, MLP_WIDTH)),
            _const_spec((MLP_WIDTH // LANES, CHUNK, 2 * CHUNK)),
            _const_spec((CHUNK, MLP_WIDTH)),
        ],
        out_specs=[
            pl.BlockSpec((1, ATTN_HEADS, HEAD_DIM, tm), lambda i, j: (i, 0, 0, j)),
            pl.BlockSpec((1, KV_HEADS, tm, HEAD_DIM), lambda i, j: (i, 0, j, 0)),
            pl.BlockSpec((1, KV_HEADS, V_ROWS, tm), lambda i, j: (i, 0, 0, j)),
            pl.BlockSpec((1, tm, MLP_WIDTH), lambda i, j: (i, j, 0)),
        ],
        out_shape=[
            jax.ShapeDtypeStruct((b, ATTN_HEADS, HEAD_DIM, n), BF16),
            jax.ShapeDtypeStruct((b, KV_HEADS, n, HEAD_DIM), BF16),
            jax.ShapeDtypeStruct((b, KV_HEADS, V_ROWS, n), BF16),
            jax.ShapeDtypeStruct((b, n, MLP_WIDTH), BF16),
        ],
        compiler_params=pltpu.CompilerParams(
            dimension_semantics=("arbitrary", "arbitrary"), vmem_limit_bytes=VMEM_LIMIT),
        name="inproj",
    )(x, mods, g_pre, w_in, gq_t, gk_t, cos_t, sin_t, g_sg, b_sg, ws_cat, bs_full)


def _attn_kernel(*refs, tq, seg_lens, bk, online):
    n_seg = len(seg_lens)
    bound_ref, qt_ref = refs[0], refs[1]
    k_refs = refs[2:2 + n_seg]
    vt_refs = refs[2 + n_seg:2 + 2 * n_seg]
    o_ref = refs[2 + 2 * n_seg]

    ts = min(tq, SUB_ROWS)
    subs = [slice(i, i + ts) for i in range(0, tq, ts)]
    m = {(t, h): jnp.full((1, ts), -jnp.inf, F32) for t in range(len(subs)) for h in range(GQA_GROUP)}
    acc = {(t, h): jnp.zeros((V_ROWS, ts), F32) for t in range(len(subs)) for h in range(GQA_GROUP)}

    def finish(t, h, s, vt_ref, start, size, last):
        vt_blk = vt_ref[0, 0, :, start:start + size]
        if online:
            m_new = jnp.maximum(m[t, h], jnp.max(s, axis=0, keepdims=True))
            alpha = jnp.exp2(m[t, h] - m_new)
            pt = jnp.exp2(s - m_new).astype(BF16)
            acc[t, h] = alpha * acc[t, h] + jnp.dot(vt_blk, pt, preferred_element_type=F32)
            m[t, h] = m_new
        else:
            pt = jnp.exp2(s - bound_ref[0]).astype(BF16)
            acc[t, h] = acc[t, h] + jnp.dot(vt_blk, pt, preferred_element_type=F32)
        if last:
            pairs = []
            for pair in range(GQA_GROUP // 2):
                ot = [acc[t, g][:HEAD_DIM] / acc[t, g][HEAD_DIM:HEAD_DIM + 1] for g in (2 * pair, 2 * pair + 1)]
                pairs.append(jnp.concatenate(ot, axis=0).T)
            o_ref[0, subs[t], :] = jnp.concatenate(pairs, axis=1).astype(BF16)

    blocks = [(k_ref, vt_ref, start, min(bk, seg - start))
              for k_ref, vt_ref, seg in zip(k_refs, vt_refs, seg_lens) for start in range(0, seg, bk)]
    pending = []
    for t, r in enumerate(subs):
        for bi, (k_ref, vt_ref, start, size) in enumerate(blocks):
            for h in range(GQA_GROUP):
                k_blk = k_ref[0, 0, start:start + size, :]
                s = jnp.dot(k_blk, qt_ref[0, h, :, r], preferred_element_type=F32)
                last = bi == len(blocks) - 1 and h == GQA_GROUP - 1
                pending.append((t, h, s, vt_ref, start, size, last))
                if len(pending) > QK_AHEAD:
                    finish(*pending.pop(0))
    while pending:
        finish(*pending.pop(0))


def _attention(bound, qt, ks, vts, *, tq, bk, online):
    b, _, _, n = qt.shape
    seg_lens = tuple(k.shape[2] for k in ks)
    kern = functools.partial(_attn_kernel, tq=tq, seg_lens=seg_lens, bk=bk, online=online)
    in_specs = [pl.BlockSpec(memory_space=pltpu.SMEM),
                pl.BlockSpec((1, GQA_GROUP, HEAD_DIM, tq), lambda i, g, j: (i, g, 0, j))]
    in_specs += [pl.BlockSpec((1, 1, s, HEAD_DIM), lambda i, g, j: (i, g, 0, 0)) for s in seg_lens]
    in_specs += [pl.BlockSpec((1, 1, V_ROWS, s), lambda i, g, j: (i, g, 0, 0)) for s in seg_lens]
    return pl.pallas_call(
        kern,
        grid=(b, KV_HEADS, n // tq),
        in_specs=in_specs,
        out_specs=pl.BlockSpec((1, tq, GQA_GROUP * HEAD_DIM), lambda i, g, j: (i, j, g)),
        out_shape=jax.ShapeDtypeStruct((b, n, ATTN_WIDTH), BF16),
        compiler_params=pltpu.CompilerParams(
            dimension_semantics=("arbitrary", "arbitrary", "arbitrary"), vmem_limit_bytes=VMEM_LIMIT),
        name="attention",
    )(bound, qt, *ks, *vts)


def _outffn_kernel(attn_ref, mlp_ref, x_ref, mod_ref, gpm_ref, gpf_ref, gqf_ref,
                   wo_ref, wgu_ref, wd_ref, o_ref, *, tm):
    mod = mod_ref[0, 0]
    gate_mix, shift, scale, gate_ffn = mod[2:3], mod[3:4], mod[4:5], mod[5:6]

    ts = min(tm, SUB_ROWS)
    subs = [slice(i, i + ts) for i in range(0, tm, ts)]
    outs = [jnp.dot(attn_ref[0, r], wo_ref[:ATTN_WIDTH], preferred_element_type=F32)
            + jnp.dot(mlp_ref[0, r], wo_ref[ATTN_WIDTH:], preferred_element_type=F32) for r in subs]

    def ffn(h):
        def gate_up(ci):
            cols = slice(ci * FFN_CHUNK, (ci + 1) * FFN_CHUNK)
            up_cols = slice(FFN_HIDDEN + ci * FFN_CHUNK, FFN_HIDDEN + (ci + 1) * FFN_CHUNK)
            return (jnp.dot(h, wgu_ref[:, cols], preferred_element_type=F32),
                    jnp.dot(h, wgu_ref[:, up_cols], preferred_element_type=F32))

        pending = [gate_up(ci) for ci in range(FFN_AHEAD)]
        y = None
        for ci in range(N_FFN_CHUNKS):
            g, u = pending.pop(0)
            if ci + FFN_AHEAD < N_FFN_CHUNKS:
                pending.append(gate_up(ci + FFN_AHEAD))
            a = (g * jax.nn.sigmoid(g) * u).astype(BF16)
            d = jnp.dot(a, wd_ref[ci * FFN_CHUNK:(ci + 1) * FFN_CHUNK], preferred_element_type=F32)
            y = d if y is None else y + d
        return y

    for r, out in zip(subs, outs):
        x1 = x_ref[0, r] + gate_mix * _rms(out, gpm_ref[...])
        h = (_rms(x1, gpf_ref[...]) * (1.0 + scale) + shift).astype(BF16)
        o_ref[0, r] = x1 + gate_ffn * _rms(ffn(h), gqf_ref[...])


def _outffn(attn, mlp, x, mods, layer, stream_row, g_post_mix, g_pre_ffn, g_post_ffn,
            w_out, w_gu, w_down, *, tm):
    b, n, _ = x.shape
    if stream_row is None:
        mod_map = lambda i, j: (layer, i, 0, 0)
    else:
        mod_map = lambda i, j: (layer, stream_row, 0, 0)
    row_spec = lambda w: pl.BlockSpec((1, tm, w), lambda i, j: (i, j, 0))
    return pl.pallas_call(
        functools.partial(_outffn_kernel, tm=tm),
        grid=(b, n // tm),
        in_specs=[
            row_spec(ATTN_WIDTH),
            row_spec(MLP_WIDTH),
            row_spec(D_MODEL),
            pl.BlockSpec((1, 1, MOD_SLOTS, D_MODEL), mod_map),
            _const_spec((1, D_MODEL)),
            _const_spec((1, D_MODEL)),
            _const_spec((1, D_MODEL)),
            _const_spec((ATTN_WIDTH + MLP_WIDTH, D_MODEL)),
            _const_spec((D_MODEL, 2 * FFN_HIDDEN)),
            _const_spec((FFN_HIDDEN, D_MODEL)),
        ],
        out_specs=row_spec(D_MODEL),
        out_shape=jax.ShapeDtypeStruct((b, n, D_MODEL), F32),
        compiler_params=pltpu.CompilerParams(
            dimension_semantics=("arbitrary", "arbitrary"), vmem_limit_bytes=VMEM_LIMIT),
        name="outffn",
    )(attn, mlp, x, mods, g_post_mix, g_pre_ffn, g_post_ffn, w_out, w_gu, w_down)


def _rope_tables(n):
    rows = n // GRID_W
    pos_row = jnp.broadcast_to(jnp.arange(rows, dtype=F32)[:, None], (rows, GRID_W)).reshape(-1)
    pos_col = jnp.broadcast_to(jnp.arange(GRID_W, dtype=F32)[None, :], (rows, GRID_W)).reshape(-1)
    inv = ROPE_THETA ** (-jnp.arange(0, ROPE_AXIS_DIM, 2, dtype=F32) / ROPE_AXIS_DIM)
    ang_r = pos_row[:, None] * inv
    ang_c = pos_col[:, None] * inv
    cos64 = jnp.concatenate([jnp.cos(ang_r)] * 2 + [jnp.cos(ang_c)] * 2, axis=-1)
    sin64 = jnp.concatenate([-jnp.sin(ang_r), jnp.sin(ang_r), -jnp.sin(ang_c), jnp.sin(ang_c)], axis=-1)
    return jnp.tile(cos64, (1, 2)), jnp.tile(sin64, (1, 2))


def kernel(x, c, ctx, c_ctx, w_mod, b_mod, g_pre_mix, g_post_mix, g_pre_ffn, g_post_ffn,
           w_in, g_q, g_k, g_sg, b_sg, w_s, b_s, w_out, w_ffn_in, w_ffn_out):
    b, n, _ = x.shape
    n_ctx = ctx.shape[1]
    cos_t, sin_t = _rope_tables(n)
    cos_c, sin_c = cos_t[:n_ctx], sin_t[:n_ctx]

    cvec = jnp.concatenate([c, c_ctx[None], jnp.zeros((MOD_ROWS - b - 1, D_MODEL), F32)], axis=0)
    mods = _modulation(cvec, w_mod, b_mod)
    mods = mods.reshape(DEPTH, MOD_ROWS, N_MOD, D_MODEL)
    mods = jnp.pad(mods, ((0, 0), (0, 0), (0, MOD_SLOTS - N_MOD), (0, 0)))

    w_in_b = w_in.astype(BF16)
    w_out_b = w_out.astype(BF16)
    w_gu = w_ffn_in.astype(BF16)
    w_down = w_ffn_out.astype(BF16)
    ws_cat = w_s.reshape(DEPTH, MLP_HEADS // 2, 2, CHUNK, CHUNK).transpose(0, 1, 3, 2, 4)
    ws_cat = ws_cat.reshape(DEPTH, MLP_HEADS // 2, CHUNK, 2 * CHUNK).astype(BF16)
    bs_full = jnp.repeat(b_s.transpose(0, 2, 1), HEAD_DIM, axis=2)
    gq_t = jnp.tile(g_q, (1, MXU_DIM // HEAD_DIM))[:, None]
    gk_t = jnp.tile(g_k, (1, LANES // HEAD_DIM))[:, None]

    xc = ctx
    for l in range(DEPTH):
        last = l == DEPTH - 1
        in_args = (g_pre_mix[l][None], w_in_b[l], gq_t[l], gk_t[l])
        mlp_args = (g_sg[l][None], b_sg[l][None], ws_cat[l], bs_full[l])
        ffn_args = (g_post_mix[l][None], g_pre_ffn[l][None], g_post_ffn[l][None],
                    w_out_b[l], w_gu[l], w_down[l])

        qct, kc, vct, mlp_c = _inproj(xc, mods, l, b, *in_args, cos_c, sin_c, *mlp_args,
                                      tm=n_ctx, use_rope=False)
        qxt, kx, vxt, mlp_x = _inproj(x, mods, l, None, *in_args, cos_t, sin_t, *mlp_args,
                                      tm=1024, use_rope=True)
        bound = (BOUND_SLACK * LOG2E * math.sqrt(HEAD_DIM)) * jnp.max(jnp.abs(g_q[l])) * jnp.max(jnp.abs(g_k[l]))
        bound = bound.reshape(1)
        attend = functools.partial(_attention, bound, tq=1024, bk=256)
        attn_x = lax.cond(2.0 * bound[0] <= SAFE_EXP2_RANGE,
                          lambda: attend(qxt, (kc, kx), (vct, vxt), online=False),
                          lambda: attend(qxt, (kc, kx), (vct, vxt), online=True))
        x = _outffn(attn_x, mlp_x, x, mods, l, None, *ffn_args, tm=512)
        if not last:
            attn_c = _attention(bound, qct, (kc,), (vct,), tq=n_ctx, bk=512, online=True)
            xc = _outffn(attn_c, mlp_c, xc, mods, l, b, *ffn_args, tm=n_ctx)
    return x
```

```python
import functools
import math

import jax
import jax.numpy as jnp
from jax import lax
from jax.experimental import pallas as pl
from jax.experimental.pallas import tpu as pltpu

D_MODEL = 1024
DEPTH = 4
GRID_W = 64
HEAD_DIM = 64
ATTN_HEADS = 8
KV_HEADS = 2
GQA_GROUP = ATTN_HEADS // KV_HEADS
ATTN_WIDTH = ATTN_HEADS * HEAD_DIM
KV_WIDTH = KV_HEADS * HEAD_DIM
MLP_HEADS = 8
MLP_WIDTH = MLP_HEADS * HEAD_DIM
CHUNK = 128
IN_WIDTH = ATTN_WIDTH + 2 * KV_WIDTH + 2 * MLP_WIDTH
FFN_HIDDEN = 2816
N_MOD = 6
ROPE_THETA = 10000.0
ROPE_AXIS_DIM = HEAD_DIM // 2
EPS = 1e-6

LANES = 128
MXU_DIM = 256
MOD_ROWS = 16
MOD_SLOTS = 8
V_ROWS = HEAD_DIM + 16
FFN_CHUNK = MXU_DIM
N_FFN_CHUNKS = FFN_HIDDEN // FFN_CHUNK
SUB_ROWS = MXU_DIM
PROJ_AHEAD = 2
FFN_AHEAD = 2
QK_AHEAD = 4
LOG2E = math.log2(math.e)
BOUND_SLACK = 1.02
SAFE_EXP2_RANGE = 120.0
VMEM_LIMIT = 56 * 1024 * 1024

F32 = jnp.float32
BF16 = jnp.bfloat16


def _const_spec(shape):
    zeros = (0,) * len(shape)
    return pl.BlockSpec(shape, lambda *_: zeros, pipeline_mode=pl.Buffered(1))


def _rms(t, g):
    return t * lax.rsqrt(jnp.mean(t * t, axis=-1, keepdims=True) + EPS) * g


def _mod_kernel(c_ref, w_ref, b_ref, o_ref):
    cv = c_ref[...]
    act = (cv * jax.nn.sigmoid(cv)).astype(BF16)
    o_ref[0] = jnp.dot(act, w_ref[0].astype(BF16), preferred_element_type=F32) + b_ref[0]


def _modulation(cvec, w_mod, b_mod):
    tn = 1536
    n_out = N_MOD * D_MODEL
    return pl.pallas_call(
        _mod_kernel,
        grid=(DEPTH, n_out // tn),
        in_specs=[
            pl.BlockSpec((MOD_ROWS, D_MODEL), lambda l, j: (0, 0)),
            pl.BlockSpec((1, D_MODEL, tn), lambda l, j: (l, 0, j)),
            pl.BlockSpec((1, 1, tn), lambda l, j: (l, 0, j)),
        ],
        out_specs=pl.BlockSpec((1, MOD_ROWS, tn), lambda l, j: (l, 0, j)),
        out_shape=jax.ShapeDtypeStruct((DEPTH, MOD_ROWS, n_out), F32),
        compiler_params=pltpu.CompilerParams(
            dimension_semantics=("arbitrary", "arbitrary"), vmem_limit_bytes=VMEM_LIMIT),
        name="modulation",
    )(cvec, w_mod, b_mod.reshape(DEPTH, 1, n_out))


def _group_sum(t2, ones_bd):
    return jnp.dot(t2.astype(BF16), ones_bd, preferred_element_type=F32)


def _swap_halves16(t):
    lane = lax.broadcasted_iota(jnp.int32, t.shape, 1)
    return jnp.where((lane % 32) < 16, pltpu.roll(t, LANES - 16, 1), pltpu.roll(t, 16, 1))


def _inproj_kernel(x_ref, mod_ref, gpre_ref, w_ref, gq_ref, gk_ref, cos_ref, sin_ref,
                   gsg_ref, bsg_ref, ws_ref, bs_ref,
                   qt_ref, k_ref, vt_ref, mlp_ref, *, tm, use_rope):
    mod = mod_ref[0, 0]
    shift, scale = mod[0:1], mod[1:2]
    ts = min(tm, SUB_ROWS)

    ri = lax.broadcasted_iota(jnp.int32, (MXU_DIM, MXU_DIM), 0) // HEAD_DIM
    ci = lax.broadcasted_iota(jnp.int32, (MXU_DIM, MXU_DIM), 1) // HEAD_DIM
    ones_bd = jnp.where(ri == ci, 1.0, 0.0).astype(BF16)

    def project(r):
        h = (_rms(x_ref[0, r], gpre_ref[...]) * (1.0 + scale) + shift).astype(BF16)
        return jnp.dot(h, w_ref[...], preferred_element_type=F32)

    def head_norm(t, g):
        w = t.shape[1]
        ss = _group_sum(t * t, ones_bd[:w, :w])
        return t * lax.rsqrt(ss * (1.0 / HEAD_DIM) + EPS) * g

    def finish(r, p):
        def rope(t):
            if not use_rope:
                return t
            return t * cos_ref[r, :] + _swap_halves16(t) * sin_ref[r, :]

        q_scale = LOG2E / math.sqrt(HEAD_DIM)
        for half in range(ATTN_WIDTH // MXU_DIM):
            qn = head_norm(p[:, half * MXU_DIM:(half + 1) * MXU_DIM], gq_ref[...])
            for pair in range(MXU_DIM // LANES):
                tt = (rope(qn[:, pair * LANES:(pair + 1) * LANES]) * q_scale).T
                h0 = half * (MXU_DIM // HEAD_DIM) + pair * 2
                qt_ref[0, h0, :, r] = tt[:HEAD_DIM].astype(BF16)
                qt_ref[0, h0 + 1, :, r] = tt[HEAD_DIM:].astype(BF16)

        kn = rope(head_norm(p[:, ATTN_WIDTH:ATTN_WIDTH + KV_WIDTH], gk_ref[...]))
        k_ref[0, 0, r, :] = kn[:, :HEAD_DIM].astype(BF16)
        k_ref[0, 1, r, :] = pltpu.roll(kn, HEAD_DIM, 1)[:, :HEAD_DIM].astype(BF16)

        vvt = p[:, ATTN_WIDTH + KV_WIDTH:ATTN_WIDTH + 2 * KV_WIDTH].T
        sub = lax.broadcasted_iota(jnp.int32, (V_ROWS - HEAD_DIM, p.shape[0]), 0)
        tail = jnp.where(sub == 0, 1.0, 0.0).astype(BF16)
        for kvh in range(KV_HEADS):
            vt_ref[0, kvh, :HEAD_DIM, r] = vvt[kvh * HEAD_DIM:(kvh + 1) * HEAD_DIM].astype(BF16)
            vt_ref[0, kvh, HEAD_DIM:, r] = tail

        z = p[:, ATTN_WIDTH + 2 * KV_WIDTH:]
        gc = math.sqrt(2.0 / math.pi)
        hz = 0.5 * z
        z = hz + hz * jnp.tanh(z * (gc + (gc * 0.044715) * (z * z)))
        u = z[:, :MLP_WIDTH]
        v2 = z[:, MLP_WIDTH:]
        mu = jnp.mean(v2, axis=-1, keepdims=True)
        vc = v2 - mu
        var = jnp.mean(vc * vc, axis=-1, keepdims=True)
        vln = vc * lax.rsqrt(var + EPS) * gsg_ref[...] + bsg_ref[...]
        lane_w = lax.broadcasted_iota(jnp.int32, vln.shape, 1)
        even_head = (lane_w % LANES) < HEAD_DIM
        v_even = jnp.where(even_head, vln, 0.0).astype(BF16)
        v_odd = jnp.where(even_head, 0.0, vln).astype(BF16)
        return u, v_even, v_odd

    def spatial_mix(r, u, v_even, v_odd):
        for ch in range(u.shape[0] // CHUNK):
            rows = slice(ch * CHUNK, (ch + 1) * CHUNK)
            out_rows = slice(r.start + ch * CHUNK, r.start + (ch + 1) * CHUNK)
            for jb in range(MLP_WIDTH // LANES):
                cols = slice(jb * LANES, (jb + 1) * LANES)
                rhs = jnp.concatenate([v_even[rows, cols], v_odd[rows, cols]], axis=0)
                s = jnp.dot(ws_ref[jb], rhs, preferred_element_type=F32)
                mlp_ref[0, out_rows, cols] = (u[rows, cols] * (s + bs_ref[:, cols])).astype(BF16)

    subs = [slice(i, i + ts) for i in range(0, tm, ts)]
    pending = [project(r) for r in subs[:PROJ_AHEAD]]
    for i, r in enumerate(subs):
        p_cur = pending.pop(0)
        if i + PROJ_AHEAD < len(subs):
            pending.append(project(subs[i + PROJ_AHEAD]))
        spatial_mix(r, *finish(r, p_cur))


def _inproj(x, mods, layer, stream_row, g_pre, w_in, gq_t, gk_t, cos_t, sin_t,
            g_sg, b_sg, ws_cat, bs_full, *, tm, use_rope):
    b, n, _ = x.shape
    if stream_row is None:
        mod_map = lambda i, j: (layer, i, 0, 0)
    else:
        mod_map = lambda i, j: (layer, stream_row, 0, 0)
    kern = functools.partial(_inproj_kernel, tm=tm, use_rope=use_rope)
    return pl.pallas_call(
        kern,
        grid=(b, n // tm),
        in_specs=[
            pl.BlockSpec((1, tm, D_MODEL), lambda i, j: (i, j, 0)),
            pl.BlockSpec((1, 1, MOD_SLOTS, D_MODEL), mod_map),
            _const_spec((1, D_MODEL)),
            _const_spec((D_MODEL, IN_WIDTH)),
            _const_spec((1, MXU_DIM)),
            _const_spec((1, LANES)),
            pl.BlockSpec((tm, LANES), lambda i, j: (j, 0)),
            pl.BlockSpec((tm, LANES), lambda i, j: (j, 0)),
            _const_spec((1, MLP_WIDTH)),
            _const_spec((1, MLP_WIDTH)),
            _const_spec((MLP_WIDTH // LANES, CHUNK, 2 * CHUNK)),
            _const_spec((CHUNK, MLP_WIDTH)),
        ],
        out_specs=[
            pl.BlockSpec((1, ATTN_HEADS, HEAD_DIM, tm), lambda i, j: (i, 0, 0, j)),
            pl.BlockSpec((1, KV_HEADS, tm, HEAD_DIM), lambda i, j: (i, 0, j, 0)),
            pl.BlockSpec((1, KV_HEADS, V_ROWS, tm), lambda i, j: (i, 0, 0, j)),
            pl.BlockSpec((1, tm, MLP_WIDTH), lambda i, j: (i, j, 0)),
        ],
        out_shape=[
            jax.ShapeDtypeStruct((b, ATTN_HEADS, HEAD_DIM, n), BF16),
            jax.ShapeDtypeStruct((b, KV_HEADS, n, HEAD_DIM), BF16),
            jax.ShapeDtypeStruct((b, KV_HEADS, V_ROWS, n), BF16),
            jax.ShapeDtypeStruct((b, n, MLP_WIDTH), BF16),
        ],
        compiler_params=pltpu.CompilerParams(
            dimension_semantics=("arbitrary", "arbitrary"), vmem_limit_bytes=VMEM_LIMIT),
        name="inproj",
    )(x, mods, g_pre, w_in, gq_t, gk_t, cos_t, sin_t, g_sg, b_sg, ws_cat, bs_full)


def _attn_kernel(*refs, tq, seg_lens, bk, online):
    n_seg = len(seg_lens)
    bound_ref, qt_ref = refs[0], refs[1]
    k_refs = refs[2:2 + n_seg]
    vt_refs = refs[2 + n_seg:2 + 2 * n_seg]
    o_ref = refs[2 + 2 * n_seg]

    ts = min(tq, SUB_ROWS)
    subs = [slice(i, i + ts) for i in range(0, tq, ts)]
    m = {(t, h): jnp.full((1, ts), -jnp.inf, F32) for t in range(len(subs)) for h in range(GQA_GROUP)}
    acc = {(t, h): jnp.zeros((V_ROWS, ts), F32) for t in range(len(subs)) for h in range(GQA_GROUP)}

    def finish(t, h, s, vt_ref, start, size, last):
        vt_blk = vt_ref[0, 0, :, start:start + size]
        if online:
            m_new = jnp.maximum(m[t, h], jnp.max(s, axis=0, keepdims=True))
            alpha = jnp.exp2(m[t, h] - m_new)
            pt = jnp.exp2(s - m_new).astype(BF16)
            acc[t, h] = alpha * acc[t, h] + jnp.dot(vt_blk, pt, preferred_element_type=F32)
            m[t, h] = m_new
        else:
            pt = jnp.exp2(s - bound_ref[0]).astype(BF16)
            acc[t, h] = acc[t, h] + jnp.dot(vt_blk, pt, preferred_element_type=F32)
        if last:
            pairs = []
            for pair in range(GQA_GROUP // 2):
                ot = [acc[t, g][:HEAD_DIM] / acc[t, g][HEAD_DIM:HEAD_DIM + 1] for g in (2 * pair, 2 * pair + 1)]
                pairs.append(jnp.concatenate(ot, axis=0).T)
            o_ref[0, subs[t], :] = jnp.concatenate(pairs, axis=1).astype(BF16)

    blocks = [(k_ref, vt_ref, start, min(bk, seg - start))
              for k_ref, vt_ref, seg in zip(k_refs, vt_refs, seg_lens) for start in range(0, seg, bk)]
    pending = []
    for t, r in enumerate(subs):
        for bi, (k_ref, vt_ref, start, size) in enumerate(blocks):
            for h in range(GQA_GROUP):
                k_blk = k_ref[0, 0, start:start + size, :]
                s = jnp.dot(k_blk, qt_ref[0, h, :, r], preferred_element_type=F32)
                last = bi == len(blocks) - 1 and h == GQA_GROUP - 1
                pending.append((t, h, s, vt_ref, start, size, last))
                if len(pending) > QK_AHEAD:
                    finish(*pending.pop(0))
    while pending:
        finish(*pending.pop(0))


def _attention(bound, qt, ks, vts, *, tq, bk, online):
    b, _, _, n = qt.shape
    seg_lens = tuple(k.shape[2] for k in ks)
    kern = functools.partial(_attn_kernel, tq=tq, seg_lens=seg_lens, bk=bk, online=online)
    in_specs = [pl.BlockSpec(memory_space=pltpu.SMEM),
                pl.BlockSpec((1, GQA_GROUP, HEAD_DIM, tq), lambda i, g, j: (i, g, 0, j))]
    in_specs += [pl.BlockSpec((1, 1, s, HEAD_DIM), lambda i, g, j: (i, g, 0, 0)) for s in seg_lens]
    in_specs += [pl.BlockSpec((1, 1, V_ROWS, s), lambda i, g, j: (i, g, 0, 0)) for s in seg_lens]
    return pl.pallas_call(
        kern,
        grid=(b, KV_HEADS, n // tq),
        in_specs=in_specs,
        out_specs=pl.BlockSpec((1, tq, GQA_GROUP * HEAD_DIM), lambda i, g, j: (i, j, g)),
        out_shape=jax.ShapeDtypeStruct((b, n, ATTN_WIDTH), BF16),
        compiler_params=pltpu.CompilerParams(
            dimension_semantics=("arbitrary", "arbitrary", "arbitrary"), vmem_limit_bytes=VMEM_LIMIT),
        name="attention",
    )(bound, qt, *ks, *vts)


def _outffn_kernel(attn_ref, mlp_ref, x_ref, mod_ref, gpm_ref, gpf_ref, gqf_ref,
                   wo_ref, wgu_ref, wd_ref, o_ref, *, tm):
    mod = mod_ref[0, 0]
    gate_mix, shift, scale, gate_ffn = mod[2:3], mod[3:4], mod[4:5], mod[5:6]

    ts = min(tm, SUB_ROWS)
    subs = [slice(i, i + ts) for i in range(0, tm, ts)]

    def out_proj(r):
        return (jnp.dot(attn_ref[0, r], wo_ref[:ATTN_WIDTH], preferred_element_type=F32)
                + jnp.dot(mlp_ref[0, r], wo_ref[ATTN_WIDTH:], preferred_element_type=F32))

    def ffn(h):
        def gate_up(ci):
            cols = slice(ci * FFN_CHUNK, (ci + 1) * FFN_CHUNK)
            up_cols = slice(FFN_HIDDEN + ci * FFN_CHUNK, FFN_HIDDEN + (ci + 1) * FFN_CHUNK)
            return (jnp.dot(h, wgu_ref[:, cols], preferred_element_type=F32),
                    jnp.dot(h, wgu_ref[:, up_cols], preferred_element_type=F32))

        pending = [gate_up(ci) for ci in range(FFN_AHEAD)]
        y = None
        for ci in range(N_FFN_CHUNKS):
            g, u = pending.pop(0)
            if ci + FFN_AHEAD < N_FFN_CHUNKS:
                pending.append(gate_up(ci + FFN_AHEAD))
            a = (g * jax.nn.sigmoid(g) * u).astype(BF16)
            d = jnp.dot(a, wd_ref[ci * FFN_CHUNK:(ci + 1) * FFN_CHUNK], preferred_element_type=F32)
            y = d if y is None else y + d
        return y

    out_next = out_proj(subs[0])
    for i, r in enumerate(subs):
        out = out_next
        if i + 1 < len(subs):
            out_next = out_proj(subs[i + 1])
        x1 = x_ref[0, r] + gate_mix * _rms(out, gpm_ref[...])
        h = (_rms(x1, gpf_ref[...]) * (1.0 + scale) + shift).astype(BF16)
        o_ref[0, r] = x1 + gate_ffn * _rms(ffn(h), gqf_ref[...])


def _outffn(attn, mlp, x, mods, layer, stream_row, g_post_mix, g_pre_ffn, g_post_ffn,
            w_out, w_gu, w_down, *, tm):
    b, n, _ = x.shape
    if stream_row is None:
        mod_map = lambda i, j: (layer, i, 0, 0)
    else:
        mod_map = lambda i, j: (layer, stream_row, 0, 0)
    row_spec = lambda w: pl.BlockSpec((1, tm, w), lambda i, j: (i, j, 0))
    return pl.pallas_call(
        functools.partial(_outffn_kernel, tm=tm),
        grid=(b, n // tm),
        in_specs=[
            row_spec(ATTN_WIDTH),
            row_spec(MLP_WIDTH),
            row_spec(D_MODEL),
            pl.BlockSpec((1, 1, MOD_SLOTS, D_MODEL), mod_map),
            _const_spec((1, D_MODEL)),
            _const_spec((1, D_MODEL)),
            _const_spec((1, D_MODEL)),
            _const_spec((ATTN_WIDTH + MLP_WIDTH, D_MODEL)),
            _const_spec((D_MODEL, 2 * FFN_HIDDEN)),
            _const_spec((FFN_HIDDEN, D_MODEL)),
        ],
        out_specs=row_spec(D_MODEL),
        out_shape=jax.ShapeDtypeStruct((b, n, D_MODEL), F32),
        compiler_params=pltpu.CompilerParams(
            dimension_semantics=("arbitrary", "arbitrary"), vmem_limit_bytes=VMEM_LIMIT),
        name="outffn",
    )(attn, mlp, x, mods, g_post_mix, g_pre_ffn, g_post_ffn, w_out, w_gu, w_down)


def _rope_tables(n):
    rows = n // GRID_W
    pos_row = jnp.broadcast_to(jnp.arange(rows, dtype=F32)[:, None], (rows, GRID_W)).reshape(-1)
    pos_col = jnp.broadcast_to(jnp.arange(GRID_W, dtype=F32)[None, :], (rows, GRID_W)).reshape(-1)
    inv = ROPE_THETA ** (-jnp.arange(0, ROPE_AXIS_DIM, 2, dtype=F32) / ROPE_AXIS_DIM)
    ang_r = pos_row[:, None] * inv
    ang_c = pos_col[:, None] * inv
    cos64 = jnp.concatenate([jnp.cos(ang_r)] * 2 + [jnp.cos(ang_c)] * 2, axis=-1)
    sin64 = jnp.concatenate([-jnp.sin(ang_r), jnp.sin(ang_r), -jnp.sin(ang_c), jnp.sin(ang_c)], axis=-1)
    return jnp.tile(cos64, (1, 2)), jnp.tile(sin64, (1, 2))


def kernel(x, c, ctx, c_ctx, w_mod, b_mod, g_pre_mix, g_post_mix, g_pre_ffn, g_post_ffn,
           w_in, g_q, g_k, g_sg, b_sg, w_s, b_s, w_out, w_ffn_in, w_ffn_out):
    b, n, _ = x.shape
    n_ctx = ctx.shape[1]
    cos_t, sin_t = _rope_tables(n)
    cos_c, sin_c = cos_t[:n_ctx], sin_t[:n_ctx]

    cvec = jnp.concatenate([c, c_ctx[None], jnp.zeros((MOD_ROWS - b - 1, D_MODEL), F32)], axis=0)
    mods = _modulation(cvec, w_mod, b_mod)
    mods = mods.reshape(DEPTH, MOD_ROWS, N_MOD, D_MODEL)
    mods = jnp.pad(mods, ((0, 0), (0, 0), (0, MOD_SLOTS - N_MOD), (0, 0)))

    w_in_b = w_in.astype(BF16)
    w_out_b = w_out.astype(BF16)
    w_gu = w_ffn_in.astype(BF16)
    w_down = w_ffn_out.astype(BF16)
    ws_cat = w_s.reshape(DEPTH, MLP_HEADS // 2, 2, CHUNK, CHUNK).transpose(0, 1, 3, 2, 4)
    ws_cat = ws_cat.reshape(DEPTH, MLP_HEADS // 2, CHUNK, 2 * CHUNK).astype(BF16)
    bs_full = jnp.repeat(b_s.transpose(0, 2, 1), HEAD_DIM, axis=2)
    gq_t = jnp.tile(g_q, (1, MXU_DIM // HEAD_DIM))[:, None]
    gk_t = jnp.tile(g_k, (1, LANES // HEAD_DIM))[:, None]

    xc = ctx
    for l in range(DEPTH):
        last = l == DEPTH - 1
        in_args = (g_pre_mix[l][None], w_in_b[l], gq_t[l], gk_t[l])
        mlp_args = (g_sg[l][None], b_sg[l][None], ws_cat[l], bs_full[l])
        ffn_args = (g_post_mix[l][None], g_pre_ffn[l][None], g_post_ffn[l][None],
                    w_out_b[l], w_gu[l], w_down[l])

        qct, kc, vct, mlp_c = _inproj(xc, mods, l, b, *in_args, cos_c, sin_c, *mlp_args,
                                      tm=n_ctx, use_rope=False)
        qxt, kx, vxt, mlp_x = _inproj(x, mods, l, None, *in_args, cos_t, sin_t, *mlp_args,
                                      tm=1024, use_rope=True)
        bound = (BOUND_SLACK * LOG2E * math.sqrt(HEAD_DIM)) * jnp.max(jnp.abs(g_q[l])) * jnp.max(jnp.abs(g_k[l]))
        bound = bound.reshape(1)
        attend = functools.partial(_attention, bound, tq=1024, bk=256)
        attn_x = lax.cond(2.0 * bound[0] <= SAFE_EXP2_RANGE,
                          lambda: attend(qxt, (kc, kx), (vct, vxt), online=False),
                          lambda: attend(qxt, (kc, kx), (vct, vxt), online=True))
        x = _outffn(attn_x, mlp_x, x, mods, l, None, *ffn_args, tm=1024)
        if not last:
            attn_c = _attention(bound, qct, (kc,), (vct,), tq=n_ctx, bk=512, online=True)
            xc = _outffn(attn_c, mlp_c, xc, mods, l, b, *ffn_args, tm=n_ctx)
    return x
```

```python
import functools
import math

import jax
import jax.numpy as jnp
from jax import lax
from jax.experimental import pallas as pl
from jax.experimental.pallas import tpu as pltpu

D_MODEL = 1024
DEPTH = 4
GRID_W = 64
HEAD_DIM = 64
ATTN_HEADS = 8
KV_HEADS = 2
GQA_GROUP = ATTN_HEADS // KV_HEADS
ATTN_WIDTH = ATTN_HEADS * HEAD_DIM
KV_WIDTH = KV_HEADS * HEAD_DIM
MLP_HEADS = 8
MLP_WIDTH = MLP_HEADS * HEAD_DIM
CHUNK = 128
IN_WIDTH = ATTN_WIDTH + 2 * KV_WIDTH + 2 * MLP_WIDTH
FFN_HIDDEN = 2816
N_MOD = 6
ROPE_THETA = 10000.0
ROPE_AXIS_DIM = HEAD_DIM // 2
EPS = 1e-6

LANES = 128
MXU_DIM = 256
MOD_ROWS = 16
MOD_SLOTS = 8
SUBLANES = 8
FFN_CHUNK = MXU_DIM
N_FFN_CHUNKS = FFN_HIDDEN // FFN_CHUNK
SUB_ROWS = MXU_DIM
PROJ_AHEAD = 2
FFN_AHEAD = 2
QK_AHEAD = 4
LOG2E = math.log2(math.e)
BOUND_SLACK = 1.02
SAFE_EXP2_RANGE = 120.0
VMEM_LIMIT = 56 * 1024 * 1024

F32 = jnp.float32
BF16 = jnp.bfloat16


def _const_spec(shape):
    zeros = (0,) * len(shape)
    return pl.BlockSpec(shape, lambda *_: zeros, pipeline_mode=pl.Buffered(1))


def _rms(t, g):
    return t * lax.rsqrt(jnp.mean(t * t, axis=-1, keepdims=True) + EPS) * g


def _mod_kernel(c_ref, w_ref, b_ref, o_ref):
    cv = c_ref[...]
    act = (cv * jax.nn.sigmoid(cv)).astype(BF16)
    o_ref[0] = jnp.dot(act, w_ref[0].astype(BF16), preferred_element_type=F32) + b_ref[0]


def _modulation(cvec, w_mod, b_mod):
    tn = 1536
    n_out = N_MOD * D_MODEL
    return pl.pallas_call(
        _mod_kernel,
        grid=(DEPTH, n_out // tn),
        in_specs=[
            pl.BlockSpec((MOD_ROWS, D_MODEL), lambda l, j: (0, 0)),
            pl.BlockSpec((1, D_MODEL, tn), lambda l, j: (l, 0, j)),
            pl.BlockSpec((1, 1, tn), lambda l, j: (l, 0, j)),
        ],
        out_specs=pl.BlockSpec((1, MOD_ROWS, tn), lambda l, j: (l, 0, j)),
        out_shape=jax.ShapeDtypeStruct((DEPTH, MOD_ROWS, n_out), F32),
        compiler_params=pltpu.CompilerParams(
            dimension_semantics=("arbitrary", "arbitrary"), vmem_limit_bytes=VMEM_LIMIT),
        name="modulation",
    )(cvec, w_mod, b_mod.reshape(DEPTH, 1, n_out))


def _group_sum(t2, ones_bd):
    return jnp.dot(t2.astype(BF16), ones_bd, preferred_element_type=F32)


def _swap_halves16(t):
    lane = lax.broadcasted_iota(jnp.int32, t.shape, 1)
    return jnp.where((lane % 32) < 16, pltpu.roll(t, LANES - 16, 1), pltpu.roll(t, 16, 1))


def _inproj_kernel(x_ref, mod_ref, gpre_ref, w_ref, gq_ref, gk_ref, cos_ref, sin_ref,
                   gsg_ref, bsg_ref, ws_ref, bs_ref,
                   qt_ref, k_ref, vt_ref, mlp_ref, *, tm, use_rope):
    mod = mod_ref[0, 0]
    shift, scale = mod[0:1], mod[1:2]
    ts = min(tm, SUB_ROWS)

    ri = lax.broadcasted_iota(jnp.int32, (MXU_DIM, MXU_DIM), 0) // HEAD_DIM
    ci = lax.broadcasted_iota(jnp.int32, (MXU_DIM, MXU_DIM), 1) // HEAD_DIM
    ones_bd = jnp.where(ri == ci, 1.0, 0.0).astype(BF16)

    def project(r):
        h = (_rms(x_ref[0, r], gpre_ref[...]) * (1.0 + scale) + shift).astype(BF16)
        return jnp.dot(h, w_ref[...], preferred_element_type=F32)

    def head_norm(t, g):
        w = t.shape[1]
        ss = _group_sum(t * t, ones_bd[:w, :w])
        return t * lax.rsqrt(ss * (1.0 / HEAD_DIM) + EPS) * g

    def finish(r, p):
        def rope(t):
            if not use_rope:
                return t
            return t * cos_ref[r, :] + _swap_halves16(t) * sin_ref[r, :]

        q_scale = LOG2E / math.sqrt(HEAD_DIM)
        for half in range(ATTN_WIDTH // MXU_DIM):
            qn = head_norm(p[:, half * MXU_DIM:(half + 1) * MXU_DIM], gq_ref[...])
            for pair in range(MXU_DIM // LANES):
                tt = (rope(qn[:, pair * LANES:(pair + 1) * LANES]) * q_scale).T
                h0 = half * (MXU_DIM // HEAD_DIM) + pair * 2
                qt_ref[0, h0, :, r] = tt[:HEAD_DIM].astype(BF16)
                qt_ref[0, h0 + 1, :, r] = tt[HEAD_DIM:].astype(BF16)

        kn = rope(head_norm(p[:, ATTN_WIDTH:ATTN_WIDTH + KV_WIDTH], gk_ref[...]))
        k_ref[0, 0, r, :] = kn[:, :HEAD_DIM].astype(BF16)
        k_ref[0, 1, r, :] = pltpu.roll(kn, HEAD_DIM, 1)[:, :HEAD_DIM].astype(BF16)

        vvt = p[:, ATTN_WIDTH + KV_WIDTH:ATTN_WIDTH + 2 * KV_WIDTH].T
        for kvh in range(KV_HEADS):
            vt_ref[0, kvh, :, r] = vvt[kvh * HEAD_DIM:(kvh + 1) * HEAD_DIM].astype(BF16)

        z = p[:, ATTN_WIDTH + 2 * KV_WIDTH:]
        gc = math.sqrt(2.0 / math.pi)
        hz = 0.5 * z
        z = hz + hz * jnp.tanh(z * (gc + (gc * 0.044715) * (z * z)))
        u = z[:, :MLP_WIDTH]
        v2 = z[:, MLP_WIDTH:]
        mu = jnp.mean(v2, axis=-1, keepdims=True)
        vc = v2 - mu
        var = jnp.mean(vc * vc, axis=-1, keepdims=True)
        vln = vc * lax.rsqrt(var + EPS) * gsg_ref[...] + bsg_ref[...]
        lane_w = lax.broadcasted_iota(jnp.int32, vln.shape, 1)
        even_head = (lane_w % LANES) < HEAD_DIM
        v_even = jnp.where(even_head, vln, 0.0).astype(BF16)
        v_odd = jnp.where(even_head, 0.0, vln).astype(BF16)
        return u, v_even, v_odd

    def spatial_mix(r, u, v_even, v_odd):
        for ch in range(u.shape[0] // CHUNK):
            rows = slice(ch * CHUNK, (ch + 1) * CHUNK)
            out_rows = slice(r.start + ch * CHUNK, r.start + (ch + 1) * CHUNK)
            for jb in range(MLP_WIDTH // LANES):
                cols = slice(jb * LANES, (jb + 1) * LANES)
                rhs = jnp.concatenate([v_even[rows, cols], v_odd[rows, cols]], axis=0)
                s = jnp.dot(ws_ref[jb], rhs, preferred_element_type=F32)
                mlp_ref[0, out_rows, cols] = (u[rows, cols] * (s + bs_ref[:, cols])).astype(BF16)

    subs = [slice(i, i + ts) for i in range(0, tm, ts)]
    pending = [project(r) for r in subs[:PROJ_AHEAD]]
    for i, r in enumerate(subs):
        p_cur = pending.pop(0)
        if i + PROJ_AHEAD < len(subs):
            pending.append(project(subs[i + PROJ_AHEAD]))
        spatial_mix(r, *finish(r, p_cur))


def _inproj(x, mods, layer, stream_row, g_pre, w_in, gq_t, gk_t, cos_t, sin_t,
            g_sg, b_sg, ws_cat, bs_full, *, tm, use_rope):
    b, n, _ = x.shape
    if stream_row is None:
        mod_map = lambda i, j: (layer, i, 0, 0)
    else:
        mod_map = lambda i, j: (layer, stream_row, 0, 0)
    kern = functools.partial(_inproj_kernel, tm=tm, use_rope=use_rope)
    return pl.pallas_call(
        kern,
        grid=(b, n // tm),
        in_specs=[
            pl.BlockSpec((1, tm, D_MODEL), lambda i, j: (i, j, 0)),
            pl.BlockSpec((1, 1, MOD_SLOTS, D_MODEL), mod_map),
            _const_spec((1, D_MODEL)),
            _const_spec((D_MODEL, IN_WIDTH)),
            _const_spec((1, MXU_DIM)),
            _const_spec((1, LANES)),
            pl.BlockSpec((tm, LANES), lambda i, j: (j, 0)),
            pl.BlockSpec((tm, LANES), lambda i, j: (j, 0)),
            _const_spec((1, MLP_WIDTH)),
            _const_spec((1, MLP_WIDTH)),
            _const_spec((MLP_WIDTH // LANES, CHUNK, 2 * CHUNK)),
            _const_spec((CHUNK, MLP_WIDTH)),
        ],
        out_specs=[
            pl.BlockSpec((1, ATTN_HEADS, HEAD_DIM, tm), lambda i, j: (i, 0, 0, j)),
            pl.BlockSpec((1, KV_HEADS, tm, HEAD_DIM), lambda i, j: (i, 0, j, 0)),
            pl.BlockSpec((1, KV_HEADS, HEAD_DIM, tm), lambda i, j: (i, 0, 0, j)),
            pl.BlockSpec((1, tm, MLP_WIDTH), lambda i, j: (i, j, 0)),
        ],
        out_shape=[
            jax.ShapeDtypeStruct((b, ATTN_HEADS, HEAD_DIM, n), BF16),
            jax.ShapeDtypeStruct((b, KV_HEADS, n, HEAD_DIM), BF16),
            jax.ShapeDtypeStruct((b, KV_HEADS, HEAD_DIM, n), BF16),
            jax.ShapeDtypeStruct((b, n, MLP_WIDTH), BF16),
        ],
        compiler_params=pltpu.CompilerParams(
            dimension_semantics=("arbitrary", "arbitrary"), vmem_limit_bytes=VMEM_LIMIT),
        name="inproj",
    )(x, mods, g_pre, w_in, gq_t, gk_t, cos_t, sin_t, g_sg, b_sg, ws_cat, bs_full)


def _attn_kernel(*refs, tq, seg_lens, bk, online):
    n_seg = len(seg_lens)
    bound_ref, qt_ref = refs[0], refs[1]
    k_refs = refs[2:2 + n_seg]
    vt_refs = refs[2 + n_seg:2 + 2 * n_seg]
    o_ref = refs[2 + 2 * n_seg]

    ts = min(tq, SUB_ROWS)
    subs = [slice(i, i + ts) for i in range(0, tq, ts)]
    m = {(t, h): jnp.full((1, ts), -jnp.inf, F32) for t in range(len(subs)) for h in range(GQA_GROUP)}
    acc = {(t, h): jnp.zeros((HEAD_DIM, ts), F32) for t in range(len(subs)) for h in range(GQA_GROUP)}
    den = {(t, h): jnp.zeros((SUBLANES, ts), F32) for t in range(len(subs)) for h in range(GQA_GROUP)}

    def key_sum(pf):
        return jnp.sum(pf.reshape(pf.shape[0] // SUBLANES, SUBLANES, pf.shape[1]), axis=0)

    def finish(t, h, s, vt_ref, start, size, last):
        vt_blk = vt_ref[0, 0, :, start:start + size]
        if online:
            m_new = jnp.maximum(m[t, h], jnp.max(s, axis=0, keepdims=True))
            alpha = jnp.exp2(m[t, h] - m_new)
            pf = jnp.exp2(s - m_new)
            acc[t, h] = alpha * acc[t, h] + jnp.dot(vt_blk, pf.astype(BF16), preferred_element_type=F32)
            den[t, h] = alpha * den[t, h] + key_sum(pf)
            m[t, h] = m_new
        else:
            pf = jnp.exp2(s - bound_ref[0])
            acc[t, h] = acc[t, h] + jnp.dot(vt_blk, pf.astype(BF16), preferred_element_type=F32)
            den[t, h] = den[t, h] + key_sum(pf)
        if last:
            pairs = []
            for pair in range(GQA_GROUP // 2):
                ot = [acc[t, g] / jnp.sum(den[t, g], axis=0, keepdims=True) for g in (2 * pair, 2 * pair + 1)]
                pairs.append(jnp.concatenate(ot, axis=0).T)
            o_ref[0, subs[t], :] = jnp.concatenate(pairs, axis=1).astype(BF16)

    blocks = [(k_ref, vt_ref, start, min(bk, seg - start))
              for k_ref, vt_ref, seg in zip(k_refs, vt_refs, seg_lens) for start in range(0, seg, bk)]
    pending = []
    for t, r in enumerate(subs):
        for bi, (k_ref, vt_ref, start, size) in enumerate(blocks):
            for h in range(GQA_GROUP):
                k_blk = k_ref[0, 0, start:start + size, :]
                s = jnp.dot(k_blk, qt_ref[0, h, :, r], preferred_element_type=F32)
                last = bi == len(blocks) - 1 and h == GQA_GROUP - 1
                pending.append((t, h, s, vt_ref, start, size, last))
                if len(pending) > QK_AHEAD:
                    finish(*pending.pop(0))
    while pending:
        finish(*pending.pop(0))


def _attention(bound, qt, ks, vts, *, tq, bk, online):
    b, _, _, n = qt.shape
    seg_lens = tuple(k.shape[2] for k in ks)
    kern = functools.partial(_attn_kernel, tq=tq, seg_lens=seg_lens, bk=bk, online=online)
    in_specs = [pl.BlockSpec(memory_space=pltpu.SMEM),
                pl.BlockSpec((1, GQA_GROUP, HEAD_DIM, tq), lambda i, g, j: (i, g, 0, j))]
    in_specs += [pl.BlockSpec((1, 1, s, HEAD_DIM), lambda i, g, j: (i, g, 0, 0)) for s in seg_lens]
    in_specs += [pl.BlockSpec((1, 1, HEAD_DIM, s), lambda i, g, j: (i, g, 0, 0)) for s in seg_lens]
    return pl.pallas_call(
        kern,
        grid=(b, KV_HEADS, n // tq),
        in_specs=in_specs,
        out_specs=pl.BlockSpec((1, tq, GQA_GROUP * HEAD_DIM), lambda i, g, j: (i, j, g)),
        out_shape=jax.ShapeDtypeStruct((b, n, ATTN_WIDTH), BF16),
        compiler_params=pltpu.CompilerParams(
            dimension_semantics=("arbitrary", "arbitrary", "arbitrary"), vmem_limit_bytes=VMEM_LIMIT),
        name="attention",
    )(bound, qt, *ks, *vts)


def _outffn_kernel(attn_ref, mlp_ref, x_ref, mod_ref, gpm_ref, gpf_ref, gqf_ref,
                   wo_ref, wgu_ref, wd_ref, o_ref, *, tm):
    mod = mod_ref[0, 0]
    gate_mix, shift, scale, gate_ffn = mod[2:3], mod[3:4], mod[4:5], mod[5:6]

    ts = min(tm, SUB_ROWS)
    subs = [slice(i, i + ts) for i in range(0, tm, ts)]

    def out_proj(r):
        return (jnp.dot(attn_ref[0, r], wo_ref[:ATTN_WIDTH], preferred_element_type=F32)
                + jnp.dot(mlp_ref[0, r], wo_ref[ATTN_WIDTH:], preferred_element_type=F32))

    def ffn(h):
        def gate_up(ci):
            cols = slice(ci * FFN_CHUNK, (ci + 1) * FFN_CHUNK)
            up_cols = slice(FFN_HIDDEN + ci * FFN_CHUNK, FFN_HIDDEN + (ci + 1) * FFN_CHUNK)
            return (jnp.dot(h, wgu_ref[:, cols], preferred_element_type=F32),
                    jnp.dot(h, wgu_ref[:, up_cols], preferred_element_type=F32))

        pending = [gate_up(ci) for ci in range(FFN_AHEAD)]
        y = None
        for ci in range(N_FFN_CHUNKS):
            g, u = pending.pop(0)
            if ci + FFN_AHEAD < N_FFN_CHUNKS:
                pending.append(gate_up(ci + FFN_AHEAD))
            a = (g * jax.nn.sigmoid(g) * u).astype(BF16)
            d = jnp.dot(a, wd_ref[ci * FFN_CHUNK:(ci + 1) * FFN_CHUNK], preferred_element_type=F32)
            y = d if y is None else y + d
        return y

    out_next = out_proj(subs[0])
    for i, r in enumerate(subs):
        out = out_next
        if i + 1 < len(subs):
            out_next = out_proj(subs[i + 1])
        x1 = x_ref[0, r] + gate_mix * _rms(out, gpm_ref[...])
        h = (_rms(x1, gpf_ref[...]) * (1.0 + scale) + shift).astype(BF16)
        o_ref[0, r] = x1 + gate_ffn * _rms(ffn(h), gqf_ref[...])


def _outffn(attn, mlp, x, mods, layer, stream_row, g_post_mix, g_pre_ffn, g_post_ffn,
            w_out, w_gu, w_down, *, tm):
    b, n, _ = x.shape
    if stream_row is None:
        mod_map = lambda i, j: (layer, i, 0, 0)
    else:
        mod_map = lambda i, j: (layer, stream_row, 0, 0)
    row_spec = lambda w: pl.BlockSpec((1, tm, w), lambda i, j: (i, j, 0))
    return pl.pallas_call(
        functools.partial(_outffn_kernel, tm=tm),
        grid=(b, n // tm),
        in_specs=[
            row_spec(ATTN_WIDTH),
            row_spec(MLP_WIDTH),
            row_spec(D_MODEL),
            pl.BlockSpec((1, 1, MOD_SLOTS, D_MODEL), mod_map),
            _const_spec((1, D_MODEL)),
            _const_spec((1, D_MODEL)),
            _const_spec((1, D_MODEL)),
            _const_spec((ATTN_WIDTH + MLP_WIDTH, D_MODEL)),
            _const_spec((D_MODEL, 2 * FFN_HIDDEN)),
            _const_spec((FFN_HIDDEN, D_MODEL)),
        ],
        out_specs=row_spec(D_MODEL),
        out_shape=jax.ShapeDtypeStruct((b, n, D_MODEL), F32),
        compiler_params=pltpu.CompilerParams(
            dimension_semantics=("arbitrary", "arbitrary"), vmem_limit_bytes=VMEM_LIMIT),
        name="outffn",
    )(attn, mlp, x, mods, g_post_mix, g_pre_ffn, g_post_ffn, w_out, w_gu, w_down)


def _rope_tables(n):
    rows = n // GRID_W
    pos_row = jnp.broadcast_to(jnp.arange(rows, dtype=F32)[:, None], (rows, GRID_W)).reshape(-1)
    pos_col = jnp.broadcast_to(jnp.arange(GRID_W, dtype=F32)[None, :], (rows, GRID_W)).reshape(-1)
    inv = ROPE_THETA ** (-jnp.arange(0, ROPE_AXIS_DIM, 2, dtype=F32) / ROPE_AXIS_DIM)
    ang_r = pos_row[:, None] * inv
    ang_c = pos_col[:, None] * inv
    cos64 = jnp.concatenate([jnp.cos(ang_r)] * 2 + [jnp.cos(ang_c)] * 2, axis=-1)
    sin64 = jnp.concatenate([-jnp.sin(ang_r), jnp.sin(ang_r), -jnp.sin(ang_c), jnp.sin(ang_c)], axis=-1)
    return jnp.tile(cos64, (1, 2)), jnp.tile(sin64, (1, 2))


def kernel(x, c, ctx, c_ctx, w_mod, b_mod, g_pre_mix, g_post_mix, g_pre_ffn, g_post_ffn,
           w_in, g_q, g_k, g_sg, b_sg, w_s, b_s, w_out, w_ffn_in, w_ffn_out):
    b, n, _ = x.shape
    n_ctx = ctx.shape[1]
    cos_t, sin_t = _rope_tables(n)
    cos_c, sin_c = cos_t[:n_ctx], sin_t[:n_ctx]

    cvec = jnp.concatenate([c, c_ctx[None], jnp.zeros((MOD_ROWS - b - 1, D_MODEL), F32)], axis=0)
    mods = _modulation(cvec, w_mod, b_mod)
    mods = mods.reshape(DEPTH, MOD_ROWS, N_MOD, D_MODEL)
    mods = jnp.pad(mods, ((0, 0), (0, 0), (0, MOD_SLOTS - N_MOD), (0, 0)))

    w_in_b = w_in.astype(BF16)
    w_out_b = w_out.astype(BF16)
    w_gu = w_ffn_in.astype(BF16)
    w_down = w_ffn_out.astype(BF16)
    ws_cat = w_s.reshape(DEPTH, MLP_HEADS // 2, 2, CHUNK, CHUNK).transpose(0, 1, 3, 2, 4)
    ws_cat = ws_cat.reshape(DEPTH, MLP_HEADS // 2, CHUNK, 2 * CHUNK).astype(BF16)
    bs_full = jnp.repeat(b_s.transpose(0, 2, 1), HEAD_DIM, axis=2)
    gq_t = jnp.tile(g_q, (1, MXU_DIM // HEAD_DIM))[:, None]
    gk_t = jnp.tile(g_k, (1, LANES // HEAD_DIM))[:, None]

    xc = ctx
    for l in range(DEPTH):
        last = l == DEPTH - 1
        in_args = (g_pre_mix[l][None], w_in_b[l], gq_t[l], gk_t[l])
        mlp_args = (g_sg[l][None], b_sg[l][None], ws_cat[l], bs_full[l])
        ffn_args = (g_post_mix[l][None], g_pre_ffn[l][None], g_post_ffn[l][None],
                    w_out_b[l], w_gu[l], w_down[l])

        qct, kc, vct, mlp_c = _inproj(xc, mods, l, b, *in_args, cos_c, sin_c, *mlp_args,
                                      tm=n_ctx, use_rope=False)
        qxt, kx, vxt, mlp_x = _inproj(x, mods, l, None, *in_args, cos_t, sin_t, *mlp_args,
                                      tm=1024, use_rope=True)
        bound = (BOUND_SLACK * LOG2E * math.sqrt(HEAD_DIM)) * jnp.max(jnp.abs(g_q[l])) * jnp.max(jnp.abs(g_k[l]))
        bound = bound.reshape(1)
        attend = functools.partial(_attention, bound, tq=1024, bk=256)
        attn_x = lax.cond(2.0 * bound[0] <= SAFE_EXP2_RANGE,
                          lambda: attend(qxt, (kc, kx), (vct, vxt), online=False),
                          lambda: attend(qxt, (kc, kx), (vct, vxt), online=True))
        x = _outffn(attn_x, mlp_x, x, mods, l, None, *ffn_args, tm=1024)
        if not last:
            attn_c = _attention(bound, qct, (kc,), (vct,), tq=n_ctx, bk=512, online=True)
            xc = _outffn(attn_c, mlp_c, xc, mods, l, b, *ffn_args, tm=n_ctx)
    return x
```

```python
import functools
import math

import jax
import jax.numpy as jnp
from jax import lax
from jax.experimental import pallas as pl
from jax.experimental.pallas import tpu as pltpu

D_MODEL = 1024
DEPTH = 4
GRID_W = 64
HEAD_DIM = 64
ATTN_HEADS = 8
KV_HEADS = 2
GQA_GROUP = ATTN_HEADS // KV_HEADS
ATTN_WIDTH = ATTN_HEADS * HEAD_DIM
KV_WIDTH = KV_HEADS * HEAD_DIM
MLP_HEADS = 8
MLP_WIDTH = MLP_HEADS * HEAD_DIM
CHUNK = 128
IN_WIDTH = ATTN_WIDTH + 2 * KV_WIDTH + 2 * MLP_WIDTH
FFN_HIDDEN = 2816
N_MOD = 6
ROPE_THETA = 10000.0
ROPE_AXIS_DIM = HEAD_DIM // 2
EPS = 1e-6

LANES = 128
MXU_DIM = 256
MOD_ROWS = 16
MOD_SLOTS = 8
SUBLANES = 8
FFN_CHUNK = MXU_DIM
N_FFN_CHUNKS = FFN_HIDDEN // FFN_CHUNK
ROW_TILE = 4 * MXU_DIM
SUB_ROWS = MXU_DIM
KEY_BLOCK = MXU_DIM
PROJ_AHEAD = 2
FFN_AHEAD = 2
QK_AHEAD = 4
LOG2E = math.log2(math.e)
BOUND_SLACK = 1.02
SAFE_EXP2_RANGE = 120.0
VMEM_LIMIT = 56 * 1024 * 1024

F32 = jnp.float32
BF16 = jnp.bfloat16


def _const_spec(shape):
    zeros = (0,) * len(shape)
    return pl.BlockSpec(shape, lambda *_: zeros, pipeline_mode=pl.Buffered(1))


def _rms(t, g):
    return t * lax.rsqrt(jnp.mean(t * t, axis=-1, keepdims=True) + EPS) * g


def _mod_kernel(c_ref, w_ref, b_ref, o_ref):
    cv = c_ref[...]
    act = (cv * jax.nn.sigmoid(cv)).astype(BF16)
    o_ref[0] = jnp.dot(act, w_ref[0].astype(BF16), preferred_element_type=F32) + b_ref[0]


def _modulation(cvec, w_mod, b_mod):
    tn = 1536
    n_out = N_MOD * D_MODEL
    return pl.pallas_call(
        _mod_kernel,
        grid=(DEPTH, n_out // tn),
        in_specs=[
            pl.BlockSpec((MOD_ROWS, D_MODEL), lambda l, j: (0, 0)),
            pl.BlockSpec((1, D_MODEL, tn), lambda l, j: (l, 0, j)),
            pl.BlockSpec((1, 1, tn), lambda l, j: (l, 0, j)),
        ],
        out_specs=pl.BlockSpec((1, MOD_ROWS, tn), lambda l, j: (l, 0, j)),
        out_shape=jax.ShapeDtypeStruct((DEPTH, MOD_ROWS, n_out), F32),
        compiler_params=pltpu.CompilerParams(
            dimension_semantics=("arbitrary", "arbitrary"), vmem_limit_bytes=VMEM_LIMIT),
        name="modulation",
    )(cvec, w_mod, b_mod.reshape(DEPTH, 1, n_out))


def _group_sum(t2, ones_bd):
    return jnp.dot(t2.astype(BF16), ones_bd, preferred_element_type=F32)


def _swap_halves16(t):
    lane = lax.broadcasted_iota(jnp.int32, t.shape, 1)
    return jnp.where((lane % 32) < 16, pltpu.roll(t, LANES - 16, 1), pltpu.roll(t, 16, 1))


def _inproj_kernel(x_ref, mod_ref, gpre_ref, w_ref, gq_ref, gk_ref, cos_ref, sin_ref,
                   gsg_ref, bsg_ref, ws_ref, bs_ref,
                   qt_ref, k_ref, vt_ref, mlp_ref, *, tm, use_rope):
    mod = mod_ref[0, 0]
    shift, scale = mod[0:1], mod[1:2]
    ts = min(tm, SUB_ROWS)

    ri = lax.broadcasted_iota(jnp.int32, (MXU_DIM, MXU_DIM), 0) // HEAD_DIM
    ci = lax.broadcasted_iota(jnp.int32, (MXU_DIM, MXU_DIM), 1) // HEAD_DIM
    ones_bd = jnp.where(ri == ci, 1.0, 0.0).astype(BF16)

    def project(r):
        h = (_rms(x_ref[0, r], gpre_ref[...]) * (1.0 + scale) + shift).astype(BF16)
        return jnp.dot(h, w_ref[...], preferred_element_type=F32)

    def head_norm(t, g):
        w = t.shape[1]
        ss = _group_sum(t * t, ones_bd[:w, :w])
        return t * lax.rsqrt(ss * (1.0 / HEAD_DIM) + EPS) * g

    def finish(r, p):
        def rope(t):
            if not use_rope:
                return t
            return t * cos_ref[r, :] + _swap_halves16(t) * sin_ref[r, :]

        q_scale = LOG2E / math.sqrt(HEAD_DIM)
        for half in range(ATTN_WIDTH // MXU_DIM):
            qn = head_norm(p[:, half * MXU_DIM:(half + 1) * MXU_DIM], gq_ref[...])
            for pair in range(MXU_DIM // LANES):
                tt = (rope(qn[:, pair * LANES:(pair + 1) * LANES]) * q_scale).T
                h0 = half * (MXU_DIM // HEAD_DIM) + pair * 2
                qt_ref[0, h0, :, r] = tt[:HEAD_DIM].astype(BF16)
                qt_ref[0, h0 + 1, :, r] = tt[HEAD_DIM:].astype(BF16)

        kn = rope(head_norm(p[:, ATTN_WIDTH:ATTN_WIDTH + KV_WIDTH], gk_ref[...]))
        k_ref[0, 0, r, :] = kn[:, :HEAD_DIM].astype(BF16)
        k_ref[0, 1, r, :] = pltpu.roll(kn, HEAD_DIM, 1)[:, :HEAD_DIM].astype(BF16)

        vvt = p[:, ATTN_WIDTH + KV_WIDTH:ATTN_WIDTH + 2 * KV_WIDTH].T
        for kvh in range(KV_HEADS):
            vt_ref[0, kvh, :, r] = vvt[kvh * HEAD_DIM:(kvh + 1) * HEAD_DIM].astype(BF16)

        z = p[:, ATTN_WIDTH + 2 * KV_WIDTH:]
        gc = math.sqrt(2.0 / math.pi)
        hz = 0.5 * z
        z = hz + hz * jnp.tanh(z * (gc + (gc * 0.044715) * (z * z)))
        u = z[:, :MLP_WIDTH]
        v2 = z[:, MLP_WIDTH:]
        mu = jnp.mean(v2, axis=-1, keepdims=True)
        vc = v2 - mu
        var = jnp.mean(vc * vc, axis=-1, keepdims=True)
        vln = vc * lax.rsqrt(var + EPS) * gsg_ref[...] + bsg_ref[...]
        lane_w = lax.broadcasted_iota(jnp.int32, vln.shape, 1)
        even_head = (lane_w % LANES) < HEAD_DIM
        v_even = jnp.where(even_head, vln, 0.0).astype(BF16)
        v_odd = jnp.where(even_head, 0.0, vln).astype(BF16)
        return u, v_even, v_odd

    def spatial_mix(r, u, v_even, v_odd):
        for ch in range(u.shape[0] // CHUNK):
            rows = slice(ch * CHUNK, (ch + 1) * CHUNK)
            out_rows = slice(r.start + ch * CHUNK, r.start + (ch + 1) * CHUNK)
            for jb in range(MLP_WIDTH // LANES):
                cols = slice(jb * LANES, (jb + 1) * LANES)
                rhs = jnp.concatenate([v_even[rows, cols], v_odd[rows, cols]], axis=0)
                s = jnp.dot(ws_ref[jb], rhs, preferred_element_type=F32)
                mlp_ref[0, out_rows, cols] = (u[rows, cols] * (s + bs_ref[:, cols])).astype(BF16)

    subs = [slice(i, i + ts) for i in range(0, tm, ts)]
    pending = [project(r) for r in subs[:PROJ_AHEAD]]
    for i, r in enumerate(subs):
        p_cur = pending.pop(0)
        if i + PROJ_AHEAD < len(subs):
            pending.append(project(subs[i + PROJ_AHEAD]))
        spatial_mix(r, *finish(r, p_cur))


def _inproj(x, mods, layer, stream_row, g_pre, w_in, gq_t, gk_t, cos_t, sin_t,
            g_sg, b_sg, ws_cat, bs_full, *, tm, use_rope):
    b, n, _ = x.shape
    if stream_row is None:
        mod_map = lambda i, j: (layer, i, 0, 0)
    else:
        mod_map = lambda i, j: (layer, stream_row, 0, 0)
    kern = functools.partial(_inproj_kernel, tm=tm, use_rope=use_rope)
    return pl.pallas_call(
        kern,
        grid=(b, n // tm),
        in_specs=[
            pl.BlockSpec((1, tm, D_MODEL), lambda i, j: (i, j, 0)),
            pl.BlockSpec((1, 1, MOD_SLOTS, D_MODEL), mod_map),
            _const_spec((1, D_MODEL)),
            _const_spec((D_MODEL, IN_WIDTH)),
            _const_spec((1, MXU_DIM)),
            _const_spec((1, LANES)),
            pl.BlockSpec((tm, LANES), lambda i, j: (j, 0)),
            pl.BlockSpec((tm, LANES), lambda i, j: (j, 0)),
            _const_spec((1, MLP_WIDTH)),
            _const_spec((1, MLP_WIDTH)),
            _const_spec((MLP_WIDTH // LANES, CHUNK, 2 * CHUNK)),
            _const_spec((CHUNK, MLP_WIDTH)),
        ],
        out_specs=[
            pl.BlockSpec((1, ATTN_HEADS, HEAD_DIM, tm), lambda i, j: (i, 0, 0, j)),
            pl.BlockSpec((1, KV_HEADS, tm, HEAD_DIM), lambda i, j: (i, 0, j, 0)),
            pl.BlockSpec((1, KV_HEADS, HEAD_DIM, tm), lambda i, j: (i, 0, 0, j)),
            pl.BlockSpec((1, tm, MLP_WIDTH), lambda i, j: (i, j, 0)),
        ],
        out_shape=[
            jax.ShapeDtypeStruct((b, ATTN_HEADS, HEAD_DIM, n), BF16),
            jax.ShapeDtypeStruct((b, KV_HEADS, n, HEAD_DIM), BF16),
            jax.ShapeDtypeStruct((b, KV_HEADS, HEAD_DIM, n), BF16),
            jax.ShapeDtypeStruct((b, n, MLP_WIDTH), BF16),
        ],
        compiler_params=pltpu.CompilerParams(
            dimension_semantics=("arbitrary", "arbitrary"), vmem_limit_bytes=VMEM_LIMIT),
        name="inproj",
    )(x, mods, g_pre, w_in, gq_t, gk_t, cos_t, sin_t, g_sg, b_sg, ws_cat, bs_full)


def _attn_kernel(*refs, tq, seg_lens, bk, online):
    n_seg = len(seg_lens)
    bound_ref, qt_ref = refs[0], refs[1]
    k_refs = refs[2:2 + n_seg]
    vt_refs = refs[2 + n_seg:2 + 2 * n_seg]
    o_ref = refs[2 + 2 * n_seg]

    ts = min(tq, SUB_ROWS)
    subs = [slice(i, i + ts) for i in range(0, tq, ts)]
    m = {(t, h): jnp.full((1, ts), -jnp.inf, F32) for t in range(len(subs)) for h in range(GQA_GROUP)}
    acc = {(t, h): jnp.zeros((HEAD_DIM, ts), F32) for t in range(len(subs)) for h in range(GQA_GROUP)}
    den = {(t, h): jnp.zeros((SUBLANES, ts), F32) for t in range(len(subs)) for h in range(GQA_GROUP)}

    def key_sum(pf):
        return jnp.sum(pf.reshape(pf.shape[0] // SUBLANES, SUBLANES, pf.shape[1]), axis=0)

    def finish(t, h, s, vt_ref, start, size, last):
        vt_blk = vt_ref[0, 0, :, start:start + size]
        if online:
            m_new = jnp.maximum(m[t, h], jnp.max(s, axis=0, keepdims=True))
            alpha = jnp.exp2(m[t, h] - m_new)
            pf = jnp.exp2(s - m_new)
            acc[t, h] = alpha * acc[t, h] + jnp.dot(vt_blk, pf.astype(BF16), preferred_element_type=F32)
            den[t, h] = alpha * den[t, h] + key_sum(pf)
            m[t, h] = m_new
        else:
            pf = jnp.exp2(s - bound_ref[0])
            acc[t, h] = acc[t, h] + jnp.dot(vt_blk, pf.astype(BF16), preferred_element_type=F32)
            den[t, h] = den[t, h] + key_sum(pf)
        if last:
            pairs = []
            for pair in range(GQA_GROUP // 2):
                ot = [acc[t, g] / jnp.sum(den[t, g], axis=0, keepdims=True) for g in (2 * pair, 2 * pair + 1)]
                pairs.append(jnp.concatenate(ot, axis=0).T)
            o_ref[0, subs[t], :] = jnp.concatenate(pairs, axis=1).astype(BF16)

    blocks = [(k_ref, vt_ref, start, min(bk, seg - start))
              for k_ref, vt_ref, seg in zip(k_refs, vt_refs, seg_lens) for start in range(0, seg, bk)]
    pending = []
    for t, r in enumerate(subs):
        for bi, (k_ref, vt_ref, start, size) in enumerate(blocks):
            for h in range(GQA_GROUP):
                k_blk = k_ref[0, 0, start:start + size, :]
                s = jnp.dot(k_blk, qt_ref[0, h, :, r], preferred_element_type=F32)
                last = bi == len(blocks) - 1 and h == GQA_GROUP - 1
                pending.append((t, h, s, vt_ref, start, size, last))
                if len(pending) > QK_AHEAD:
                    finish(*pending.pop(0))
    while pending:
        finish(*pending.pop(0))


def _attention(bound, qt, ks, vts, *, tq, bk, online):
    b, _, _, n = qt.shape
    seg_lens = tuple(k.shape[2] for k in ks)
    kern = functools.partial(_attn_kernel, tq=tq, seg_lens=seg_lens, bk=bk, online=online)
    in_specs = [pl.BlockSpec(memory_space=pltpu.SMEM),
                pl.BlockSpec((1, GQA_GROUP, HEAD_DIM, tq), lambda i, g, j: (i, g, 0, j))]
    in_specs += [pl.BlockSpec((1, 1, s, HEAD_DIM), lambda i, g, j: (i, g, 0, 0)) for s in seg_lens]
    in_specs += [pl.BlockSpec((1, 1, HEAD_DIM, s), lambda i, g, j: (i, g, 0, 0)) for s in seg_lens]
    return pl.pallas_call(
        kern,
        grid=(b, KV_HEADS, n // tq),
        in_specs=in_specs,
        out_specs=pl.BlockSpec((1, tq, GQA_GROUP * HEAD_DIM), lambda i, g, j: (i, j, g)),
        out_shape=jax.ShapeDtypeStruct((b, n, ATTN_WIDTH), BF16),
        compiler_params=pltpu.CompilerParams(
            dimension_semantics=("arbitrary", "arbitrary", "arbitrary"), vmem_limit_bytes=VMEM_LIMIT),
        name="attention",
    )(bound, qt, *ks, *vts)


def _outffn_kernel(attn_ref, mlp_ref, x_ref, mod_ref, gpm_ref, gpf_ref, gqf_ref,
                   wo_ref, wgu_ref, wd_ref, o_ref, *, tm):
    mod = mod_ref[0, 0]
    gate_mix, shift, scale, gate_ffn = mod[2:3], mod[3:4], mod[4:5], mod[5:6]

    ts = min(tm, SUB_ROWS)
    subs = [slice(i, i + ts) for i in range(0, tm, ts)]

    def out_proj(r):
        return (jnp.dot(attn_ref[0, r], wo_ref[:ATTN_WIDTH], preferred_element_type=F32)
                + jnp.dot(mlp_ref[0, r], wo_ref[ATTN_WIDTH:], preferred_element_type=F32))

    def ffn(h):
        def gate_up(ci):
            cols = slice(ci * FFN_CHUNK, (ci + 1) * FFN_CHUNK)
            up_cols = slice(FFN_HIDDEN + ci * FFN_CHUNK, FFN_HIDDEN + (ci + 1) * FFN_CHUNK)
            return (jnp.dot(h, wgu_ref[:, cols], preferred_element_type=F32),
                    jnp.dot(h, wgu_ref[:, up_cols], preferred_element_type=F32))

        pending = [gate_up(ci) for ci in range(FFN_AHEAD)]
        y = None
        for ci in range(N_FFN_CHUNKS):
            g, u = pending.pop(0)
            if ci + FFN_AHEAD < N_FFN_CHUNKS:
                pending.append(gate_up(ci + FFN_AHEAD))
            a = (g * jax.nn.sigmoid(g) * u).astype(BF16)
            d = jnp.dot(a, wd_ref[ci * FFN_CHUNK:(ci + 1) * FFN_CHUNK], preferred_element_type=F32)
            y = d if y is None else y + d
        return y

    out_next = out_proj(subs[0])
    for i, r in enumerate(subs):
        out = out_next
        if i + 1 < len(subs):
            out_next = out_proj(subs[i + 1])
        x1 = x_ref[0, r] + gate_mix * _rms(out, gpm_ref[...])
        h = (_rms(x1, gpf_ref[...]) * (1.0 + scale) + shift).astype(BF16)
        o_ref[0, r] = x1 + gate_ffn * _rms(ffn(h), gqf_ref[...])


def _outffn(attn, mlp, x, mods, layer, stream_row, g_post_mix, g_pre_ffn, g_post_ffn,
            w_out, w_gu, w_down, *, tm):
    b, n, _ = x.shape
    if stream_row is None:
        mod_map = lambda i, j: (layer, i, 0, 0)
    else:
        mod_map = lambda i, j: (layer, stream_row, 0, 0)
    row_spec = lambda w: pl.BlockSpec((1, tm, w), lambda i, j: (i, j, 0))
    return pl.pallas_call(
        functools.partial(_outffn_kernel, tm=tm),
        grid=(b, n // tm),
        in_specs=[
            row_spec(ATTN_WIDTH),
            row_spec(MLP_WIDTH),
            row_spec(D_MODEL),
            pl.BlockSpec((1, 1, MOD_SLOTS, D_MODEL), mod_map),
            _const_spec((1, D_MODEL)),
            _const_spec((1, D_MODEL)),
            _const_spec((1, D_MODEL)),
            _const_spec((ATTN_WIDTH + MLP_WIDTH, D_MODEL)),
            _const_spec((D_MODEL, 2 * FFN_HIDDEN)),
            _const_spec((FFN_HIDDEN, D_MODEL)),
        ],
        out_specs=row_spec(D_MODEL),
        out_shape=jax.ShapeDtypeStruct((b, n, D_MODEL), F32),
        compiler_params=pltpu.CompilerParams(
            dimension_semantics=("arbitrary", "arbitrary"), vmem_limit_bytes=VMEM_LIMIT),
        name="outffn",
    )(attn, mlp, x, mods, g_post_mix, g_pre_ffn, g_post_ffn, w_out, w_gu, w_down)


def _rope_tables(n):
    rows = n // GRID_W
    pos_row = jnp.broadcast_to(jnp.arange(rows, dtype=F32)[:, None], (rows, GRID_W)).reshape(-1)
    pos_col = jnp.broadcast_to(jnp.arange(GRID_W, dtype=F32)[None, :], (rows, GRID_W)).reshape(-1)
    inv = ROPE_THETA ** (-jnp.arange(0, ROPE_AXIS_DIM, 2, dtype=F32) / ROPE_AXIS_DIM)
    ang_r = pos_row[:, None] * inv
    ang_c = pos_col[:, None] * inv
    cos64 = jnp.concatenate([jnp.cos(ang_r)] * 2 + [jnp.cos(ang_c)] * 2, axis=-1)
    sin64 = jnp.concatenate([-jnp.sin(ang_r), jnp.sin(ang_r), -jnp.sin(ang_c), jnp.sin(ang_c)], axis=-1)
    return jnp.tile(cos64, (1, 2)), jnp.tile(sin64, (1, 2))


def kernel(x, c, ctx, c_ctx, w_mod, b_mod, g_pre_mix, g_post_mix, g_pre_ffn, g_post_ffn,
           w_in, g_q, g_k, g_sg, b_sg, w_s, b_s, w_out, w_ffn_in, w_ffn_out):
    b, n, _ = x.shape
    n_ctx = ctx.shape[1]
    assert n % ROW_TILE == 0 and n_ctx % SUB_ROWS == 0 and n_ctx <= ROW_TILE and b < MOD_ROWS
    cos_t, sin_t = _rope_tables(n)
    cos_c, sin_c = cos_t[:n_ctx], sin_t[:n_ctx]

    cvec = jnp.concatenate([c, c_ctx[None], jnp.zeros((MOD_ROWS - b - 1, D_MODEL), F32)], axis=0)
    mods = _modulation(cvec, w_mod, b_mod)
    mods = mods.reshape(DEPTH, MOD_ROWS, N_MOD, D_MODEL)
    mods = jnp.pad(mods, ((0, 0), (0, 0), (0, MOD_SLOTS - N_MOD), (0, 0)))

    w_in_b = w_in.astype(BF16)
    w_out_b = w_out.astype(BF16)
    w_gu = w_ffn_in.astype(BF16)
    w_down = w_ffn_out.astype(BF16)
    ws_cat = w_s.reshape(DEPTH, MLP_HEADS // 2, 2, CHUNK, CHUNK).transpose(0, 1, 3, 2, 4)
    ws_cat = ws_cat.reshape(DEPTH, MLP_HEADS // 2, CHUNK, 2 * CHUNK).astype(BF16)
    bs_full = jnp.repeat(b_s.transpose(0, 2, 1), HEAD_DIM, axis=2)
    gq_t = jnp.tile(g_q, (1, MXU_DIM // HEAD_DIM))[:, None]
    gk_t = jnp.tile(g_k, (1, LANES // HEAD_DIM))[:, None]

    xc = ctx
    for l in range(DEPTH):
        last = l == DEPTH - 1
        in_args = (g_pre_mix[l][None], w_in_b[l], gq_t[l], gk_t[l])
        mlp_args = (g_sg[l][None], b_sg[l][None], ws_cat[l], bs_full[l])
        ffn_args = (g_post_mix[l][None], g_pre_ffn[l][None], g_post_ffn[l][None],
                    w_out_b[l], w_gu[l], w_down[l])

        qct, kc, vct, mlp_c = _inproj(xc, mods, l, b, *in_args, cos_c, sin_c, *mlp_args,
                                      tm=n_ctx, use_rope=False)
        qxt, kx, vxt, mlp_x = _inproj(x, mods, l, None, *in_args, cos_t, sin_t, *mlp_args,
                                      tm=ROW_TILE, use_rope=True)
        bound = (BOUND_SLACK * LOG2E * math.sqrt(HEAD_DIM)) * jnp.max(jnp.abs(g_q[l])) * jnp.max(jnp.abs(g_k[l]))
        bound = bound.reshape(1)
        attend = functools.partial(_attention, bound, tq=ROW_TILE, bk=KEY_BLOCK)
        attn_x = lax.cond(2.0 * bound[0] <= SAFE_EXP2_RANGE,
                          lambda: attend(qxt, (kc, kx), (vct, vxt), online=False),
                          lambda: attend(qxt, (kc, kx), (vct, vxt), online=True))
        x = _outffn(attn_x, mlp_x, x, mods, l, None, *ffn_args, tm=ROW_TILE)
        if not last:
            attn_c = _attention(bound, qct, (kc,), (vct,), tq=n_ctx, bk=KEY_BLOCK, online=True)
            xc = _outffn(attn_c, mlp_c, xc, mods, l, b, *ffn_args, tm=n_ctx)
    return x
```

```python
import functools
import math

import jax
import jax.numpy as jnp
from jax import lax
from jax.experimental import pallas as pl
from jax.experimental.pallas import tpu as pltpu

D_MODEL = 1024
DEPTH = 4
GRID_W = 64
HEAD_DIM = 64
ATTN_HEADS = 8
KV_HEADS = 2
GQA_GROUP = ATTN_HEADS // KV_HEADS
ATTN_WIDTH = ATTN_HEADS * HEAD_DIM
KV_WIDTH = KV_HEADS * HEAD_DIM
MLP_HEADS = 8
MLP_WIDTH = MLP_HEADS * HEAD_DIM
CHUNK = 128
IN_WIDTH = ATTN_WIDTH + 2 * KV_WIDTH + 2 * MLP_WIDTH
FFN_HIDDEN = 2816
N_MOD = 6
ROPE_THETA = 10000.0
ROPE_AXIS_DIM = HEAD_DIM // 2
EPS = 1e-6

LANES = 128
MXU_DIM = 256
MOD_ROWS = 16
MOD_SLOTS = 8
SUBLANES = 8
FFN_CHUNK = MXU_DIM
N_FFN_CHUNKS = FFN_HIDDEN // FFN_CHUNK
ROW_TILE = 4 * MXU_DIM
SUB_ROWS = MXU_DIM
KEY_BLOCK = MXU_DIM
PROJ_AHEAD = 2
FFN_AHEAD = 2
QK_AHEAD = 4
LOG2E = math.log2(math.e)
BOUND_SLACK = 1.02
SAFE_EXP2_RANGE = 120.0
VMEM_LIMIT = 56 * 1024 * 1024

F32 = jnp.float32
BF16 = jnp.bfloat16


def _const_spec(shape):
    zeros = (0,) * len(shape)
    return pl.BlockSpec(shape, lambda *_: zeros, pipeline_mode=pl.Buffered(1))


def _rms(t, g):
    return t * lax.rsqrt(jnp.mean(t * t, axis=-1, keepdims=True) + EPS) * g


def _mod_kernel(c_ref, w_ref, b_ref, o_ref):
    cv = c_ref[...]
    act = (cv * jax.nn.sigmoid(cv)).astype(BF16)
    o_ref[0] = jnp.dot(act, w_ref[0].astype(BF16), preferred_element_type=F32) + b_ref[0]


def _modulation(cvec, w_mod, b_mod):
    tn = 1536
    n_out = N_MOD * D_MODEL
    return pl.pallas_call(
        _mod_kernel,
        grid=(DEPTH, n_out // tn),
        in_specs=[
            pl.BlockSpec((MOD_ROWS, D_MODEL), lambda l, j: (0, 0)),
            pl.BlockSpec((1, D_MODEL, tn), lambda l, j: (l, 0, j)),
            pl.BlockSpec((1, 1, tn), lambda l, j: (l, 0, j)),
        ],
        out_specs=pl.BlockSpec((1, MOD_ROWS, tn), lambda l, j: (l, 0, j)),
        out_shape=jax.ShapeDtypeStruct((DEPTH, MOD_ROWS, n_out), F32),
        compiler_params=pltpu.CompilerParams(
            dimension_semantics=("arbitrary", "arbitrary"), vmem_limit_bytes=VMEM_LIMIT),
        name="modulation",
    )(cvec, w_mod, b_mod.reshape(DEPTH, 1, n_out))


def _group_sum(t2, ones_bd):
    return jnp.dot(t2.astype(BF16), ones_bd, preferred_element_type=F32)


def _swap_halves16(t):
    lane = lax.broadcasted_iota(jnp.int32, t.shape, 1)
    return jnp.where((lane % 32) < 16, pltpu.roll(t, LANES - 16, 1), pltpu.roll(t, 16, 1))


def _inproj_kernel(x_ref, mod_ref, gpre_ref, w_ref, gq_ref, gk_ref, cos_ref, sin_ref,
                   gsg_ref, bsg_ref, ws_ref, bs_ref,
                   qt_ref, k_ref, vt_ref, mlp_ref, *, tm, use_rope):
    mod = mod_ref[0, 0]
    shift, scale = mod[0:1], mod[1:2]
    ts = min(tm, SUB_ROWS)

    ri = lax.broadcasted_iota(jnp.int32, (MXU_DIM, MXU_DIM), 0) // HEAD_DIM
    ci = lax.broadcasted_iota(jnp.int32, (MXU_DIM, MXU_DIM), 1) // HEAD_DIM
    ones_bd = jnp.where(ri == ci, 1.0, 0.0).astype(BF16)

    def project(r):
        h = (_rms(x_ref[0, r], gpre_ref[...]) * (1.0 + scale) + shift).astype(BF16)
        return jnp.dot(h, w_ref[...], preferred_element_type=F32)

    def head_norm(t, g):
        w = t.shape[1]
        ss = _group_sum(t * t, ones_bd[:w, :w])
        return t * lax.rsqrt(ss * (1.0 / HEAD_DIM) + EPS) * g

    def finish(r, p):
        def rope(t):
            if not use_rope:
                return t
            return t * cos_ref[r, :] + _swap_halves16(t) * sin_ref[r, :]

        q_scale = LOG2E / math.sqrt(HEAD_DIM)
        for half in range(ATTN_WIDTH // MXU_DIM):
            qn = head_norm(p[:, half * MXU_DIM:(half + 1) * MXU_DIM], gq_ref[...])
            for pair in range(MXU_DIM // LANES):
                tt = (rope(qn[:, pair * LANES:(pair + 1) * LANES]) * q_scale).T
                h0 = half * (MXU_DIM // HEAD_DIM) + pair * 2
                qt_ref[0, h0, :, r] = tt[:HEAD_DIM].astype(BF16)
                qt_ref[0, h0 + 1, :, r] = tt[HEAD_DIM:].astype(BF16)

        kn = rope(head_norm(p[:, ATTN_WIDTH:ATTN_WIDTH + KV_WIDTH], gk_ref[...]))
        k_ref[0, 0, r, :] = kn[:, :HEAD_DIM].astype(BF16)
        k_ref[0, 1, r, :] = pltpu.roll(kn, HEAD_DIM, 1)[:, :HEAD_DIM].astype(BF16)

        vvt = p[:, ATTN_WIDTH + KV_WIDTH:ATTN_WIDTH + 2 * KV_WIDTH].T
        for kvh in range(KV_HEADS):
            vt_ref[0, kvh, :, r] = vvt[kvh * HEAD_DIM:(kvh + 1) * HEAD_DIM].astype(BF16)

        z = p[:, ATTN_WIDTH + 2 * KV_WIDTH:]
        gc = math.sqrt(2.0 / math.pi)
        hz = 0.5 * z
        z = hz + hz * jnp.tanh(z * (gc + (gc * 0.044715) * (z * z)))
        u = z[:, :MLP_WIDTH]
        v2 = z[:, MLP_WIDTH:]
        mu = jnp.mean(v2, axis=-1, keepdims=True)
        vc = v2 - mu
        var = jnp.mean(vc * vc, axis=-1, keepdims=True)
        vln = vc * lax.rsqrt(var + EPS) * gsg_ref[...] + bsg_ref[...]
        lane_w = lax.broadcasted_iota(jnp.int32, vln.shape, 1)
        even_head = (lane_w % LANES) < HEAD_DIM
        v_even = jnp.where(even_head, vln, 0.0).astype(BF16)
        v_odd = jnp.where(even_head, 0.0, vln).astype(BF16)
        return u, v_even, v_odd

    def spatial_mix(r, u, v_even, v_odd):
        for ch in range(u.shape[0] // CHUNK):
            rows = slice(ch * CHUNK, (ch + 1) * CHUNK)
            out_rows = slice(r.start + ch * CHUNK, r.start + (ch + 1) * CHUNK)
            for jb in range(MLP_WIDTH // LANES):
                cols = slice(jb * LANES, (jb + 1) * LANES)
                rhs = jnp.concatenate([v_even[rows, cols], v_odd[rows, cols]], axis=0)
                s = jnp.dot(ws_ref[jb], rhs, preferred_element_type=F32)
                mlp_ref[0, out_rows, cols] = (u[rows, cols] * (s + bs_ref[:, cols])).astype(BF16)

    subs = [slice(i, i + ts) for i in range(0, tm, ts)]
    pending = [project(r) for r in subs[:PROJ_AHEAD]]
    for i, r in enumerate(subs):
        p_cur = pending.pop(0)
        if i + PROJ_AHEAD < len(subs):
            pending.append(project(subs[i + PROJ_AHEAD]))
        spatial_mix(r, *finish(r, p_cur))


def _inproj(x, mods, layer, stream_row, g_pre, w_in, gq_t, gk_t, cos_t, sin_t,
            g_sg, b_sg, ws_cat, bs_full, *, tm, use_rope):
    b, n, _ = x.shape
    if stream_row is None:
        mod_map = lambda i, j: (layer, i, 0, 0)
    else:
        mod_map = lambda i, j: (layer, stream_row, 0, 0)
    kern = functools.partial(_inproj_kernel, tm=tm, use_rope=use_rope)
    return pl.pallas_call(
        kern,
        grid=(b, n // tm),
        in_specs=[
            pl.BlockSpec((1, tm, D_MODEL), lambda i, j: (i, j, 0)),
            pl.BlockSpec((1, 1, MOD_SLOTS, D_MODEL), mod_map),
            _const_spec((1, D_MODEL)),
            _const_spec((D_MODEL, IN_WIDTH)),
            _const_spec((1, MXU_DIM)),
            _const_spec((1, LANES)),
            pl.BlockSpec((tm, LANES), lambda i, j: (j, 0)),
            pl.BlockSpec((tm, LANES), lambda i, j: (j, 0)),
            _const_spec((1, MLP_WIDTH)),
            _const_spec((1, MLP_WIDTH)),
            _const_spec((MLP_WIDTH // LANES, CHUNK, 2 * CHUNK)),
            _const_spec((CHUNK, MLP_WIDTH)),
        ],
        out_specs=[
            pl.BlockSpec((1, ATTN_HEADS, HEAD_DIM, tm), lambda i, j: (i, 0, 0, j)),
            pl.BlockSpec((1, KV_HEADS, tm, HEAD_DIM), lambda i, j: (i, 0, j, 0)),
            pl.BlockSpec((1, KV_HEADS, HEAD_DIM, tm), lambda i, j: (i, 0, 0, j)),
            pl.BlockSpec((1, tm, MLP_WIDTH), lambda i, j: (i, j, 0)),
        ],
        out_shape=[
            jax.ShapeDtypeStruct((b, ATTN_HEADS, HEAD_DIM, n), BF16),
            jax.ShapeDtypeStruct((b, KV_HEADS, n, HEAD_DIM), BF16),
            jax.ShapeDtypeStruct((b, KV_HEADS, HEAD_DIM, n), BF16),
            jax.ShapeDtypeStruct((b, n, MLP_WIDTH), BF16),
        ],
        compiler_params=pltpu.CompilerParams(
            dimension_semantics=("arbitrary", "arbitrary"), vmem_limit_bytes=VMEM_LIMIT),
        name="inproj",
    )(x, mods, g_pre, w_in, gq_t, gk_t, cos_t, sin_t, g_sg, b_sg, ws_cat, bs_full)


def _attn_kernel(*refs, tq, seg_lens, bk, online):
    n_seg = len(seg_lens)
    bound_ref, qt_ref = refs[0], refs[1]
    k_refs = refs[2:2 + n_seg]
    vt_refs = refs[2 + n_seg:2 + 2 * n_seg]
    o_ref = refs[2 + 2 * n_seg]

    ts = min(tq, SUB_ROWS)
    subs = [slice(i, i + ts) for i in range(0, tq, ts)]
    m = {(t, h): jnp.full((1, ts), -jnp.inf, F32) for t in range(len(subs)) for h in range(GQA_GROUP)}
    acc = {(t, h): jnp.zeros((HEAD_DIM, ts), F32) for t in range(len(subs)) for h in range(GQA_GROUP)}
    den = {(t, h): jnp.zeros((SUBLANES, ts), F32) for t in range(len(subs)) for h in range(GQA_GROUP)}

    def key_sum(pf):
        return jnp.sum(pf.reshape(pf.shape[0] // SUBLANES, SUBLANES, pf.shape[1]), axis=0)

    def finish(t, h, s, vt_ref, start, size, last):
        vt_blk = vt_ref[0, 0, :, start:start + size]
        if online:
            m_new = jnp.maximum(m[t, h], jnp.max(s, axis=0, keepdims=True))
            alpha = jnp.exp2(m[t, h] - m_new)
            pf = jnp.exp2(s - m_new)
            acc[t, h] = alpha * acc[t, h] + jnp.dot(vt_blk, pf.astype(BF16), preferred_element_type=F32)
            den[t, h] = alpha * den[t, h] + key_sum(pf)
            m[t, h] = m_new
        else:
            pf = jnp.exp2(s - bound_ref[0])
            acc[t, h] = acc[t, h] + jnp.dot(vt_blk, pf.astype(BF16), preferred_element_type=F32)
            den[t, h] = den[t, h] + key_sum(pf)
        if last:
            pairs = []
            for pair in range(GQA_GROUP // 2):
                ot = [acc[t, g] / jnp.sum(den[t, g], axis=0, keepdims=True) for g in (2 * pair, 2 * pair + 1)]
                pairs.append(jnp.concatenate(ot, axis=0).T)
            o_ref[0, subs[t], :] = jnp.concatenate(pairs, axis=1).astype(BF16)

    blocks = [(k_ref, vt_ref, start, min(bk, seg - start))
              for k_ref, vt_ref, seg in zip(k_refs, vt_refs, seg_lens) for start in range(0, seg, bk)]
    pending = []
    for t, r in enumerate(subs):
        for bi, (k_ref, vt_ref, start, size) in enumerate(blocks):
            for h in range(GQA_GROUP):
                k_blk = k_ref[0, 0, start:start + size, :]
                s = jnp.dot(k_blk, qt_ref[0, h, :, r], preferred_element_type=F32)
                last = bi == len(blocks) - 1 and h == GQA_GROUP - 1
                pending.append((t, h, s, vt_ref, start, size, last))
                if len(pending) > QK_AHEAD:
                    finish(*pending.pop(0))
    while pending:
        finish(*pending.pop(0))


def _attention(bound, qt, ks, vts, *, tq, bk, online):
    b, _, _, n = qt.shape
    seg_lens = tuple(k.shape[2] for k in ks)
    kern = functools.partial(_attn_kernel, tq=tq, seg_lens=seg_lens, bk=bk, online=online)
    in_specs = [pl.BlockSpec(memory_space=pltpu.SMEM),
                pl.BlockSpec((1, GQA_GROUP, HEAD_DIM, tq), lambda i, g, j: (i, g, 0, j))]
    in_specs += [pl.BlockSpec((1, 1, s, HEAD_DIM), lambda i, g, j: (i, g, 0, 0)) for s in seg_lens]
    in_specs += [pl.BlockSpec((1, 1, HEAD_DIM, s), lambda i, g, j: (i, g, 0, 0)) for s in seg_lens]
    return pl.pallas_call(
        kern,
        grid=(b, KV_HEADS, n // tq),
        in_specs=in_specs,
        out_specs=pl.BlockSpec((1, tq, GQA_GROUP * HEAD_DIM), lambda i, g, j: (i, j, g)),
        out_shape=jax.ShapeDtypeStruct((b, n, ATTN_WIDTH), BF16),
        compiler_params=pltpu.CompilerParams(
            dimension_semantics=("arbitrary", "arbitrary", "arbitrary"), vmem_limit_bytes=VMEM_LIMIT),
        name="attention",
    )(bound, qt, *ks, *vts)


def _outffn_kernel(attn_ref, mlp_ref, x_ref, mod_ref, gpm_ref, gpf_ref, gqf_ref,
                   wo_ref, wgu_ref, wd_ref, o_ref, *, tm):
    mod = mod_ref[0, 0]
    gate_mix, shift, scale, gate_ffn = mod[2:3], mod[3:4], mod[4:5], mod[5:6]

    ts = min(tm, SUB_ROWS)
    subs = [slice(i, i + ts) for i in range(0, tm, ts)]

    def out_proj(r):
        return (jnp.dot(attn_ref[0, r], wo_ref[:ATTN_WIDTH], preferred_element_type=F32)
                + jnp.dot(mlp_ref[0, r], wo_ref[ATTN_WIDTH:], preferred_element_type=F32))

    def ffn(h):
        def gate_up(ci):
            cols = slice(ci * FFN_CHUNK, (ci + 1) * FFN_CHUNK)
            up_cols = slice(FFN_HIDDEN + ci * FFN_CHUNK, FFN_HIDDEN + (ci + 1) * FFN_CHUNK)
            return (jnp.dot(h, wgu_ref[:, cols], preferred_element_type=F32),
                    jnp.dot(h, wgu_ref[:, up_cols], preferred_element_type=F32))

        pending = [gate_up(ci) for ci in range(FFN_AHEAD)]
        y = None
        for ci in range(N_FFN_CHUNKS):
            g, u = pending.pop(0)
            if ci + FFN_AHEAD < N_FFN_CHUNKS:
                pending.append(gate_up(ci + FFN_AHEAD))
            a = (g * jax.nn.sigmoid(g) * u).astype(BF16)
            d = jnp.dot(a, wd_ref[ci * FFN_CHUNK:(ci + 1) * FFN_CHUNK], preferred_element_type=F32)
            y = d if y is None else y + d
        return y

    out_next = out_proj(subs[0])
    for i, r in enumerate(subs):
        out = out_next
        if i + 1 < len(subs):
            out_next = out_proj(subs[i + 1])
        x1 = x_ref[0, r] + gate_mix * _rms(out, gpm_ref[...])
        h = (_rms(x1, gpf_ref[...]) * (1.0 + scale) + shift).astype(BF16)
        o_ref[0, r] = x1 + gate_ffn * _rms(ffn(h), gqf_ref[...])


def _outffn(attn, mlp, x, mods, layer, stream_row, g_post_mix, g_pre_ffn, g_post_ffn,
            w_out, w_gu, w_down, *, tm):
    b, n, _ = x.shape
    if stream_row is None:
        mod_map = lambda i, j: (layer, i, 0, 0)
    else:
        mod_map = lambda i, j: (layer, stream_row, 0, 0)
    row_spec = lambda w: pl.BlockSpec((1, tm, w), lambda i, j: (i, j, 0))
    return pl.pallas_call(
        functools.partial(_outffn_kernel, tm=tm),
        grid=(b, n // tm),
        in_specs=[
            row_spec(ATTN_WIDTH),
            row_spec(MLP_WIDTH),
            row_spec(D_MODEL),
            pl.BlockSpec((1, 1, MOD_SLOTS, D_MODEL), mod_map),
            _const_spec((1, D_MODEL)),
            _const_spec((1, D_MODEL)),
            _const_spec((1, D_MODEL)),
            _const_spec((ATTN_WIDTH + MLP_WIDTH, D_MODEL)),
            _const_spec((D_MODEL, 2 * FFN_HIDDEN)),
            _const_spec((FFN_HIDDEN, D_MODEL)),
        ],
        out_specs=row_spec(D_MODEL),
        out_shape=jax.ShapeDtypeStruct((b, n, D_MODEL), F32),
        compiler_params=pltpu.CompilerParams(
            dimension_semantics=("arbitrary", "arbitrary"), vmem_limit_bytes=VMEM_LIMIT),
        name="outffn",
    )(attn, mlp, x, mods, g_post_mix, g_pre_ffn, g_post_ffn, w_out, w_gu, w_down)


def _rope_tables(n):
    rows = n // GRID_W
    pos_row = jnp.broadcast_to(jnp.arange(rows, dtype=F32)[:, None], (rows, GRID_W)).reshape(-1)
    pos_col = jnp.broadcast_to(jnp.arange(GRID_W, dtype=F32)[None, :], (rows, GRID_W)).reshape(-1)
    inv = ROPE_THETA ** (-jnp.arange(0, ROPE_AXIS_DIM, 2, dtype=F32) / ROPE_AXIS_DIM)
    ang_r = pos_row[:, None] * inv
    ang_c = pos_col[:, None] * inv
    cos64 = jnp.concatenate([jnp.cos(ang_r)] * 2 + [jnp.cos(ang_c)] * 2, axis=-1)
    sin64 = jnp.concatenate([-jnp.sin(ang_r), jnp.sin(ang_r), -jnp.sin(ang_c), jnp.sin(ang_c)], axis=-1)
    return jnp.tile(cos64, (1, 2)), jnp.tile(sin64, (1, 2))


def kernel(x, c, ctx, c_ctx, w_mod, b_mod, g_pre_mix, g_post_mix, g_pre_ffn, g_post_ffn,
           w_in, g_q, g_k, g_sg, b_sg, w_s, b_s, w_out, w_ffn_in, w_ffn_out):
    b, n, _ = x.shape
    n_ctx = ctx.shape[1]
    assert n % ROW_TILE == 0 and n_ctx % SUB_ROWS == 0 and n_ctx <= ROW_TILE and b < MOD_ROWS
    cos_t, sin_t = _rope_tables(n)
    cos_c, sin_c = cos_t[:n_ctx], sin_t[:n_ctx]

    cvec = jnp.concatenate([c, c_ctx[None], jnp.zeros((MOD_ROWS - b - 1, D_MODEL), F32)], axis=0)
    mods = _modulation(cvec, w_mod, b_mod)
    mods = mods.reshape(DEPTH, MOD_ROWS, N_MOD, D_MODEL)
    mods = jnp.pad(mods, ((0, 0), (0, 0), (0, MOD_SLOTS - N_MOD), (0, 0)))

    w_in_b = w_in.astype(BF16)
    w_out_b = w_out.astype(BF16)
    w_gu = w_ffn_in.astype(BF16)
    w_down = w_ffn_out.astype(BF16)
    ws_cat = w_s.reshape(DEPTH, MLP_HEADS // 2, 2, CHUNK, CHUNK).transpose(0, 1, 3, 2, 4)
    ws_cat = ws_cat.reshape(DEPTH, MLP_HEADS // 2, CHUNK, 2 * CHUNK).astype(BF16)
    bs_full = jnp.repeat(b_s.transpose(0, 2, 1), HEAD_DIM, axis=2)
    gq_t = jnp.tile(g_q, (1, MXU_DIM // HEAD_DIM))[:, None]
    gk_t = jnp.tile(g_k, (1, LANES // HEAD_DIM))[:, None]

    xc = ctx
    for l in range(DEPTH):
        last = l == DEPTH - 1
        in_args = (g_pre_mix[l][None], w_in_b[l], gq_t[l], gk_t[l])
        mlp_args = (g_sg[l][None], b_sg[l][None], ws_cat[l], bs_full[l])
        ffn_args = (g_post_mix[l][None], g_pre_ffn[l][None], g_post_ffn[l][None],
                    w_out_b[l], w_gu[l], w_down[l])

        qct, kc, vct, mlp_c = _inproj(xc, mods, l, b, *in_args, cos_c, sin_c, *mlp_args,
                                      tm=n_ctx, use_rope=False)
        qxt, kx, vxt, mlp_x = _inproj(x, mods, l, None, *in_args, cos_t, sin_t, *mlp_args,
                                      tm=2 * ROW_TILE, use_rope=True)
        bound = (BOUND_SLACK * LOG2E * math.sqrt(HEAD_DIM)) * jnp.max(jnp.abs(g_q[l])) * jnp.max(jnp.abs(g_k[l]))
        bound = bound.reshape(1)
        attend = functools.partial(_attention, bound, tq=ROW_TILE, bk=KEY_BLOCK)
        attn_x = lax.cond(2.0 * bound[0] <= SAFE_EXP2_RANGE,
                          lambda: attend(qxt, (kc, kx), (vct, vxt), online=False),
                          lambda: attend(qxt, (kc, kx), (vct, vxt), online=True))
        x = _outffn(attn_x, mlp_x, x, mods, l, None, *ffn_args, tm=ROW_TILE)
        if not last:
            attn_c = _attention(bound, qct, (kc,), (vct,), tq=n_ctx, bk=KEY_BLOCK, online=True)
            xc = _outffn(attn_c, mlp_c, xc, mods, l, b, *ffn_args, tm=n_ctx)
    return x
```

```python
import functools
import math

import jax
import jax.numpy as jnp
from jax import lax
from jax.experimental import pallas as pl
from jax.experimental.pallas import tpu as pltpu

D_MODEL = 1024
DEPTH = 4
GRID_W = 64
HEAD_DIM = 64
ATTN_HEADS = 8
KV_HEADS = 2
GQA_GROUP = ATTN_HEADS // KV_HEADS
ATTN_WIDTH = ATTN_HEADS * HEAD_DIM
KV_WIDTH = KV_HEADS * HEAD_DIM
MLP_HEADS = 8
MLP_WIDTH = MLP_HEADS * HEAD_DIM
CHUNK = 128
IN_WIDTH = ATTN_WIDTH + 2 * KV_WIDTH + 2 * MLP_WIDTH
FFN_HIDDEN = 2816
N_MOD = 6
ROPE_THETA = 10000.0
ROPE_AXIS_DIM = HEAD_DIM // 2
EPS = 1e-6

LANES = 128
MXU_DIM = 256
MOD_ROWS = 16
MOD_SLOTS = 8
SUBLANES = 8
FFN_CHUNK = MXU_DIM
N_FFN_CHUNKS = FFN_HIDDEN // FFN_CHUNK
ROW_TILE = 4 * MXU_DIM
SUB_ROWS = MXU_DIM
KEY_BLOCK = MXU_DIM
PROJ_AHEAD = 2
FFN_AHEAD = 2
QK_AHEAD = 4
LOG2E = math.log2(math.e)
BOUND_SLACK = 1.02
SAFE_EXP2_RANGE = 120.0
VMEM_LIMIT = 56 * 1024 * 1024

F32 = jnp.float32
BF16 = jnp.bfloat16


def _const_spec(shape):
    zeros = (0,) * len(shape)
    return pl.BlockSpec(shape, lambda *_: zeros, pipeline_mode=pl.Buffered(1))


def _layer_spec(layer, shape):
    index = (layer,) + (0,) * len(shape)
    return pl.BlockSpec((1,) + tuple(shape), lambda *_: index, pipeline_mode=pl.Buffered(1))


def _rms(t, g):
    return t * lax.rsqrt(jnp.mean(t * t, axis=-1, keepdims=True) + EPS) * g


def _mod_kernel(c_ref, w_ref, b_ref, o_ref):
    cv = c_ref[...]
    act = (cv * jax.nn.sigmoid(cv)).astype(BF16)
    o_ref[0] = jnp.dot(act, w_ref[0].astype(BF16), preferred_element_type=F32) + b_ref[0]


def _modulation(cvec, w_mod, b_mod):
    tn = 1536
    n_out = N_MOD * D_MODEL
    return pl.pallas_call(
        _mod_kernel,
        grid=(DEPTH, n_out // tn),
        in_specs=[
            pl.BlockSpec((MOD_ROWS, D_MODEL), lambda l, j: (0, 0)),
            pl.BlockSpec((1, D_MODEL, tn), lambda l, j: (l, 0, j)),
            pl.BlockSpec((1, 1, tn), lambda l, j: (l, 0, j)),
        ],
        out_specs=pl.BlockSpec((1, MOD_ROWS, tn), lambda l, j: (l, 0, j)),
        out_shape=jax.ShapeDtypeStruct((DEPTH, MOD_ROWS, n_out), F32),
        compiler_params=pltpu.CompilerParams(
            dimension_semantics=("arbitrary", "arbitrary"), vmem_limit_bytes=VMEM_LIMIT),
        name="modulation",
    )(cvec, w_mod, b_mod.reshape(DEPTH, 1, n_out))


def _group_sum(t2, ones_bd):
    return jnp.dot(t2.astype(BF16), ones_bd, preferred_element_type=F32)


def _swap_halves16(t):
    lane = lax.broadcasted_iota(jnp.int32, t.shape, 1)
    return jnp.where((lane % 32) < 16, pltpu.roll(t, LANES - 16, 1), pltpu.roll(t, 16, 1))


def _inproj_kernel(x_ref, mod_ref, gpre_ref, w_ref, gq_ref, gk_ref, cos_ref, sin_ref,
                   gsg_ref, bsg_ref, ws_ref, bs_ref,
                   qt_ref, k_ref, vt_ref, mlp_ref, *, tm, use_rope):
    mod = mod_ref[0, 0]
    shift, scale = mod[0:1], mod[1:2]
    ts = min(tm, SUB_ROWS)

    ri = lax.broadcasted_iota(jnp.int32, (MXU_DIM, MXU_DIM), 0) // HEAD_DIM
    ci = lax.broadcasted_iota(jnp.int32, (MXU_DIM, MXU_DIM), 1) // HEAD_DIM
    ones_bd = jnp.where(ri == ci, 1.0, 0.0).astype(BF16)

    def project(r):
        h = (_rms(x_ref[0, r], gpre_ref[...]) * (1.0 + scale) + shift).astype(BF16)
        return jnp.dot(h, w_ref[0], preferred_element_type=F32)

    def head_norm(t, g):
        w = t.shape[1]
        ss = _group_sum(t * t, ones_bd[:w, :w])
        return t * lax.rsqrt(ss * (1.0 / HEAD_DIM) + EPS) * g

    def finish(r, p):
        def rope(t):
            if not use_rope:
                return t
            return t * cos_ref[r, :] + _swap_halves16(t) * sin_ref[r, :]

        q_scale = LOG2E / math.sqrt(HEAD_DIM)
        for half in range(ATTN_WIDTH // MXU_DIM):
            qn = head_norm(p[:, half * MXU_DIM:(half + 1) * MXU_DIM], gq_ref[...])
            for pair in range(MXU_DIM // LANES):
                tt = (rope(qn[:, pair * LANES:(pair + 1) * LANES]) * q_scale).T
                h0 = half * (MXU_DIM // HEAD_DIM) + pair * 2
                qt_ref[0, h0, :, r] = tt[:HEAD_DIM].astype(BF16)
                qt_ref[0, h0 + 1, :, r] = tt[HEAD_DIM:].astype(BF16)

        kn = rope(head_norm(p[:, ATTN_WIDTH:ATTN_WIDTH + KV_WIDTH], gk_ref[...]))
        k_ref[0, 0, r, :] = kn[:, :HEAD_DIM].astype(BF16)
        k_ref[0, 1, r, :] = pltpu.roll(kn, HEAD_DIM, 1)[:, :HEAD_DIM].astype(BF16)

        vvt = p[:, ATTN_WIDTH + KV_WIDTH:ATTN_WIDTH + 2 * KV_WIDTH].T
        for kvh in range(KV_HEADS):
            vt_ref[0, kvh, :, r] = vvt[kvh * HEAD_DIM:(kvh + 1) * HEAD_DIM].astype(BF16)

        z = p[:, ATTN_WIDTH + 2 * KV_WIDTH:]
        gc = math.sqrt(2.0 / math.pi)
        hz = 0.5 * z
        z = hz + hz * jnp.tanh(z * (gc + (gc * 0.044715) * (z * z)))
        u = z[:, :MLP_WIDTH]
        v2 = z[:, MLP_WIDTH:]
        mu = jnp.mean(v2, axis=-1, keepdims=True)
        vc = v2 - mu
        var = jnp.mean(vc * vc, axis=-1, keepdims=True)
        vln = vc * lax.rsqrt(var + EPS) * gsg_ref[...] + bsg_ref[...]
        lane_w = lax.broadcasted_iota(jnp.int32, vln.shape, 1)
        even_head = (lane_w % LANES) < HEAD_DIM
        v_even = jnp.where(even_head, vln, 0.0).astype(BF16)
        v_odd = jnp.where(even_head, 0.0, vln).astype(BF16)
        return u, v_even, v_odd

    def spatial_mix(r, u, v_even, v_odd):
        for ch in range(u.shape[0] // CHUNK):
            rows = slice(ch * CHUNK, (ch + 1) * CHUNK)
            out_rows = slice(r.start + ch * CHUNK, r.start + (ch + 1) * CHUNK)
            for jb in range(MLP_WIDTH // LANES):
                cols = slice(jb * LANES, (jb + 1) * LANES)
                rhs = jnp.concatenate([v_even[rows, cols], v_odd[rows, cols]], axis=0)
                s = jnp.dot(ws_ref[jb], rhs, preferred_element_type=F32)
                mlp_ref[0, out_rows, cols] = (u[rows, cols] * (s + bs_ref[:, cols])).astype(BF16)

    subs = [slice(i, i + ts) for i in range(0, tm, ts)]
    pending = [project(r) for r in subs[:PROJ_AHEAD]]
    for i, r in enumerate(subs):
        p_cur = pending.pop(0)
        if i + PROJ_AHEAD < len(subs):
            pending.append(project(subs[i + PROJ_AHEAD]))
        spatial_mix(r, *finish(r, p_cur))


def _inproj(x, mods, layer, stream_row, g_pre, w_in, gq_t, gk_t, cos_t, sin_t,
            g_sg, b_sg, ws_cat, bs_full, *, tm, use_rope):
    b, n, _ = x.shape
    if stream_row is None:
        mod_map = lambda i, j: (layer, i, 0, 0)
    else:
        mod_map = lambda i, j: (layer, stream_row, 0, 0)
    kern = functools.partial(_inproj_kernel, tm=tm, use_rope=use_rope)
    return pl.pallas_call(
        kern,
        grid=(b, n // tm),
        in_specs=[
            pl.BlockSpec((1, tm, D_MODEL), lambda i, j: (i, j, 0)),
            pl.BlockSpec((1, 1, MOD_SLOTS, D_MODEL), mod_map),
            _const_spec((1, D_MODEL)),
            _layer_spec(layer, (D_MODEL, IN_WIDTH)),
            _const_spec((1, MXU_DIM)),
            _const_spec((1, LANES)),
            pl.BlockSpec((tm, LANES), lambda i, j: (j, 0)),
            pl.BlockSpec((tm, LANES), lambda i, j: (j, 0)),
            _const_spec((1, MLP_WIDTH)),
            _const_spec((1, MLP_WIDTH)),
            _const_spec((MLP_WIDTH // LANES, CHUNK, 2 * CHUNK)),
            _const_spec((CHUNK, MLP_WIDTH)),
        ],
        out_specs=[
            pl.BlockSpec((1, ATTN_HEADS, HEAD_DIM, tm), lambda i, j: (i, 0, 0, j)),
            pl.BlockSpec((1, KV_HEADS, tm, HEAD_DIM), lambda i, j: (i, 0, j, 0)),
            pl.BlockSpec((1, KV_HEADS, HEAD_DIM, tm), lambda i, j: (i, 0, 0, j)),
            pl.BlockSpec((1, tm, MLP_WIDTH), lambda i, j: (i, j, 0)),
        ],
        out_shape=[
            jax.ShapeDtypeStruct((b, ATTN_HEADS, HEAD_DIM, n), BF16),
            jax.ShapeDtypeStruct((b, KV_HEADS, n, HEAD_DIM), BF16),
            jax.ShapeDtypeStruct((b, KV_HEADS, HEAD_DIM, n), BF16),
            jax.ShapeDtypeStruct((b, n, MLP_WIDTH), BF16),
        ],
        compiler_params=pltpu.CompilerParams(
            dimension_semantics=("arbitrary", "arbitrary"), vmem_limit_bytes=VMEM_LIMIT),
        name="inproj",
    )(x, mods, g_pre, w_in, gq_t, gk_t, cos_t, sin_t, g_sg, b_sg, ws_cat, bs_full)


def _attn_kernel(*refs, tq, seg_lens, bk, online):
    n_seg = len(seg_lens)
    bound_ref, qt_ref = refs[0], refs[1]
    k_refs = refs[2:2 + n_seg]
    vt_refs = refs[2 + n_seg:2 + 2 * n_seg]
    o_ref = refs[2 + 2 * n_seg]

    ts = min(tq, SUB_ROWS)
    subs = [slice(i, i + ts) for i in range(0, tq, ts)]
    m = {(t, h): jnp.full((1, ts), -jnp.inf, F32) for t in range(len(subs)) for h in range(GQA_GROUP)}
    acc = {(t, h): jnp.zeros((HEAD_DIM, ts), F32) for t in range(len(subs)) for h in range(GQA_GROUP)}
    den = {(t, h): jnp.zeros((SUBLANES, ts), F32) for t in range(len(subs)) for h in range(GQA_GROUP)}

    def key_sum(pf):
        return jnp.sum(pf.reshape(pf.shape[0] // SUBLANES, SUBLANES, pf.shape[1]), axis=0)

    def finish(t, h, s, vt_ref, start, size, last):
        vt_blk = vt_ref[0, 0, :, start:start + size]
        if online:
            m_new = jnp.maximum(m[t, h], jnp.max(s, axis=0, keepdims=True))
            alpha = jnp.exp2(m[t, h] - m_new)
            pf = jnp.exp2(s - m_new)
            acc[t, h] = alpha * acc[t, h] + jnp.dot(vt_blk, pf.astype(BF16), preferred_element_type=F32)
            den[t, h] = alpha * den[t, h] + key_sum(pf)
            m[t, h] = m_new
        else:
            pf = jnp.exp2(s - bound_ref[0])
            acc[t, h] = acc[t, h] + jnp.dot(vt_blk, pf.astype(BF16), preferred_element_type=F32)
            den[t, h] = den[t, h] + key_sum(pf)
        if last:
            pairs = []
            for pair in range(GQA_GROUP // 2):
                ot = [acc[t, g] / jnp.sum(den[t, g], axis=0, keepdims=True) for g in (2 * pair, 2 * pair + 1)]
                pairs.append(jnp.concatenate(ot, axis=0).T)
            o_ref[0, subs[t], :] = jnp.concatenate(pairs, axis=1).astype(BF16)

    blocks = [(k_ref, vt_ref, start, min(bk, seg - start))
              for k_ref, vt_ref, seg in zip(k_refs, vt_refs, seg_lens) for start in range(0, seg, bk)]
    pending = []
    for t, r in enumerate(subs):
        for bi, (k_ref, vt_ref, start, size) in enumerate(blocks):
            for h in range(GQA_GROUP):
                k_blk = k_ref[0, 0, start:start + size, :]
                s = jnp.dot(k_blk, qt_ref[0, h, :, r], preferred_element_type=F32)
                last = bi == len(blocks) - 1 and h == GQA_GROUP - 1
                pending.append((t, h, s, vt_ref, start, size, last))
                if len(pending) > QK_AHEAD:
                    finish(*pending.pop(0))
    while pending:
        finish(*pending.pop(0))


def _attention(bound, qt, ks, vts, *, tq, bk, online):
    b, _, _, n = qt.shape
    seg_lens = tuple(k.shape[2] for k in ks)
    kern = functools.partial(_attn_kernel, tq=tq, seg_lens=seg_lens, bk=bk, online=online)
    in_specs = [pl.BlockSpec(memory_space=pltpu.SMEM),
                pl.BlockSpec((1, GQA_GROUP, HEAD_DIM, tq), lambda i, g, j: (i, g, 0, j))]
    in_specs += [pl.BlockSpec((1, 1, s, HEAD_DIM), lambda i, g, j: (i, g, 0, 0)) for s in seg_lens]
    in_specs += [pl.BlockSpec((1, 1, HEAD_DIM, s), lambda i, g, j: (i, g, 0, 0)) for s in seg_lens]
    return pl.pallas_call(
        kern,
        grid=(b, KV_HEADS, n // tq),
        in_specs=in_specs,
        out_specs=pl.BlockSpec((1, tq, GQA_GROUP * HEAD_DIM), lambda i, g, j: (i, j, g)),
        out_shape=jax.ShapeDtypeStruct((b, n, ATTN_WIDTH), BF16),
        compiler_params=pltpu.CompilerParams(
            dimension_semantics=("arbitrary", "arbitrary", "arbitrary"), vmem_limit_bytes=VMEM_LIMIT),
        name="attention",
    )(bound, qt, *ks, *vts)


def _outffn_kernel(attn_ref, mlp_ref, x_ref, mod_ref, gpm_ref, gpf_ref, gqf_ref,
                   wo_ref, wgu_ref, wd_ref, o_ref, *, tm):
    mod = mod_ref[0, 0]
    gate_mix, shift, scale, gate_ffn = mod[2:3], mod[3:4], mod[4:5], mod[5:6]

    ts = min(tm, SUB_ROWS)
    subs = [slice(i, i + ts) for i in range(0, tm, ts)]

    def out_proj(r):
        return (jnp.dot(attn_ref[0, r], wo_ref[0, :ATTN_WIDTH], preferred_element_type=F32)
                + jnp.dot(mlp_ref[0, r], wo_ref[0, ATTN_WIDTH:], preferred_element_type=F32))

    def ffn(h):
        def gate_up(ci):
            cols = slice(ci * FFN_CHUNK, (ci + 1) * FFN_CHUNK)
            up_cols = slice(FFN_HIDDEN + ci * FFN_CHUNK, FFN_HIDDEN + (ci + 1) * FFN_CHUNK)
            return (jnp.dot(h, wgu_ref[0, :, cols], preferred_element_type=F32),
                    jnp.dot(h, wgu_ref[0, :, up_cols], preferred_element_type=F32))

        pending = [gate_up(ci) for ci in range(FFN_AHEAD)]
        y = None
        for ci in range(N_FFN_CHUNKS):
            g, u = pending.pop(0)
            if ci + FFN_AHEAD < N_FFN_CHUNKS:
                pending.append(gate_up(ci + FFN_AHEAD))
            a = (g * jax.nn.sigmoid(g) * u).astype(BF16)
            d = jnp.dot(a, wd_ref[0, ci * FFN_CHUNK:(ci + 1) * FFN_CHUNK], preferred_element_type=F32)
            y = d if y is None else y + d
        return y

    out_next = out_proj(subs[0])
    for i, r in enumerate(subs):
        out = out_next
        if i + 1 < len(subs):
            out_next = out_proj(subs[i + 1])
        x1 = x_ref[0, r] + gate_mix * _rms(out, gpm_ref[...])
        h = (_rms(x1, gpf_ref[...]) * (1.0 + scale) + shift).astype(BF16)
        o_ref[0, r] = x1 + gate_ffn * _rms(ffn(h), gqf_ref[...])


def _outffn(attn, mlp, x, mods, layer, stream_row, g_post_mix, g_pre_ffn, g_post_ffn,
            w_out, w_gu, w_down, *, tm):
    b, n, _ = x.shape
    if stream_row is None:
        mod_map = lambda i, j: (layer, i, 0, 0)
    else:
        mod_map = lambda i, j: (layer, stream_row, 0, 0)
    row_spec = lambda w: pl.BlockSpec((1, tm, w), lambda i, j: (i, j, 0))
    return pl.pallas_call(
        functools.partial(_outffn_kernel, tm=tm),
        grid=(b, n // tm),
        in_specs=[
            row_spec(ATTN_WIDTH),
            row_spec(MLP_WIDTH),
            row_spec(D_MODEL),
            pl.BlockSpec((1, 1, MOD_SLOTS, D_MODEL), mod_map),
            _const_spec((1, D_MODEL)),
            _const_spec((1, D_MODEL)),
            _const_spec((1, D_MODEL)),
            _layer_spec(layer, (ATTN_WIDTH + MLP_WIDTH, D_MODEL)),
            _layer_spec(layer, (D_MODEL, 2 * FFN_HIDDEN)),
            _layer_spec(layer, (FFN_HIDDEN, D_MODEL)),
        ],
        out_specs=row_spec(D_MODEL),
        out_shape=jax.ShapeDtypeStruct((b, n, D_MODEL), F32),
        compiler_params=pltpu.CompilerParams(
            dimension_semantics=("arbitrary", "arbitrary"), vmem_limit_bytes=VMEM_LIMIT),
        name="outffn",
    )(attn, mlp, x, mods, g_post_mix, g_pre_ffn, g_post_ffn, w_out, w_gu, w_down)


def _rope_tables(n):
    rows = n // GRID_W
    pos_row = jnp.broadcast_to(jnp.arange(rows, dtype=F32)[:, None], (rows, GRID_W)).reshape(-1)
    pos_col = jnp.broadcast_to(jnp.arange(GRID_W, dtype=F32)[None, :], (rows, GRID_W)).reshape(-1)
    inv = ROPE_THETA ** (-jnp.arange(0, ROPE_AXIS_DIM, 2, dtype=F32) / ROPE_AXIS_DIM)
    ang_r = pos_row[:, None] * inv
    ang_c = pos_col[:, None] * inv
    cos64 = jnp.concatenate([jnp.cos(ang_r)] * 2 + [jnp.cos(ang_c)] * 2, axis=-1)
    sin64 = jnp.concatenate([-jnp.sin(ang_r), jnp.sin(ang_r), -jnp.sin(ang_c), jnp.sin(ang_c)], axis=-1)
    return jnp.tile(cos64, (1, 2)), jnp.tile(sin64, (1, 2))


def kernel(x, c, ctx, c_ctx, w_mod, b_mod, g_pre_mix, g_post_mix, g_pre_ffn, g_post_ffn,
           w_in, g_q, g_k, g_sg, b_sg, w_s, b_s, w_out, w_ffn_in, w_ffn_out):
    b, n, _ = x.shape
    n_ctx = ctx.shape[1]
    assert n % ROW_TILE == 0 and n_ctx % SUB_ROWS == 0 and n_ctx <= ROW_TILE and b < MOD_ROWS
    cos_t, sin_t = _rope_tables(n)
    cos_c, sin_c = cos_t[:n_ctx], sin_t[:n_ctx]

    cvec = jnp.concatenate([c, c_ctx[None], jnp.zeros((MOD_ROWS - b - 1, D_MODEL), F32)], axis=0)
    mods = _modulation(cvec, w_mod, b_mod)
    mods = mods.reshape(DEPTH, MOD_ROWS, N_MOD, D_MODEL)
    mods = jnp.pad(mods, ((0, 0), (0, 0), (0, MOD_SLOTS - N_MOD), (0, 0)))

    w_in_b = w_in.astype(BF16)
    w_out_b = w_out.astype(BF16)
    w_gu = w_ffn_in.astype(BF16)
    w_down = w_ffn_out.astype(BF16)
    ws_cat = w_s.reshape(DEPTH, MLP_HEADS // 2, 2, CHUNK, CHUNK).transpose(0, 1, 3, 2, 4)
    ws_cat = ws_cat.reshape(DEPTH, MLP_HEADS // 2, CHUNK, 2 * CHUNK).astype(BF16)
    bs_full = jnp.repeat(b_s.transpose(0, 2, 1), HEAD_DIM, axis=2)
    gq_t = jnp.tile(g_q, (1, MXU_DIM // HEAD_DIM))[:, None]
    gk_t = jnp.tile(g_k, (1, LANES // HEAD_DIM))[:, None]

    xc = ctx
    for l in range(DEPTH):
        last = l == DEPTH - 1
        in_args = (g_pre_mix[l][None], w_in_b, gq_t[l], gk_t[l])
        mlp_args = (g_sg[l][None], b_sg[l][None], ws_cat[l], bs_full[l])
        ffn_args = (g_post_mix[l][None], g_pre_ffn[l][None], g_post_ffn[l][None], w_out_b, w_gu, w_down)

        qct, kc, vct, mlp_c = _inproj(xc, mods, l, b, *in_args, cos_c, sin_c, *mlp_args,
                                      tm=n_ctx, use_rope=False)
        qxt, kx, vxt, mlp_x = _inproj(x, mods, l, None, *in_args, cos_t, sin_t, *mlp_args,
                                      tm=2 * ROW_TILE, use_rope=True)
        bound = (BOUND_SLACK * LOG2E * math.sqrt(HEAD_DIM)) * jnp.max(jnp.abs(g_q[l])) * jnp.max(jnp.abs(g_k[l]))
        bound = bound.reshape(1)
        attend = functools.partial(_attention, bound, tq=ROW_TILE, bk=KEY_BLOCK)
        attn_x = lax.cond(2.0 * bound[0] <= SAFE_EXP2_RANGE,
                          lambda: attend(qxt, (kc, kx), (vct, vxt), online=False),
                          lambda: attend(qxt, (kc, kx), (vct, vxt), online=True))
        x = _outffn(attn_x, mlp_x, x, mods, l, None, *ffn_args, tm=ROW_TILE)
        if not last:
            attn_c = _attention(bound, qct, (kc,), (vct,), tq=n_ctx, bk=KEY_BLOCK, online=True)
            xc = _outffn(attn_c, mlp_c, xc, mods, l, b, *ffn_args, tm=n_ctx)
    return x
```

```python
import functools
import math

import jax
import jax.numpy as jnp
from jax import lax
from jax.experimental import pallas as pl
from jax.experimental.pallas import tpu as pltpu

D_MODEL = 1024
DEPTH = 4
GRID_W = 64
HEAD_DIM = 64
ATTN_HEADS = 8
KV_HEADS = 2
GQA_GROUP = ATTN_HEADS // KV_HEADS
ATTN_WIDTH = ATTN_HEADS * HEAD_DIM
KV_WIDTH = KV_HEADS * HEAD_DIM
MLP_HEADS = 8
MLP_WIDTH = MLP_HEADS * HEAD_DIM
CHUNK = 128
IN_WIDTH = ATTN_WIDTH + 2 * KV_WIDTH + 2 * MLP_WIDTH
FFN_HIDDEN = 2816
N_MOD = 6
ROPE_THETA = 10000.0
ROPE_AXIS_DIM = HEAD_DIM // 2
EPS = 1e-6

LANES = 128
MXU_DIM = 256
MOD_ROWS = 16
MOD_SLOTS = 8
SUBLANES = 8
FFN_CHUNK = MXU_DIM
N_FFN_CHUNKS = FFN_HIDDEN // FFN_CHUNK
ROW_TILE = 4 * MXU_DIM
SUB_ROWS = MXU_DIM
KEY_BLOCK = MXU_DIM
PROJ_AHEAD = 2
FFN_AHEAD = 2
QK_AHEAD = 4
LOG2E = math.log2(math.e)
BOUND_SLACK = 1.02
SAFE_EXP2_RANGE = 120.0
VMEM_LIMIT = 56 * 1024 * 1024

F32 = jnp.float32
BF16 = jnp.bfloat16


def _const_spec(shape):
    zeros = (0,) * len(shape)
    return pl.BlockSpec(shape, lambda *_: zeros, pipeline_mode=pl.Buffered(1))


def _layer_spec(layer, shape):
    index = (layer,) + (0,) * len(shape)
    return pl.BlockSpec((1,) + tuple(shape), lambda *_: index, pipeline_mode=pl.Buffered(1))


def _rms(t, g):
    return t * lax.rsqrt(jnp.mean(t * t, axis=-1, keepdims=True) + EPS) * g


def _mod_kernel(c_ref, w_ref, b_ref, o_ref):
    cv = c_ref[...]
    act = (cv * jax.nn.sigmoid(cv)).astype(BF16)
    o_ref[0] = jnp.dot(act, w_ref[0].astype(BF16), preferred_element_type=F32) + b_ref[0]


def _modulation(cvec, w_mod, b_mod):
    tn = 1536
    n_out = N_MOD * D_MODEL
    return pl.pallas_call(
        _mod_kernel,
        grid=(DEPTH, n_out // tn),
        in_specs=[
            pl.BlockSpec((MOD_ROWS, D_MODEL), lambda l, j: (0, 0)),
            pl.BlockSpec((1, D_MODEL, tn), lambda l, j: (l, 0, j)),
            pl.BlockSpec((1, 1, tn), lambda l, j: (l, 0, j)),
        ],
        out_specs=pl.BlockSpec((1, MOD_ROWS, tn), lambda l, j: (l, 0, j)),
        out_shape=jax.ShapeDtypeStruct((DEPTH, MOD_ROWS, n_out), F32),
        compiler_params=pltpu.CompilerParams(
            dimension_semantics=("arbitrary", "arbitrary"), vmem_limit_bytes=VMEM_LIMIT),
        name="modulation",
    )(cvec, w_mod, b_mod.reshape(DEPTH, 1, n_out))


def _group_sum(t2, ones_bd):
    return jnp.dot(t2.astype(BF16), ones_bd, preferred_element_type=F32)


def _swap_halves16(t):
    lane = lax.broadcasted_iota(jnp.int32, t.shape, 1)
    return jnp.where((lane % 32) < 16, pltpu.roll(t, LANES - 16, 1), pltpu.roll(t, 16, 1))


def _inproj_kernel(x_ref, mod_ref, gpre_ref, w_ref, gq_ref, gk_ref, cos_ref, sin_ref,
                   gsg_ref, bsg_ref, ws_ref, bs_ref,
                   qt_ref, k_ref, vt_ref, mlp_ref, *, tm, use_rope):
    mod = mod_ref[0, 0]
    shift, scale = mod[0:1], mod[1:2]
    ts = min(tm, SUB_ROWS)

    ri = lax.broadcasted_iota(jnp.int32, (MXU_DIM, MXU_DIM), 0) // HEAD_DIM
    ci = lax.broadcasted_iota(jnp.int32, (MXU_DIM, MXU_DIM), 1) // HEAD_DIM
    ones_bd = jnp.where(ri == ci, 1.0, 0.0).astype(BF16)

    def project(r):
        h = (_rms(x_ref[0, r], gpre_ref[...]) * (1.0 + scale) + shift).astype(BF16)
        return jnp.dot(h, w_ref[0], preferred_element_type=F32)

    def head_norm(t, g):
        w = t.shape[1]
        ss = _group_sum(t * t, ones_bd[:w, :w])
        return t * lax.rsqrt(ss * (1.0 / HEAD_DIM) + EPS) * g

    def finish(r, p):
        def rope(t):
            if not use_rope:
                return t
            return t * cos_ref[r, :] + _swap_halves16(t) * sin_ref[r, :]

        q_scale = LOG2E / math.sqrt(HEAD_DIM)
        for half in range(ATTN_WIDTH // MXU_DIM):
            qn = head_norm(p[:, half * MXU_DIM:(half + 1) * MXU_DIM], gq_ref[...])
            for pair in range(MXU_DIM // LANES):
                tt = (rope(qn[:, pair * LANES:(pair + 1) * LANES]) * q_scale).T
                h0 = half * (MXU_DIM // HEAD_DIM) + pair * 2
                qt_ref[0, h0, :, r] = tt[:HEAD_DIM].astype(BF16)
                qt_ref[0, h0 + 1, :, r] = tt[HEAD_DIM:].astype(BF16)

        kn = rope(head_norm(p[:, ATTN_WIDTH:ATTN_WIDTH + KV_WIDTH], gk_ref[...]))
        k_ref[0, 0, r, :] = kn[:, :HEAD_DIM].astype(BF16)
        k_ref[0, 1, r, :] = pltpu.roll(kn, HEAD_DIM, 1)[:, :HEAD_DIM].astype(BF16)

        vvt = p[:, ATTN_WIDTH + KV_WIDTH:ATTN_WIDTH + 2 * KV_WIDTH].T
        for kvh in range(KV_HEADS):
            vt_ref[0, kvh, :, r] = vvt[kvh * HEAD_DIM:(kvh + 1) * HEAD_DIM].astype(BF16)

        z = p[:, ATTN_WIDTH + 2 * KV_WIDTH:]
        gc = math.sqrt(2.0 / math.pi)
        hz = 0.5 * z
        z = hz + hz * jnp.tanh(z * (gc + (gc * 0.044715) * (z * z)))
        u = z[:, :MLP_WIDTH]
        v2 = z[:, MLP_WIDTH:]
        mu = jnp.mean(v2, axis=-1, keepdims=True)
        vc = v2 - mu
        var = jnp.mean(vc * vc, axis=-1, keepdims=True)
        vln = vc * lax.rsqrt(var + EPS) * gsg_ref[...] + bsg_ref[...]
        lane_w = lax.broadcasted_iota(jnp.int32, vln.shape, 1)
        even_head = (lane_w % LANES) < HEAD_DIM
        v_even = jnp.where(even_head, vln, 0.0).astype(BF16)
        v_odd = jnp.where(even_head, 0.0, vln).astype(BF16)
        return u, v_even, v_odd

    def spatial_mix(r, u, v_even, v_odd):
        for ch in range(u.shape[0] // CHUNK):
            rows = slice(ch * CHUNK, (ch + 1) * CHUNK)
            out_rows = slice(r.start + ch * CHUNK, r.start + (ch + 1) * CHUNK)
            for jb in range(MLP_WIDTH // LANES):
                cols = slice(jb * LANES, (jb + 1) * LANES)
                rhs = jnp.concatenate([v_even[rows, cols], v_odd[rows, cols]], axis=0)
                s = jnp.dot(ws_ref[jb], rhs, preferred_element_type=F32)
                mlp_ref[0, out_rows, cols] = (u[rows, cols] * (s + bs_ref[:, cols])).astype(BF16)

    subs = [slice(i, i + ts) for i in range(0, tm, ts)]
    pending = [project(r) for r in subs[:PROJ_AHEAD]]
    for i, r in enumerate(subs):
        p_cur = pending.pop(0)
        if i + PROJ_AHEAD < len(subs):
            pending.append(project(subs[i + PROJ_AHEAD]))
        spatial_mix(r, *finish(r, p_cur))


def _inproj(x, mods, layer, stream_row, g_pre, w_in, gq_t, gk_t, cos_t, sin_t,
            g_sg, b_sg, ws_cat, bs_full, *, tm, use_rope):
    b, n, _ = x.shape
    if stream_row is None:
        mod_map = lambda i, j: (layer, i, 0, 0)
    else:
        mod_map = lambda i, j: (layer, stream_row, 0, 0)
    kern = functools.partial(_inproj_kernel, tm=tm, use_rope=use_rope)
    return pl.pallas_call(
        kern,
        grid=(b, n // tm),
        in_specs=[
            pl.BlockSpec((1, tm, D_MODEL), lambda i, j: (i, j, 0)),
            pl.BlockSpec((1, 1, MOD_SLOTS, D_MODEL), mod_map),
            _const_spec((1, D_MODEL)),
            _layer_spec(layer, (D_MODEL, IN_WIDTH)),
            _const_spec((1, MXU_DIM)),
            _const_spec((1, LANES)),
            pl.BlockSpec((tm, LANES), lambda i, j: (j, 0)),
            pl.BlockSpec((tm, LANES), lambda i, j: (j, 0)),
            _const_spec((1, MLP_WIDTH)),
            _const_spec((1, MLP_WIDTH)),
            _const_spec((MLP_WIDTH // LANES, CHUNK, 2 * CHUNK)),
            _const_spec((CHUNK, MLP_WIDTH)),
        ],
        out_specs=[
            pl.BlockSpec((1, ATTN_HEADS, HEAD_DIM, tm), lambda i, j: (i, 0, 0, j)),
            pl.BlockSpec((1, KV_HEADS, tm, HEAD_DIM), lambda i, j: (i, 0, j, 0)),
            pl.BlockSpec((1, KV_HEADS, HEAD_DIM, tm), lambda i, j: (i, 0, 0, j)),
            pl.BlockSpec((1, tm, MLP_WIDTH), lambda i, j: (i, j, 0)),
        ],
        out_shape=[
            jax.ShapeDtypeStruct((b, ATTN_HEADS, HEAD_DIM, n), BF16),
            jax.ShapeDtypeStruct((b, KV_HEADS, n, HEAD_DIM), BF16),
            jax.ShapeDtypeStruct((b, KV_HEADS, HEAD_DIM, n), BF16),
            jax.ShapeDtypeStruct((b, n, MLP_WIDTH), BF16),
        ],
        compiler_params=pltpu.CompilerParams(
            dimension_semantics=("arbitrary", "arbitrary"), vmem_limit_bytes=VMEM_LIMIT),
        name="inproj",
    )(x, mods, g_pre, w_in, gq_t, gk_t, cos_t, sin_t, g_sg, b_sg, ws_cat, bs_full)


def _attn_kernel(*refs, tq, seg_lens, bk, fast_path):
    n_seg = len(seg_lens)
    bound_ref, qt_ref = refs[0], refs[1]
    segments = list(zip(refs[2:2 + n_seg], refs[2 + n_seg:2 + 2 * n_seg], seg_lens))
    o_ref = refs[2 + 2 * n_seg]
    ts = min(tq, SUB_ROWS)

    def key_sum(pf):
        return jnp.sum(pf.reshape(pf.shape[0] // SUBLANES, SUBLANES, pf.shape[1]), axis=0)

    def normalised(acc, den):
        pairs = []
        for pair in range(GQA_GROUP // 2):
            ot = [acc[g] / jnp.sum(den[g], axis=0, keepdims=True) for g in (2 * pair, 2 * pair + 1)]
            pairs.append(jnp.concatenate(ot, axis=0).T)
        return jnp.concatenate(pairs, axis=1).astype(BF16)

    def fixed_shift_path():
        subs = [slice(i, i + ts) for i in range(0, tq, ts)]
        acc = {(t, h): jnp.zeros((HEAD_DIM, ts), F32) for t in range(len(subs)) for h in range(GQA_GROUP)}
        den = {(t, h): jnp.zeros((SUBLANES, ts), F32) for t in range(len(subs)) for h in range(GQA_GROUP)}

        def finish(t, h, s, vt_ref, start, size, last):
            pf = jnp.exp2(s - bound_ref[0])
            vt_blk = vt_ref[0, 0, :, start:start + size]
            acc[t, h] = acc[t, h] + jnp.dot(vt_blk, pf.astype(BF16), preferred_element_type=F32)
            den[t, h] = den[t, h] + key_sum(pf)
            if last:
                o_ref[0, subs[t], :] = normalised([acc[t, g] for g in range(GQA_GROUP)],
                                                  [den[t, g] for g in range(GQA_GROUP)])

        blocks = [(k_ref, vt_ref, start, min(bk, seg - start))
                  for k_ref, vt_ref, seg in segments for start in range(0, seg, bk)]
        pending = []
        for t, r in enumerate(subs):
            for bi, (k_ref, vt_ref, start, size) in enumerate(blocks):
                for h in range(GQA_GROUP):
                    k_blk = k_ref[0, 0, start:start + size, :]
                    s = jnp.dot(k_blk, qt_ref[0, h, :, r], preferred_element_type=F32)
                    last = bi == len(blocks) - 1 and h == GQA_GROUP - 1
                    pending.append((t, h, s, vt_ref, start, size, last))
                    if len(pending) > QK_AHEAD:
                        finish(*pending.pop(0))
        while pending:
            finish(*pending.pop(0))

    def online_max_path():
        def sub_tile(t, carry):
            q0 = pl.multiple_of(t * ts, ts)
            qs = [qt_ref[0, h, :, pl.ds(q0, ts)] for h in range(GQA_GROUP)]
            state = tuple((jnp.full((1, ts), -jnp.inf, F32), jnp.zeros((SUBLANES, ts), F32),
                           jnp.zeros((HEAD_DIM, ts), F32)) for _ in range(GQA_GROUP))
            for k_ref, vt_ref, seg in segments:
                def key_block(j, st, k_ref=k_ref, vt_ref=vt_ref):
                    k0 = pl.multiple_of(j * bk, bk)
                    k_blk = k_ref[0, 0, pl.ds(k0, bk), :]
                    vt_blk = vt_ref[0, 0, :, pl.ds(k0, bk)]
                    new = []
                    for h in range(GQA_GROUP):
                        m, den, acc = st[h]
                        s = jnp.dot(k_blk, qs[h], preferred_element_type=F32)
                        m_new = jnp.maximum(m, jnp.max(s, axis=0, keepdims=True))
                        alpha = jnp.exp2(m - m_new)
                        pf = jnp.exp2(s - m_new)
                        new.append((m_new, alpha * den + key_sum(pf),
                                    alpha * acc + jnp.dot(vt_blk, pf.astype(BF16), preferred_element_type=F32)))
                    return tuple(new)
                state = lax.fori_loop(0, seg // bk, key_block, state)
            o_ref[0, pl.ds(q0, ts), :] = normalised([st[2] for st in state], [st[1] for st in state])
            return carry
        lax.fori_loop(0, tq // ts, sub_tile, 0)

    if fast_path:
        safe = 2.0 * bound_ref[0] <= SAFE_EXP2_RANGE
        pl.when(safe)(fixed_shift_path)
        pl.when(jnp.logical_not(safe))(online_max_path)
    else:
        online_max_path()


def _attention(bound, qt, ks, vts, *, tq, bk, fast_path):
    b, _, _, n = qt.shape
    seg_lens = tuple(k.shape[2] for k in ks)
    assert all(s % bk == 0 for s in seg_lens)
    kern = functools.partial(_attn_kernel, tq=tq, seg_lens=seg_lens, bk=bk, fast_path=fast_path)
    in_specs = [pl.BlockSpec(memory_space=pltpu.SMEM),
                pl.BlockSpec((1, GQA_GROUP, HEAD_DIM, tq), lambda i, g, j: (i, g, 0, j))]
    in_specs += [pl.BlockSpec((1, 1, s, HEAD_DIM), lambda i, g, j: (i, g, 0, 0)) for s in seg_lens]
    in_specs += [pl.BlockSpec((1, 1, HEAD_DIM, s), lambda i, g, j: (i, g, 0, 0)) for s in seg_lens]
    return pl.pallas_call(
        kern,
        grid=(b, KV_HEADS, n // tq),
        in_specs=in_specs,
        out_specs=pl.BlockSpec((1, tq, GQA_GROUP * HEAD_DIM), lambda i, g, j: (i, j, g)),
        out_shape=jax.ShapeDtypeStruct((b, n, ATTN_WIDTH), BF16),
        compiler_params=pltpu.CompilerParams(
            dimension_semantics=("arbitrary", "arbitrary", "arbitrary"), vmem_limit_bytes=VMEM_LIMIT),
        name="attention",
    )(bound, qt, *ks, *vts)


def _outffn_kernel(attn_ref, mlp_ref, x_ref, mod_ref, gpm_ref, gpf_ref, gqf_ref,
                   wo_ref, wgu_ref, wd_ref, o_ref, *, tm):
    mod = mod_ref[0, 0]
    gate_mix, shift, scale, gate_ffn = mod[2:3], mod[3:4], mod[4:5], mod[5:6]

    ts = min(tm, SUB_ROWS)
    subs = [slice(i, i + ts) for i in range(0, tm, ts)]

    def out_proj(r):
        return (jnp.dot(attn_ref[0, r], wo_ref[0, :ATTN_WIDTH], preferred_element_type=F32)
                + jnp.dot(mlp_ref[0, r], wo_ref[0, ATTN_WIDTH:], preferred_element_type=F32))

    def ffn(h):
        def gate_up(ci):
            cols = slice(ci * FFN_CHUNK, (ci + 1) * FFN_CHUNK)
            up_cols = slice(FFN_HIDDEN + ci * FFN_CHUNK, FFN_HIDDEN + (ci + 1) * FFN_CHUNK)
            return (jnp.dot(h, wgu_ref[0, :, cols], preferred_element_type=F32),
                    jnp.dot(h, wgu_ref[0, :, up_cols], preferred_element_type=F32))

        pending = [gate_up(ci) for ci in range(FFN_AHEAD)]
        y = None
        for ci in range(N_FFN_CHUNKS):
            g, u = pending.pop(0)
            if ci + FFN_AHEAD < N_FFN_CHUNKS:
                pending.append(gate_up(ci + FFN_AHEAD))
            a = (g * jax.nn.sigmoid(g) * u).astype(BF16)
            d = jnp.dot(a, wd_ref[0, ci * FFN_CHUNK:(ci + 1) * FFN_CHUNK], preferred_element_type=F32)
            y = d if y is None else y + d
        return y

    out_next = out_proj(subs[0])
    for i, r in enumerate(subs):
        out = out_next
        if i + 1 < len(subs):
            out_next = out_proj(subs[i + 1])
        x1 = x_ref[0, r] + gate_mix * _rms(out, gpm_ref[...])
        h = (_rms(x1, gpf_ref[...]) * (1.0 + scale) + shift).astype(BF16)
        o_ref[0, r] = x1 + gate_ffn * _rms(ffn(h), gqf_ref[...])


def _outffn(attn, mlp, x, mods, layer, stream_row, g_post_mix, g_pre_ffn, g_post_ffn,
            w_out, w_gu, w_down, *, tm):
    b, n, _ = x.shape
    if stream_row is None:
        mod_map = lambda i, j: (layer, i, 0, 0)
    else:
        mod_map = lambda i, j: (layer, stream_row, 0, 0)
    row_spec = lambda w: pl.BlockSpec((1, tm, w), lambda i, j: (i, j, 0))
    return pl.pallas_call(
        functools.partial(_outffn_kernel, tm=tm),
        grid=(b, n // tm),
        in_specs=[
            row_spec(ATTN_WIDTH),
            row_spec(MLP_WIDTH),
            row_spec(D_MODEL),
            pl.BlockSpec((1, 1, MOD_SLOTS, D_MODEL), mod_map),
            _const_spec((1, D_MODEL)),
            _const_spec((1, D_MODEL)),
            _const_spec((1, D_MODEL)),
            _layer_spec(layer, (ATTN_WIDTH + MLP_WIDTH, D_MODEL)),
            _layer_spec(layer, (D_MODEL, 2 * FFN_HIDDEN)),
            _layer_spec(layer, (FFN_HIDDEN, D_MODEL)),
        ],
        out_specs=row_spec(D_MODEL),
        out_shape=jax.ShapeDtypeStruct((b, n, D_MODEL), F32),
        compiler_params=pltpu.CompilerParams(
            dimension_semantics=("arbitrary", "arbitrary"), vmem_limit_bytes=VMEM_LIMIT),
        name="outffn",
    )(attn, mlp, x, mods, g_post_mix, g_pre_ffn, g_post_ffn, w_out, w_gu, w_down)


def _rope_tables(n):
    rows = n // GRID_W
    pos_row = jnp.broadcast_to(jnp.arange(rows, dtype=F32)[:, None], (rows, GRID_W)).reshape(-1)
    pos_col = jnp.broadcast_to(jnp.arange(GRID_W, dtype=F32)[None, :], (rows, GRID_W)).reshape(-1)
    inv = ROPE_THETA ** (-jnp.arange(0, ROPE_AXIS_DIM, 2, dtype=F32) / ROPE_AXIS_DIM)
    ang_r = pos_row[:, None] * inv
    ang_c = pos_col[:, None] * inv
    cos64 = jnp.concatenate([jnp.cos(ang_r)] * 2 + [jnp.cos(ang_c)] * 2, axis=-1)
    sin64 = jnp.concatenate([-jnp.sin(ang_r), jnp.sin(ang_r), -jnp.sin(ang_c), jnp.sin(ang_c)], axis=-1)
    return jnp.tile(cos64, (1, 2)), jnp.tile(sin64, (1, 2))


def kernel(x, c, ctx, c_ctx, w_mod, b_mod, g_pre_mix, g_post_mix, g_pre_ffn, g_post_ffn,
           w_in, g_q, g_k, g_sg, b_sg, w_s, b_s, w_out, w_ffn_in, w_ffn_out):
    b, n, _ = x.shape
    n_ctx = ctx.shape[1]
    assert n % ROW_TILE == 0 and n_ctx % SUB_ROWS == 0 and n_ctx <= ROW_TILE and b < MOD_ROWS
    cos_t, sin_t = _rope_tables(n)
    cos_c, sin_c = cos_t[:n_ctx], sin_t[:n_ctx]

    cvec = jnp.concatenate([c, c_ctx[None], jnp.zeros((MOD_ROWS - b - 1, D_MODEL), F32)], axis=0)
    mods = _modulation(cvec, w_mod, b_mod)
    mods = mods.reshape(DEPTH, MOD_ROWS, N_MOD, D_MODEL)
    mods = jnp.pad(mods, ((0, 0), (0, 0), (0, MOD_SLOTS - N_MOD), (0, 0)))

    w_in_b = w_in.astype(BF16)
    w_out_b = w_out.astype(BF16)
    w_gu = w_ffn_in.astype(BF16)
    w_down = w_ffn_out.astype(BF16)
    ws_cat = w_s.reshape(DEPTH, MLP_HEADS // 2, 2, CHUNK, CHUNK).transpose(0, 1, 3, 2, 4)
    ws_cat = ws_cat.reshape(DEPTH, MLP_HEADS // 2, CHUNK, 2 * CHUNK).astype(BF16)
    bs_full = jnp.repeat(b_s.transpose(0, 2, 1), HEAD_DIM, axis=2)
    gq_t = jnp.tile(g_q, (1, MXU_DIM // HEAD_DIM))[:, None]
    gk_t = jnp.tile(g_k, (1, LANES // HEAD_DIM))[:, None]

    xc = ctx
    for l in range(DEPTH):
        last = l == DEPTH - 1
        in_args = (g_pre_mix[l][None], w_in_b, gq_t[l], gk_t[l])
        mlp_args = (g_sg[l][None], b_sg[l][None], ws_cat[l], bs_full[l])
        ffn_args = (g_post_mix[l][None], g_pre_ffn[l][None], g_post_ffn[l][None], w_out_b, w_gu, w_down)

        qct, kc, vct, mlp_c = _inproj(xc, mods, l, b, *in_args, cos_c, sin_c, *mlp_args,
                                      tm=n_ctx, use_rope=False)
        qxt, kx, vxt, mlp_x = _inproj(x, mods, l, None, *in_args, cos_t, sin_t, *mlp_args,
                                      tm=2 * ROW_TILE, use_rope=True)
        bound = (BOUND_SLACK * LOG2E * math.sqrt(HEAD_DIM)) * jnp.max(jnp.abs(g_q[l])) * jnp.max(jnp.abs(g_k[l]))
        bound = bound.reshape(1)
        attn_x = _attention(bound, qxt, (kc, kx), (vct, vxt), tq=ROW_TILE, bk=KEY_BLOCK, fast_path=True)
        x = _outffn(attn_x, mlp_x, x, mods, l, None, *ffn_args, tm=ROW_TILE)
        if not last:
            attn_c = _attention(bound, qct, (kc,), (vct,), tq=n_ctx, bk=KEY_BLOCK, fast_path=False)
            xc = _outffn(attn_c, mlp_c, xc, mods, l, b, *ffn_args, tm=n_ctx)
    return x
```

```python
import functools
import math

import jax
import jax.numpy as jnp
from jax import lax
from jax.experimental import pallas as pl
from jax.experimental.pallas import tpu as pltpu

D_MODEL = 1024
DEPTH = 4
GRID_W = 64
HEAD_DIM = 64
ATTN_HEADS = 8
KV_HEADS = 2
GQA_GROUP = ATTN_HEADS // KV_HEADS
ATTN_WIDTH = ATTN_HEADS * HEAD_DIM
KV_WIDTH = KV_HEADS * HEAD_DIM
MLP_HEADS = 8
MLP_WIDTH = MLP_HEADS * HEAD_DIM
CHUNK = 128
IN_WIDTH = ATTN_WIDTH + 2 * KV_WIDTH + 2 * MLP_WIDTH
FFN_HIDDEN = 2816
N_MOD = 6
ROPE_THETA = 10000.0
ROPE_AXIS_DIM = HEAD_DIM // 2
EPS = 1e-6

LANES = 128
MXU_DIM = 256
MOD_ROWS = 16
MOD_SLOTS = 8
SUBLANES = 8
FFN_CHUNK = MXU_DIM
N_FFN_CHUNKS = FFN_HIDDEN // FFN_CHUNK
ROW_TILE = 4 * MXU_DIM
SUB_ROWS = MXU_DIM
KEY_BLOCK = MXU_DIM
PROJ_AHEAD = 2
FFN_AHEAD = 2
QK_AHEAD = 4
LOG2E = math.log2(math.e)
BOUND_SLACK = 1.02
SAFE_EXP2_RANGE = 120.0
VMEM_LIMIT = 56 * 1024 * 1024

F32 = jnp.float32
BF16 = jnp.bfloat16


def _const_spec(shape):
    zeros = (0,) * len(shape)
    return pl.BlockSpec(shape, lambda *_: zeros, pipeline_mode=pl.Buffered(1))


def _layer_spec(layer, shape):
    index = (layer,) + (0,) * len(shape)
    return pl.BlockSpec((1,) + tuple(shape), lambda *_: index, pipeline_mode=pl.Buffered(1))


def _rms(t, g):
    return t * lax.rsqrt(jnp.mean(t * t, axis=-1, keepdims=True) + EPS) * g


def _mod_kernel(c_ref, w_ref, b_ref, o_ref):
    cv = c_ref[...]
    act = (cv * jax.nn.sigmoid(cv)).astype(BF16)
    o_ref[0] = jnp.dot(act, w_ref[0].astype(BF16), preferred_element_type=F32) + b_ref[0]


def _modulation(cvec, w_mod, b_mod):
    tn = 1536
    n_out = N_MOD * D_MODEL
    return pl.pallas_call(
        _mod_kernel,
        grid=(DEPTH, n_out // tn),
        in_specs=[
            pl.BlockSpec((MOD_ROWS, D_MODEL), lambda l, j: (0, 0)),
            pl.BlockSpec((1, D_MODEL, tn), lambda l, j: (l, 0, j)),
            pl.BlockSpec((1, 1, tn), lambda l, j: (l, 0, j)),
        ],
        out_specs=pl.BlockSpec((1, MOD_ROWS, tn), lambda l, j: (l, 0, j)),
        out_shape=jax.ShapeDtypeStruct((DEPTH, MOD_ROWS, n_out), F32),
        compiler_params=pltpu.CompilerParams(
            dimension_semantics=("arbitrary", "arbitrary"), vmem_limit_bytes=VMEM_LIMIT),
        name="modulation",
    )(cvec, w_mod, b_mod.reshape(DEPTH, 1, n_out))


def _group_sum(t2, ones_bd):
    return jnp.dot(t2.astype(BF16), ones_bd, preferred_element_type=F32)


def _swap_halves16(t):
    lane = lax.broadcasted_iota(jnp.int32, t.shape, 1)
    return jnp.where((lane % 32) < 16, pltpu.roll(t, LANES - 16, 1), pltpu.roll(t, 16, 1))


def _inproj_kernel(x_ref, mod_ref, gpre_ref, w_ref, gq_ref, gk_ref, cos_ref, sin_ref,
                   gsg_ref, bsg_ref, ws_ref, bs_ref,
                   qt_ref, k_ref, vt_ref, mlp_ref, *, tm, use_rope):
    mod = mod_ref[0, 0]
    shift, scale = mod[0:1], mod[1:2]
    ts = min(tm, SUB_ROWS)

    ri = lax.broadcasted_iota(jnp.int32, (MXU_DIM, MXU_DIM), 0) // HEAD_DIM
    ci = lax.broadcasted_iota(jnp.int32, (MXU_DIM, MXU_DIM), 1) // HEAD_DIM
    ones_bd = jnp.where(ri == ci, 1.0, 0.0).astype(BF16)

    def project(r):
        h = (_rms(x_ref[0, r], gpre_ref[...]) * (1.0 + scale) + shift).astype(BF16)
        return jnp.dot(h, w_ref[0], preferred_element_type=F32)

    def head_norm(t, g):
        w = t.shape[1]
        ss = _group_sum(t * t, ones_bd[:w, :w])
        return t * lax.rsqrt(ss * (1.0 / HEAD_DIM) + EPS) * g

    def finish(r, p):
        def rope(t):
            if not use_rope:
                return t
            return t * cos_ref[r, :] + _swap_halves16(t) * sin_ref[r, :]

        q_scale = LOG2E / math.sqrt(HEAD_DIM)
        for half in range(ATTN_WIDTH // MXU_DIM):
            qn = head_norm(p[:, half * MXU_DIM:(half + 1) * MXU_DIM], gq_ref[...])
            for pair in range(MXU_DIM // LANES):
                tt = (rope(qn[:, pair * LANES:(pair + 1) * LANES]) * q_scale).T
                h0 = half * (MXU_DIM // HEAD_DIM) + pair * 2
                qt_ref[0, h0, :, r] = tt[:HEAD_DIM].astype(BF16)
                qt_ref[0, h0 + 1, :, r] = tt[HEAD_DIM:].astype(BF16)

        kn = rope(head_norm(p[:, ATTN_WIDTH:ATTN_WIDTH + KV_WIDTH], gk_ref[...]))
        k_ref[0, 0, r, :] = kn[:, :HEAD_DIM].astype(BF16)
        k_ref[0, 1, r, :] = pltpu.roll(kn, HEAD_DIM, 1)[:, :HEAD_DIM].astype(BF16)

        vvt = p[:, ATTN_WIDTH + KV_WIDTH:ATTN_WIDTH + 2 * KV_WIDTH].T
        for kvh in range(KV_HEADS):
            vt_ref[0, kvh, :, r] = vvt[kvh * HEAD_DIM:(kvh + 1) * HEAD_DIM].astype(BF16)

        z = p[:, ATTN_WIDTH + 2 * KV_WIDTH:]
        gc = math.sqrt(2.0 / math.pi)
        hz = 0.5 * z
        z = hz + hz * jnp.tanh(z * (gc + (gc * 0.044715) * (z * z)))
        u = z[:, :MLP_WIDTH]
        v2 = z[:, MLP_WIDTH:]
        mu = jnp.mean(v2, axis=-1, keepdims=True)
        vc = v2 - mu
        var = jnp.mean(vc * vc, axis=-1, keepdims=True)
        vln = vc * lax.rsqrt(var + EPS) * gsg_ref[...] + bsg_ref[...]
        lane_w = lax.broadcasted_iota(jnp.int32, vln.shape, 1)
        even_head = (lane_w % LANES) < HEAD_DIM
        v_even = jnp.where(even_head, vln, 0.0).astype(BF16)
        v_odd = jnp.where(even_head, 0.0, vln).astype(BF16)
        return u, v_even, v_odd

    def spatial_mix(r, u, v_even, v_odd):
        for ch in range(u.shape[0] // CHUNK):
            rows = slice(ch * CHUNK, (ch + 1) * CHUNK)
            out_rows = slice(r.start + ch * CHUNK, r.start + (ch + 1) * CHUNK)
            for jb in range(MLP_WIDTH // LANES):
                cols = slice(jb * LANES, (jb + 1) * LANES)
                rhs = jnp.concatenate([v_even[rows, cols], v_odd[rows, cols]], axis=0)
                s = jnp.dot(ws_ref[jb], rhs, preferred_element_type=F32)
                mlp_ref[0, out_rows, cols] = (u[rows, cols] * (s + bs_ref[:, cols])).astype(BF16)

    subs = [slice(i, i + ts) for i in range(0, tm, ts)]
    pending = [project(r) for r in subs[:PROJ_AHEAD]]
    for i, r in enumerate(subs):
        p_cur = pending.pop(0)
        if i + PROJ_AHEAD < len(subs):
            pending.append(project(subs[i + PROJ_AHEAD]))
        spatial_mix(r, *finish(r, p_cur))


def _inproj(x, mods, layer, stream_row, g_pre, w_in, gq_t, gk_t, cos_t, sin_t,
            g_sg, b_sg, ws_cat, bs_full, *, tm, use_rope):
    b, n, _ = x.shape
    if stream_row is None:
        mod_map = lambda i, j: (layer, i, 0, 0)
    else:
        mod_map = lambda i, j: (layer, stream_row, 0, 0)
    kern = functools.partial(_inproj_kernel, tm=tm, use_rope=use_rope)
    return pl.pallas_call(
        kern,
        grid=(b, n // tm),
        in_specs=[
            pl.BlockSpec((1, tm, D_MODEL), lambda i, j: (i, j, 0)),
            pl.BlockSpec((1, 1, MOD_SLOTS, D_MODEL), mod_map),
            _const_spec((1, D_MODEL)),
            _layer_spec(layer, (D_MODEL, IN_WIDTH)),
            _const_spec((1, MXU_DIM)),
            _const_spec((1, LANES)),
            pl.BlockSpec((tm, LANES), lambda i, j: (j, 0)),
            pl.BlockSpec((tm, LANES), lambda i, j: (j, 0)),
            _const_spec((1, MLP_WIDTH)),
            _const_spec((1, MLP_WIDTH)),
            _const_spec((MLP_WIDTH // LANES, CHUNK, 2 * CHUNK)),
            _const_spec((CHUNK, MLP_WIDTH)),
        ],
        out_specs=[
            pl.BlockSpec((1, ATTN_HEADS, HEAD_DIM, tm), lambda i, j: (i, 0, 0, j)),
            pl.BlockSpec((1, KV_HEADS, tm, HEAD_DIM), lambda i, j: (i, 0, j, 0)),
            pl.BlockSpec((1, KV_HEADS, HEAD_DIM, tm), lambda i, j: (i, 0, 0, j)),
            pl.BlockSpec((1, tm, MLP_WIDTH), lambda i, j: (i, j, 0)),
        ],
        out_shape=[
            jax.ShapeDtypeStruct((b, ATTN_HEADS, HEAD_DIM, n), BF16),
            jax.ShapeDtypeStruct((b, KV_HEADS, n, HEAD_DIM), BF16),
            jax.ShapeDtypeStruct((b, KV_HEADS, HEAD_DIM, n), BF16),
            jax.ShapeDtypeStruct((b, n, MLP_WIDTH), BF16),
        ],
        compiler_params=pltpu.CompilerParams(
            dimension_semantics=("arbitrary", "arbitrary"), vmem_limit_bytes=VMEM_LIMIT),
        name="inproj",
    )(x, mods, g_pre, w_in, gq_t, gk_t, cos_t, sin_t, g_sg, b_sg, ws_cat, bs_full)


def _attn_kernel(*refs, tq, seg_lens, bk):
    n_seg = len(seg_lens)
    bound_ref, qt_ref = refs[0], refs[1]
    segments = list(zip(refs[2:2 + n_seg], refs[2 + n_seg:2 + 2 * n_seg], seg_lens))
    o_ref = refs[2 + 2 * n_seg]
    ts = min(tq, SUB_ROWS)

    def key_sum(pf):
        return jnp.sum(pf.reshape(pf.shape[0] // SUBLANES, SUBLANES, pf.shape[1]), axis=0)

    def normalised(acc, den):
        pairs = []
        for pair in range(GQA_GROUP // 2):
            ot = [acc[g] / jnp.sum(den[g], axis=0, keepdims=True) for g in (2 * pair, 2 * pair + 1)]
            pairs.append(jnp.concatenate(ot, axis=0).T)
        return jnp.concatenate(pairs, axis=1).astype(BF16)

    def fixed_shift_path():
        subs = [slice(i, i + ts) for i in range(0, tq, ts)]
        acc = {(t, h): jnp.zeros((HEAD_DIM, ts), F32) for t in range(len(subs)) for h in range(GQA_GROUP)}
        den = {(t, h): jnp.zeros((SUBLANES, ts), F32) for t in range(len(subs)) for h in range(GQA_GROUP)}

        def finish(t, h, s, vt_ref, start, size, last):
            pf = jnp.exp2(s - bound_ref[0])
            vt_blk = vt_ref[0, 0, :, start:start + size]
            acc[t, h] = acc[t, h] + jnp.dot(vt_blk, pf.astype(BF16), preferred_element_type=F32)
            den[t, h] = den[t, h] + key_sum(pf)
            if last:
                o_ref[0, subs[t], :] = normalised([acc[t, g] for g in range(GQA_GROUP)],
                                                  [den[t, g] for g in range(GQA_GROUP)])

        blocks = [(k_ref, vt_ref, start, min(bk, seg - start))
                  for k_ref, vt_ref, seg in segments for start in range(0, seg, bk)]
        pending = []
        for t, r in enumerate(subs):
            for bi, (k_ref, vt_ref, start, size) in enumerate(blocks):
                for h in range(GQA_GROUP):
                    k_blk = k_ref[0, 0, start:start + size, :]
                    s = jnp.dot(k_blk, qt_ref[0, h, :, r], preferred_element_type=F32)
                    last = bi == len(blocks) - 1 and h == GQA_GROUP - 1
                    pending.append((t, h, s, vt_ref, start, size, last))
                    if len(pending) > QK_AHEAD:
                        finish(*pending.pop(0))
        while pending:
            finish(*pending.pop(0))

    def online_max_path():
        def sub_tile(t, carry):
            q0 = pl.multiple_of(t * ts, ts)
            qs = [qt_ref[0, h, :, pl.ds(q0, ts)] for h in range(GQA_GROUP)]
            state = tuple((jnp.full((1, ts), -jnp.inf, F32), jnp.zeros((SUBLANES, ts), F32),
                           jnp.zeros((HEAD_DIM, ts), F32)) for _ in range(GQA_GROUP))
            for k_ref, vt_ref, seg in segments:
                def key_block(j, st, k_ref=k_ref, vt_ref=vt_ref):
                    k0 = pl.multiple_of(j * bk, bk)
                    k_blk = k_ref[0, 0, pl.ds(k0, bk), :]
                    vt_blk = vt_ref[0, 0, :, pl.ds(k0, bk)]
                    new = []
                    for h in range(GQA_GROUP):
                        m, den, acc = st[h]
                        s = jnp.dot(k_blk, qs[h], preferred_element_type=F32)
                        m_new = jnp.maximum(m, jnp.max(s, axis=0, keepdims=True))
                        alpha = jnp.exp2(m - m_new)
                        pf = jnp.exp2(s - m_new)
                        new.append((m_new, alpha * den + key_sum(pf),
                                    alpha * acc + jnp.dot(vt_blk, pf.astype(BF16), preferred_element_type=F32)))
                    return tuple(new)
                state = lax.fori_loop(0, seg // bk, key_block, state)
            o_ref[0, pl.ds(q0, ts), :] = normalised([st[2] for st in state], [st[1] for st in state])
            return carry
        lax.fori_loop(0, tq // ts, sub_tile, 0)

    safe = 2.0 * bound_ref[0] <= SAFE_EXP2_RANGE
    pl.when(safe)(fixed_shift_path)
    pl.when(jnp.logical_not(safe))(online_max_path)


def _attention(bound, qt, ks, vts, *, tq, bk):
    b, _, _, n = qt.shape
    seg_lens = tuple(k.shape[2] for k in ks)
    assert all(s % bk == 0 for s in seg_lens)
    kern = functools.partial(_attn_kernel, tq=tq, seg_lens=seg_lens, bk=bk)
    in_specs = [pl.BlockSpec(memory_space=pltpu.SMEM),
                pl.BlockSpec((1, GQA_GROUP, HEAD_DIM, tq), lambda i, g, j: (i, g, 0, j))]
    in_specs += [pl.BlockSpec((1, 1, s, HEAD_DIM), lambda i, g, j: (i, g, 0, 0)) for s in seg_lens]
    in_specs += [pl.BlockSpec((1, 1, HEAD_DIM, s), lambda i, g, j: (i, g, 0, 0)) for s in seg_lens]
    return pl.pallas_call(
        kern,
        grid=(b, KV_HEADS, n // tq),
        in_specs=in_specs,
        out_specs=pl.BlockSpec((1, tq, GQA_GROUP * HEAD_DIM), lambda i, g, j: (i, j, g)),
        out_shape=jax.ShapeDtypeStruct((b, n, ATTN_WIDTH), BF16),
        compiler_params=pltpu.CompilerParams(
            dimension_semantics=("arbitrary", "arbitrary", "arbitrary"), vmem_limit_bytes=VMEM_LIMIT),
        name="attention",
    )(bound, qt, *ks, *vts)


def _outffn_kernel(attn_ref, mlp_ref, x_ref, mod_ref, gpm_ref, gpf_ref, gqf_ref,
                   wo_ref, wgu_ref, wd_ref, o_ref, *, tm):
    mod = mod_ref[0, 0]
    gate_mix, shift, scale, gate_ffn = mod[2:3], mod[3:4], mod[4:5], mod[5:6]

    ts = min(tm, SUB_ROWS)
    subs = [slice(i, i + ts) for i in range(0, tm, ts)]

    def out_proj(r):
        return (jnp.dot(attn_ref[0, r], wo_ref[0, :ATTN_WIDTH], preferred_element_type=F32)
                + jnp.dot(mlp_ref[0, r], wo_ref[0, ATTN_WIDTH:], preferred_element_type=F32))

    def ffn(h):
        def gate_up(ci):
            cols = slice(ci * FFN_CHUNK, (ci + 1) * FFN_CHUNK)
            up_cols = slice(FFN_HIDDEN + ci * FFN_CHUNK, FFN_HIDDEN + (ci + 1) * FFN_CHUNK)
            return (jnp.dot(h, wgu_ref[0, :, cols], preferred_element_type=F32),
                    jnp.dot(h, wgu_ref[0, :, up_cols], preferred_element_type=F32))

        pending = [gate_up(ci) for ci in range(FFN_AHEAD)]
        y = None
        for ci in range(N_FFN_CHUNKS):
            g, u = pending.pop(0)
            if ci + FFN_AHEAD < N_FFN_CHUNKS:
                pending.append(gate_up(ci + FFN_AHEAD))
            a = (g * jax.nn.sigmoid(g) * u).astype(BF16)
            d = jnp.dot(a, wd_ref[0, ci * FFN_CHUNK:(ci + 1) * FFN_CHUNK], preferred_element_type=F32)
            y = d if y is None else y + d
        return y

    out_next = out_proj(subs[0])
    for i, r in enumerate(subs):
        out = out_next
        if i + 1 < len(subs):
            out_next = out_proj(subs[i + 1])
        x1 = x_ref[0, r] + gate_mix * _rms(out, gpm_ref[...])
        h = (_rms(x1, gpf_ref[...]) * (1.0 + scale) + shift).astype(BF16)
        o_ref[0, r] = x1 + gate_ffn * _rms(ffn(h), gqf_ref[...])


def _outffn(attn, mlp, x, mods, layer, stream_row, g_post_mix, g_pre_ffn, g_post_ffn,
            w_out, w_gu, w_down, *, tm):
    b, n, _ = x.shape
    if stream_row is None:
        mod_map = lambda i, j: (layer, i, 0, 0)
    else:
        mod_map = lambda i, j: (layer, stream_row, 0, 0)
    row_spec = lambda w: pl.BlockSpec((1, tm, w), lambda i, j: (i, j, 0))
    return pl.pallas_call(
        functools.partial(_outffn_kernel, tm=tm),
        grid=(b, n // tm),
        in_specs=[
            row_spec(ATTN_WIDTH),
            row_spec(MLP_WIDTH),
            row_spec(D_MODEL),
            pl.BlockSpec((1, 1, MOD_SLOTS, D_MODEL), mod_map),
            _const_spec((1, D_MODEL)),
            _const_spec((1, D_MODEL)),
            _const_spec((1, D_MODEL)),
            _layer_spec(layer, (ATTN_WIDTH + MLP_WIDTH, D_MODEL)),
            _layer_spec(layer, (D_MODEL, 2 * FFN_HIDDEN)),
            _layer_spec(layer, (FFN_HIDDEN, D_MODEL)),
        ],
        out_specs=row_spec(D_MODEL),
        out_shape=jax.ShapeDtypeStruct((b, n, D_MODEL), F32),
        compiler_params=pltpu.CompilerParams(
            dimension_semantics=("arbitrary", "arbitrary"), vmem_limit_bytes=VMEM_LIMIT),
        name="outffn",
    )(attn, mlp, x, mods, g_post_mix, g_pre_ffn, g_post_ffn, w_out, w_gu, w_down)


def _rope_tables(n):
    rows = n // GRID_W
    pos_row = jnp.broadcast_to(jnp.arange(rows, dtype=F32)[:, None], (rows, GRID_W)).reshape(-1)
    pos_col = jnp.broadcast_to(jnp.arange(GRID_W, dtype=F32)[None, :], (rows, GRID_W)).reshape(-1)
    inv = ROPE_THETA ** (-jnp.arange(0, ROPE_AXIS_DIM, 2, dtype=F32) / ROPE_AXIS_DIM)
    ang_r = pos_row[:, None] * inv
    ang_c = pos_col[:, None] * inv
    cos64 = jnp.concatenate([jnp.cos(ang_r)] * 2 + [jnp.cos(ang_c)] * 2, axis=-1)
    sin64 = jnp.concatenate([-jnp.sin(ang_r), jnp.sin(ang_r), -jnp.sin(ang_c), jnp.sin(ang_c)], axis=-1)
    return jnp.tile(cos64, (1, 2)), jnp.tile(sin64, (1, 2))


def kernel(x, c, ctx, c_ctx, w_mod, b_mod, g_pre_mix, g_post_mix, g_pre_ffn, g_post_ffn,
           w_in, g_q, g_k, g_sg, b_sg, w_s, b_s, w_out, w_ffn_in, w_ffn_out):
    b, n, _ = x.shape
    n_ctx = ctx.shape[1]
    assert n % ROW_TILE == 0 and n_ctx % SUB_ROWS == 0 and n_ctx <= ROW_TILE and b < MOD_ROWS
    cos_t, sin_t = _rope_tables(n)
    cos_c, sin_c = cos_t[:n_ctx], sin_t[:n_ctx]

    cvec = jnp.concatenate([c, c_ctx[None], jnp.zeros((MOD_ROWS - b - 1, D_MODEL), F32)], axis=0)
    mods = _modulation(cvec, w_mod, b_mod)
    mods = mods.reshape(DEPTH, MOD_ROWS, N_MOD, D_MODEL)
    mods = jnp.pad(mods, ((0, 0), (0, 0), (0, MOD_SLOTS - N_MOD), (0, 0)))

    w_in_b = w_in.astype(BF16)
    w_out_b = w_out.astype(BF16)
    w_gu = w_ffn_in.astype(BF16)
    w_down = w_ffn_out.astype(BF16)
    ws_cat = w_s.reshape(DEPTH, MLP_HEADS // 2, 2, CHUNK, CHUNK).transpose(0, 1, 3, 2, 4)
    ws_cat = ws_cat.reshape(DEPTH, MLP_HEADS // 2, CHUNK, 2 * CHUNK).astype(BF16)
    bs_full = jnp.repeat(b_s.transpose(0, 2, 1), HEAD_DIM, axis=2)
    gq_t = jnp.tile(g_q, (1, MXU_DIM // HEAD_DIM))[:, None]
    gk_t = jnp.tile(g_k, (1, LANES // HEAD_DIM))[:, None]

    xc = ctx
    for l in range(DEPTH):
        last = l == DEPTH - 1
        in_args = (g_pre_mix[l][None], w_in_b, gq_t[l], gk_t[l])
        mlp_args = (g_sg[l][None], b_sg[l][None], ws_cat[l], bs_full[l])
        ffn_args = (g_post_mix[l][None], g_pre_ffn[l][None], g_post_ffn[l][None], w_out_b, w_gu, w_down)

        qct, kc, vct, mlp_c = _inproj(xc, mods, l, b, *in_args, cos_c, sin_c, *mlp_args,
                                      tm=n_ctx, use_rope=False)
        qxt, kx, vxt, mlp_x = _inproj(x, mods, l, None, *in_args, cos_t, sin_t, *mlp_args,
                                      tm=2 * ROW_TILE, use_rope=True)
        bound = (BOUND_SLACK * LOG2E * math.sqrt(HEAD_DIM)) * jnp.max(jnp.abs(g_q[l])) * jnp.max(jnp.abs(g_k[l]))
        bound = bound.reshape(1)
        attn_x = _attention(bound, qxt, (kc, kx), (vct, vxt), tq=ROW_TILE, bk=KEY_BLOCK)
        x = _outffn(attn_x, mlp_x, x, mods, l, None, *ffn_args, tm=ROW_TILE)
        if not last:
            attn_c = _attention(bound, qct, (kc,), (vct,), tq=n_ctx, bk=KEY_BLOCK)
            xc = _outffn(attn_c, mlp_c, xc, mods, l, b, *ffn_args, tm=n_ctx)
    return x
```

```python
import functools
import math

import jax
import jax.numpy as jnp
from jax import lax
from jax.experimental import pallas as pl
from jax.experimental.pallas import tpu as pltpu

D_MODEL = 1024
DEPTH = 4
GRID_W = 64
HEAD_DIM = 64
ATTN_HEADS = 8
KV_HEADS = 2
GQA_GROUP = ATTN_HEADS // KV_HEADS
ATTN_WIDTH = ATTN_HEADS * HEAD_DIM
KV_WIDTH = KV_HEADS * HEAD_DIM
MLP_HEADS = 8
MLP_WIDTH = MLP_HEADS * HEAD_DIM
CHUNK = 128
IN_WIDTH = ATTN_WIDTH + 2 * KV_WIDTH + 2 * MLP_WIDTH
FFN_HIDDEN = 2816
N_MOD = 6
ROPE_THETA = 10000.0
ROPE_AXIS_DIM = HEAD_DIM // 2
EPS = 1e-6

LANES = 128
MXU_DIM = 256
MOD_ROWS = 16
MOD_SLOTS = 8
SUBLANES = 8
FFN_CHUNK = MXU_DIM
N_FFN_CHUNKS = FFN_HIDDEN // FFN_CHUNK
ROW_TILE = 4 * MXU_DIM
SUB_ROWS = MXU_DIM
KEY_BLOCK = MXU_DIM
PROJ_AHEAD = 2
FFN_AHEAD = 2
QK_AHEAD = 4
LOG2E = math.log2(math.e)
BOUND_SLACK = 1.02
SAFE_EXP2_RANGE = 120.0
VMEM_LIMIT = 56 * 1024 * 1024

F32 = jnp.float32
BF16 = jnp.bfloat16


def _const_spec(shape):
    zeros = (0,) * len(shape)
    return pl.BlockSpec(shape, lambda *_: zeros, pipeline_mode=pl.Buffered(1))


def _layer_spec(layer, shape):
    index = (layer,) + (0,) * len(shape)
    return pl.BlockSpec((1,) + tuple(shape), lambda *_: index, pipeline_mode=pl.Buffered(1))


def _rms(t, g):
    return t * lax.rsqrt(jnp.mean(t * t, axis=-1, keepdims=True) + EPS) * g


def _mod_kernel(c_ref, w_ref, b_ref, o_ref):
    cv = c_ref[...]
    act = (cv * jax.nn.sigmoid(cv)).astype(BF16)
    o_ref[0] = jnp.dot(act, w_ref[0].astype(BF16), preferred_element_type=F32) + b_ref[0]


def _modulation(cvec, w_mod, b_mod):
    tn = 1536
    n_out = N_MOD * D_MODEL
    return pl.pallas_call(
        _mod_kernel,
        grid=(DEPTH, n_out // tn),
        in_specs=[
            pl.BlockSpec((MOD_ROWS, D_MODEL), lambda l, j: (0, 0)),
            pl.BlockSpec((1, D_MODEL, tn), lambda l, j: (l, 0, j)),
            pl.BlockSpec((1, 1, tn), lambda l, j: (l, 0, j)),
        ],
        out_specs=pl.BlockSpec((1, MOD_ROWS, tn), lambda l, j: (l, 0, j)),
        out_shape=jax.ShapeDtypeStruct((DEPTH, MOD_ROWS, n_out), F32),
        compiler_params=pltpu.CompilerParams(
            dimension_semantics=("arbitrary", "arbitrary"), vmem_limit_bytes=VMEM_LIMIT),
        name="modulation",
    )(cvec, w_mod, b_mod.reshape(DEPTH, 1, n_out))


def _group_sum(t2, ones_bd):
    return jnp.dot(t2.astype(BF16), ones_bd, preferred_element_type=F32)


def _swap_halves16(t):
    lane = lax.broadcasted_iota(jnp.int32, t.shape, 1)
    return jnp.where((lane % 32) < 16, pltpu.roll(t, LANES - 16, 1), pltpu.roll(t, 16, 1))


def _inproj_kernel(x_ref, mod_ref, gpre_ref, w_ref, gq_ref, gk_ref, cos_ref, sin_ref,
                   gsg_ref, bsg_ref, ws_ref, bs_ref,
                   qt_ref, k_ref, vt_ref, mlp_ref, *, tm, use_rope):
    mod = mod_ref[0, 0]
    shift, scale = mod[0:1], mod[1:2]
    ts = min(tm, SUB_ROWS)

    ri = lax.broadcasted_iota(jnp.int32, (MXU_DIM, MXU_DIM), 0) // HEAD_DIM
    ci = lax.broadcasted_iota(jnp.int32, (MXU_DIM, MXU_DIM), 1) // HEAD_DIM
    ones_bd = jnp.where(ri == ci, 1.0, 0.0).astype(BF16)

    def project(r):
        h = (_rms(x_ref[0, r], gpre_ref[...]) * (1.0 + scale) + shift).astype(BF16)
        return jnp.dot(h, w_ref[0], preferred_element_type=F32)

    def head_norm(t, g):
        w = t.shape[1]
        ss = _group_sum(t * t, ones_bd[:w, :w])
        return t * lax.rsqrt(ss * (1.0 / HEAD_DIM) + EPS) * g

    def finish(r, p):
        def rope(t):
            if not use_rope:
                return t
            return t * cos_ref[r, :] + _swap_halves16(t) * sin_ref[r, :]

        q_scale = LOG2E / math.sqrt(HEAD_DIM)
        for half in range(ATTN_WIDTH // MXU_DIM):
            qn = head_norm(p[:, half * MXU_DIM:(half + 1) * MXU_DIM], gq_ref[...])
            for pair in range(MXU_DIM // LANES):
                tt = (rope(qn[:, pair * LANES:(pair + 1) * LANES]) * q_scale).T
                h0 = half * (MXU_DIM // HEAD_DIM) + pair * 2
                qt_ref[0, h0, :, r] = tt[:HEAD_DIM].astype(BF16)
                qt_ref[0, h0 + 1, :, r] = tt[HEAD_DIM:].astype(BF16)

        kn = rope(head_norm(p[:, ATTN_WIDTH:ATTN_WIDTH + KV_WIDTH], gk_ref[...]))
        k_ref[0, 0, r, :] = kn[:, :HEAD_DIM].astype(BF16)
        k_ref[0, 1, r, :] = pltpu.roll(kn, HEAD_DIM, 1)[:, :HEAD_DIM].astype(BF16)

        vvt = p[:, ATTN_WIDTH + KV_WIDTH:ATTN_WIDTH + 2 * KV_WIDTH].T
        for kvh in range(KV_HEADS):
            vt_ref[0, kvh, :, r] = vvt[kvh * HEAD_DIM:(kvh + 1) * HEAD_DIM].astype(BF16)

        z = p[:, ATTN_WIDTH + 2 * KV_WIDTH:]
        gc = math.sqrt(2.0 / math.pi)
        hz = 0.5 * z
        z = hz + hz * jnp.tanh(z * (gc + (gc * 0.044715) * (z * z)))
        u = z[:, :MLP_WIDTH]
        v2 = z[:, MLP_WIDTH:]
        mu = jnp.mean(v2, axis=-1, keepdims=True)
        vc = v2 - mu
        var = jnp.mean(vc * vc, axis=-1, keepdims=True)
        vln = vc * lax.rsqrt(var + EPS) * gsg_ref[...] + bsg_ref[...]
        lane_w = lax.broadcasted_iota(jnp.int32, vln.shape, 1)
        even_head = (lane_w % LANES) < HEAD_DIM
        v_even = jnp.where(even_head, vln, 0.0).astype(BF16)
        v_odd = jnp.where(even_head, 0.0, vln).astype(BF16)
        return u, v_even, v_odd

    def spatial_mix(r, u, v_even, v_odd):
        for ch in range(u.shape[0] // CHUNK):
            rows = slice(ch * CHUNK, (ch + 1) * CHUNK)
            out_rows = slice(r.start + ch * CHUNK, r.start + (ch + 1) * CHUNK)
            for jb in range(MLP_WIDTH // LANES):
                cols = slice(jb * LANES, (jb + 1) * LANES)
                rhs = jnp.concatenate([v_even[rows, cols], v_odd[rows, cols]], axis=0)
                s = jnp.dot(ws_ref[jb], rhs, preferred_element_type=F32)
                mlp_ref[0, out_rows, cols] = (u[rows, cols] * (s + bs_ref[:, cols])).astype(BF16)

    subs = [slice(i, i + ts) for i in range(0, tm, ts)]
    pending = [project(r) for r in subs[:PROJ_AHEAD]]
    for i, r in enumerate(subs):
        p_cur = pending.pop(0)
        if i + PROJ_AHEAD < len(subs):
            pending.append(project(subs[i + PROJ_AHEAD]))
        spatial_mix(r, *finish(r, p_cur))


def _inproj(x, mods, layer, stream_row, g_pre, w_in, gq_t, gk_t, cos_t, sin_t,
            g_sg, b_sg, ws_cat, bs_full, *, tm, use_rope):
    b, n, _ = x.shape
    if stream_row is None:
        mod_map = lambda i, j: (layer, i, 0, 0)
    else:
        mod_map = lambda i, j: (layer, stream_row, 0, 0)
    kern = functools.partial(_inproj_kernel, tm=tm, use_rope=use_rope)
    return pl.pallas_call(
        kern,
        grid=(b, n // tm),
        in_specs=[
            pl.BlockSpec((1, tm, D_MODEL), lambda i, j: (i, j, 0)),
            pl.BlockSpec((1, 1, MOD_SLOTS, D_MODEL), mod_map),
            _const_spec((1, D_MODEL)),
            _layer_spec(layer, (D_MODEL, IN_WIDTH)),
            _const_spec((1, MXU_DIM)),
            _const_spec((1, LANES)),
            pl.BlockSpec((tm, LANES), lambda i, j: (j, 0)),
            pl.BlockSpec((tm, LANES), lambda i, j: (j, 0)),
            _const_spec((1, MLP_WIDTH)),
            _const_spec((1, MLP_WIDTH)),
            _const_spec((MLP_WIDTH // LANES, CHUNK, 2 * CHUNK)),
            _const_spec((CHUNK, MLP_WIDTH)),
        ],
        out_specs=[
            pl.BlockSpec((1, ATTN_HEADS, HEAD_DIM, tm), lambda i, j: (i, 0, 0, j)),
            pl.BlockSpec((1, KV_HEADS, tm, HEAD_DIM), lambda i, j: (i, 0, j, 0)),
            pl.BlockSpec((1, KV_HEADS, HEAD_DIM, tm), lambda i, j: (i, 0, 0, j)),
            pl.BlockSpec((1, tm, MLP_WIDTH), lambda i, j: (i, j, 0)),
        ],
        out_shape=[
            jax.ShapeDtypeStruct((b, ATTN_HEADS, HEAD_DIM, n), BF16),
            jax.ShapeDtypeStruct((b, KV_HEADS, n, HEAD_DIM), BF16),
            jax.ShapeDtypeStruct((b, KV_HEADS, HEAD_DIM, n), BF16),
            jax.ShapeDtypeStruct((b, n, MLP_WIDTH), BF16),
        ],
        compiler_params=pltpu.CompilerParams(
            dimension_semantics=("arbitrary", "arbitrary"), vmem_limit_bytes=VMEM_LIMIT),
        name="inproj",
    )(x, mods, g_pre, w_in, gq_t, gk_t, cos_t, sin_t, g_sg, b_sg, ws_cat, bs_full)


def _attn_kernel(*refs, tq, seg_lens, bk):
    n_seg = len(seg_lens)
    bound_ref, qt_ref = refs[0], refs[1]
    segments = list(zip(refs[2:2 + n_seg], refs[2 + n_seg:2 + 2 * n_seg], seg_lens))
    o_ref = refs[2 + 2 * n_seg]
    ts = min(tq, SUB_ROWS)

    def key_sum(pf):
        return jnp.sum(pf.reshape(pf.shape[0] // SUBLANES, SUBLANES, pf.shape[1]), axis=0)

    def normalised(acc, den):
        pairs = []
        for pair in range(GQA_GROUP // 2):
            ot = [acc[g] / jnp.sum(den[g], axis=0, keepdims=True) for g in (2 * pair, 2 * pair + 1)]
            pairs.append(jnp.concatenate(ot, axis=0).T)
        return jnp.concatenate(pairs, axis=1).astype(BF16)

    def fixed_shift_path():
        subs = [slice(i, i + ts) for i in range(0, tq, ts)]
        acc = {(t, h): jnp.zeros((HEAD_DIM, ts), F32) for t in range(len(subs)) for h in range(GQA_GROUP)}
        den = {(t, h): jnp.zeros((SUBLANES, ts), F32) for t in range(len(subs)) for h in range(GQA_GROUP)}

        def finish(t, h, s, vt_ref, start, size, last):
            pf = jnp.exp2(s - bound_ref[0])
            vt_blk = vt_ref[0, 0, :, start:start + size]
            acc[t, h] = acc[t, h] + jnp.dot(vt_blk, pf.astype(BF16), preferred_element_type=F32)
            den[t, h] = den[t, h] + key_sum(pf)
            if last:
                o_ref[0, subs[t], :] = normalised([acc[t, g] for g in range(GQA_GROUP)],
                                                  [den[t, g] for g in range(GQA_GROUP)])

        blocks = [(k_ref, vt_ref, start, min(bk, seg - start))
                  for k_ref, vt_ref, seg in segments for start in range(0, seg, bk)]
        pending = []
        for t, r in enumerate(subs):
            for bi, (k_ref, vt_ref, start, size) in enumerate(blocks):
                for h in range(GQA_GROUP):
                    k_blk = k_ref[0, 0, start:start + size, :]
                    s = jnp.dot(k_blk, qt_ref[0, h, :, r], preferred_element_type=F32)
                    last = bi == len(blocks) - 1 and h == GQA_GROUP - 1
                    pending.append((t, h, s, vt_ref, start, size, last))
                    if len(pending) > QK_AHEAD:
                        finish(*pending.pop(0))
        while pending:
            finish(*pending.pop(0))

    def online_max_path():
        def sub_tile(t, carry):
            q0 = pl.multiple_of(t * ts, ts)
            qs = [qt_ref[0, h, :, pl.ds(q0, ts)] for h in range(GQA_GROUP)]
            state = tuple((jnp.full((1, ts), -jnp.inf, F32), jnp.zeros((SUBLANES, ts), F32),
                           jnp.zeros((HEAD_DIM, ts), F32)) for _ in range(GQA_GROUP))
            for k_ref, vt_ref, seg in segments:
                def key_block(j, st, k_ref=k_ref, vt_ref=vt_ref):
                    k0 = pl.multiple_of(j * bk, bk)
                    k_blk = k_ref[0, 0, pl.ds(k0, bk), :]
                    vt_blk = vt_ref[0, 0, :, pl.ds(k0, bk)]
                    new = []
                    for h in range(GQA_GROUP):
                        m, den, acc = st[h]
                        s = jnp.dot(k_blk, qs[h], preferred_element_type=F32)
                        m_new = jnp.maximum(m, jnp.max(s, axis=0, keepdims=True))
                        alpha = jnp.exp2(m - m_new)
                        pf = jnp.exp2(s - m_new)
                        new.append((m_new, alpha * den + key_sum(pf),
                                    alpha * acc + jnp.dot(vt_blk, pf.astype(BF16), preferred_element_type=F32)))
                    return tuple(new)
                state = lax.fori_loop(0, seg // bk, key_block, state)
            o_ref[0, pl.ds(q0, ts), :] = normalised([st[2] for st in state], [st[1] for st in state])
            return carry
        lax.fori_loop(0, tq // ts, sub_tile, 0)

    safe = 2.0 * bound_ref[0] <= SAFE_EXP2_RANGE
    pl.when(safe)(fixed_shift_path)
    pl.when(jnp.logical_not(safe))(online_max_path)


def _attention(bound, qt, ks, vts, *, n_batch, q_pack, seg_packs, tq, bk):
    nq = qt.shape[3] // q_pack
    q_tiles = nq // tq
    seg_lens = tuple(k.shape[2] // p for k, p in zip(ks, seg_packs))
    assert all(s % bk == 0 for s in seg_lens) and nq % tq == 0
    kern = functools.partial(_attn_kernel, tq=tq, seg_lens=seg_lens, bk=bk)
    in_specs = [pl.BlockSpec(memory_space=pltpu.SMEM),
                pl.BlockSpec((1, GQA_GROUP, HEAD_DIM, tq),
                             lambda i, g, j: (i // q_pack, g, 0, (i % q_pack) * q_tiles + j))]
    in_specs += [pl.BlockSpec((1, 1, s, HEAD_DIM), lambda i, g, j, p=p: (i // p, g, i % p, 0))
                 for s, p in zip(seg_lens, seg_packs)]
    in_specs += [pl.BlockSpec((1, 1, HEAD_DIM, s), lambda i, g, j, p=p: (i // p, g, 0, i % p))
                 for s, p in zip(seg_lens, seg_packs)]
    return pl.pallas_call(
        kern,
        grid=(n_batch, KV_HEADS, q_tiles),
        in_specs=in_specs,
        out_specs=pl.BlockSpec((1, tq, GQA_GROUP * HEAD_DIM),
                               lambda i, g, j: (i // q_pack, (i % q_pack) * q_tiles + j, g)),
        out_shape=jax.ShapeDtypeStruct((qt.shape[0], qt.shape[3], ATTN_WIDTH), BF16),
        compiler_params=pltpu.CompilerParams(
            dimension_semantics=("arbitrary", "arbitrary", "arbitrary"), vmem_limit_bytes=VMEM_LIMIT),
        name="attention",
    )(bound, qt, *ks, *vts)


def _outffn_kernel(attn_ref, mlp_ref, x_ref, mod_ref, gpm_ref, gpf_ref, gqf_ref,
                   wo_ref, wgu_ref, wd_ref, o_ref, *, tm):
    mod = mod_ref[0, 0]
    gate_mix, shift, scale, gate_ffn = mod[2:3], mod[3:4], mod[4:5], mod[5:6]

    ts = min(tm, SUB_ROWS)
    subs = [slice(i, i + ts) for i in range(0, tm, ts)]

    def out_proj(r):
        return (jnp.dot(attn_ref[0, r], wo_ref[0, :ATTN_WIDTH], preferred_element_type=F32)
                + jnp.dot(mlp_ref[0, r], wo_ref[0, ATTN_WIDTH:], preferred_element_type=F32))

    def ffn(h):
        def gate_up(ci):
            cols = slice(ci * FFN_CHUNK, (ci + 1) * FFN_CHUNK)
            up_cols = slice(FFN_HIDDEN + ci * FFN_CHUNK, FFN_HIDDEN + (ci + 1) * FFN_CHUNK)
            return (jnp.dot(h, wgu_ref[0, :, cols], preferred_element_type=F32),
                    jnp.dot(h, wgu_ref[0, :, up_cols], preferred_element_type=F32))

        pending = [gate_up(ci) for ci in range(FFN_AHEAD)]
        y = None
        for ci in range(N_FFN_CHUNKS):
            g, u = pending.pop(0)
            if ci + FFN_AHEAD < N_FFN_CHUNKS:
                pending.append(gate_up(ci + FFN_AHEAD))
            a = (g * jax.nn.sigmoid(g) * u).astype(BF16)
            d = jnp.dot(a, wd_ref[0, ci * FFN_CHUNK:(ci + 1) * FFN_CHUNK], preferred_element_type=F32)
            y = d if y is None else y + d
        return y

    out_next = out_proj(subs[0])
    for i, r in enumerate(subs):
        out = out_next
        if i + 1 < len(subs):
            out_next = out_proj(subs[i + 1])
        x1 = x_ref[0, r] + gate_mix * _rms(out, gpm_ref[...])
        h = (_rms(x1, gpf_ref[...]) * (1.0 + scale) + shift).astype(BF16)
        o_ref[0, r] = x1 + gate_ffn * _rms(ffn(h), gqf_ref[...])


def _outffn(attn, mlp, x, mods, layer, stream_row, g_post_mix, g_pre_ffn, g_post_ffn,
            w_out, w_gu, w_down, *, tm):
    b, n, _ = x.shape
    if stream_row is None:
        mod_map = lambda i, j: (layer, i, 0, 0)
    else:
        mod_map = lambda i, j: (layer, stream_row, 0, 0)
    row_spec = lambda w: pl.BlockSpec((1, tm, w), lambda i, j: (i, j, 0))
    return pl.pallas_call(
        functools.partial(_outffn_kernel, tm=tm),
        grid=(b, n // tm),
        in_specs=[
            row_spec(ATTN_WIDTH),
            row_spec(MLP_WIDTH),
            row_spec(D_MODEL),
            pl.BlockSpec((1, 1, MOD_SLOTS, D_MODEL), mod_map),
            _const_spec((1, D_MODEL)),
            _const_spec((1, D_MODEL)),
            _const_spec((1, D_MODEL)),
            _layer_spec(layer, (ATTN_WIDTH + MLP_WIDTH, D_MODEL)),
            _layer_spec(layer, (D_MODEL, 2 * FFN_HIDDEN)),
            _layer_spec(layer, (FFN_HIDDEN, D_MODEL)),
        ],
        out_specs=row_spec(D_MODEL),
        out_shape=jax.ShapeDtypeStruct((b, n, D_MODEL), F32),
        compiler_params=pltpu.CompilerParams(
            dimension_semantics=("arbitrary", "arbitrary"), vmem_limit_bytes=VMEM_LIMIT),
        name="outffn",
    )(attn, mlp, x, mods, g_post_mix, g_pre_ffn, g_post_ffn, w_out, w_gu, w_down)


def _rope_tables(n):
    rows = n // GRID_W
    pos_row = jnp.broadcast_to(jnp.arange(rows, dtype=F32)[:, None], (rows, GRID_W)).reshape(-1)
    pos_col = jnp.broadcast_to(jnp.arange(GRID_W, dtype=F32)[None, :], (rows, GRID_W)).reshape(-1)
    inv = ROPE_THETA ** (-jnp.arange(0, ROPE_AXIS_DIM, 2, dtype=F32) / ROPE_AXIS_DIM)
    ang_r = pos_row[:, None] * inv
    ang_c = pos_col[:, None] * inv
    cos64 = jnp.concatenate([jnp.cos(ang_r)] * 2 + [jnp.cos(ang_c)] * 2, axis=-1)
    sin64 = jnp.concatenate([-jnp.sin(ang_r), jnp.sin(ang_r), -jnp.sin(ang_c), jnp.sin(ang_c)], axis=-1)
    return jnp.tile(cos64, (1, 2)), jnp.tile(sin64, (1, 2))


def kernel(x, c, ctx, c_ctx, w_mod, b_mod, g_pre_mix, g_post_mix, g_pre_ffn, g_post_ffn,
           w_in, g_q, g_k, g_sg, b_sg, w_s, b_s, w_out, w_ffn_in, w_ffn_out):
    b, n, _ = x.shape
    n_ctx = ctx.shape[1]
    c_pack = ROW_TILE // n_ctx
    assert n % (2 * ROW_TILE) == 0 and n_ctx % SUB_ROWS == 0 and ROW_TILE % n_ctx == 0
    assert b % c_pack == 0 and b < MOD_ROWS
    cos_t, sin_t = _rope_tables(n)
    cos_c, sin_c = cos_t[:ROW_TILE], sin_t[:ROW_TILE]

    cvec = jnp.concatenate([c, c_ctx[None], jnp.zeros((MOD_ROWS - b - 1, D_MODEL), F32)], axis=0)
    mods = _modulation(cvec, w_mod, b_mod)
    mods = mods.reshape(DEPTH, MOD_ROWS, N_MOD, D_MODEL)
    mods = jnp.pad(mods, ((0, 0), (0, 0), (0, MOD_SLOTS - N_MOD), (0, 0)))

    w_in_b = w_in.astype(BF16)
    w_out_b = w_out.astype(BF16)
    w_gu = w_ffn_in.astype(BF16)
    w_down = w_ffn_out.astype(BF16)
    ws_cat = w_s.reshape(DEPTH, MLP_HEADS // 2, 2, CHUNK, CHUNK).transpose(0, 1, 3, 2, 4)
    ws_cat = ws_cat.reshape(DEPTH, MLP_HEADS // 2, CHUNK, 2 * CHUNK).astype(BF16)
    bs_full = jnp.repeat(b_s.transpose(0, 2, 1), HEAD_DIM, axis=2)
    gq_t = jnp.tile(g_q, (1, MXU_DIM // HEAD_DIM))[:, None]
    gk_t = jnp.tile(g_k, (1, LANES // HEAD_DIM))[:, None]

    xc = ctx.reshape(b // c_pack, ROW_TILE, D_MODEL)
    for l in range(DEPTH):
        last = l == DEPTH - 1
        in_args = (g_pre_mix[l][None], w_in_b, gq_t[l], gk_t[l])
        mlp_args = (g_sg[l][None], b_sg[l][None], ws_cat[l], bs_full[l])
        ffn_args = (g_post_mix[l][None], g_pre_ffn[l][None], g_post_ffn[l][None], w_out_b, w_gu, w_down)

        qct, kc, vct, mlp_c = _inproj(xc, mods, l, b, *in_args, cos_c, sin_c, *mlp_args,
                                      tm=ROW_TILE, use_rope=False)
        qxt, kx, vxt, mlp_x = _inproj(x, mods, l, None, *in_args, cos_t, sin_t, *mlp_args,
                                      tm=2 * ROW_TILE, use_rope=True)
        bound = (BOUND_SLACK * LOG2E * math.sqrt(HEAD_DIM)) * jnp.max(jnp.abs(g_q[l])) * jnp.max(jnp.abs(g_k[l]))
        bound = bound.reshape(1)
        attn_x = _attention(bound, qxt, (kc, kx), (vct, vxt), n_batch=b, q_pack=1, seg_packs=(c_pack, 1),
                            tq=ROW_TILE, bk=KEY_BLOCK)
        x = _outffn(attn_x, mlp_x, x, mods, l, None, *ffn_args, tm=ROW_TILE)
        if not last:
            attn_c = _attention(bound, qct, (kc,), (vct,), n_batch=b, q_pack=c_pack, seg_packs=(c_pack,),
                                tq=n_ctx, bk=KEY_BLOCK)
            xc = _outffn(attn_c, mlp_c, xc, mods, l, b, *ffn_args, tm=ROW_TILE)
    return x
```

```python
import functools
import math

import jax
import jax.numpy as jnp
from jax import lax
from jax.experimental import pallas as pl
from jax.experimental.pallas import tpu as pltpu

D_MODEL = 1024
DEPTH = 4
GRID_W = 64
HEAD_DIM = 64
ATTN_HEADS = 8
KV_HEADS = 2
GQA_GROUP = ATTN_HEADS // KV_HEADS
ATTN_WIDTH = ATTN_HEADS * HEAD_DIM
KV_WIDTH = KV_HEADS * HEAD_DIM
MLP_HEADS = 8
MLP_WIDTH = MLP_HEADS * HEAD_DIM
CHUNK = 128
IN_WIDTH = ATTN_WIDTH + 2 * KV_WIDTH + 2 * MLP_WIDTH
FFN_HIDDEN = 2816
N_MOD = 6
ROPE_THETA = 10000.0
ROPE_AXIS_DIM = HEAD_DIM // 2
EPS = 1e-6

LANES = 128
MXU_DIM = 256
MOD_ROWS = 16
MOD_SLOTS = 8
SUBLANES = 8
FFN_CHUNK = MXU_DIM
N_FFN_CHUNKS = FFN_HIDDEN // FFN_CHUNK
ROW_TILE = 4 * MXU_DIM
SUB_ROWS = MXU_DIM
KEY_BLOCK = MXU_DIM
PROJ_AHEAD = 2
FFN_AHEAD = 2
QK_AHEAD = 5
LOG2E = math.log2(math.e)
BOUND_SLACK = 1.02
SAFE_EXP2_RANGE = 120.0
VMEM_LIMIT = 56 * 1024 * 1024

F32 = jnp.float32
BF16 = jnp.bfloat16


def _const_spec(shape):
    zeros = (0,) * len(shape)
    return pl.BlockSpec(shape, lambda *_: zeros, pipeline_mode=pl.Buffered(1))


def _layer_spec(layer, shape):
    index = (layer,) + (0,) * len(shape)
    return pl.BlockSpec((1,) + tuple(shape), lambda *_: index, pipeline_mode=pl.Buffered(1))


def _rms(t, g):
    return t * lax.rsqrt(jnp.mean(t * t, axis=-1, keepdims=True) + EPS) * g


def _mod_kernel(c_ref, w_ref, b_ref, o_ref):
    cv = c_ref[...]
    act = (cv * jax.nn.sigmoid(cv)).astype(BF16)
    o_ref[0] = jnp.dot(act, w_ref[0].astype(BF16), preferred_element_type=F32) + b_ref[0]


def _modulation(cvec, w_mod, b_mod):
    tn = 1536
    n_out = N_MOD * D_MODEL
    return pl.pallas_call(
        _mod_kernel,
        grid=(DEPTH, n_out // tn),
        in_specs=[
            pl.BlockSpec((MOD_ROWS, D_MODEL), lambda l, j: (0, 0)),
            pl.BlockSpec((1, D_MODEL, tn), lambda l, j: (l, 0, j)),
            pl.BlockSpec((1, 1, tn), lambda l, j: (l, 0, j)),
        ],
        out_specs=pl.BlockSpec((1, MOD_ROWS, tn), lambda l, j: (l, 0, j)),
        out_shape=jax.ShapeDtypeStruct((DEPTH, MOD_ROWS, n_out), F32),
        compiler_params=pltpu.CompilerParams(
            dimension_semantics=("arbitrary", "arbitrary"), vmem_limit_bytes=VMEM_LIMIT),
        name="modulation",
    )(cvec, w_mod, b_mod.reshape(DEPTH, 1, n_out))


def _group_sum(t2, ones_bd):
    return jnp.dot(t2.astype(BF16), ones_bd, preferred_element_type=F32)


def _swap_halves16(t):
    lane = lax.broadcasted_iota(jnp.int32, t.shape, 1)
    return jnp.where((lane % 32) < 16, pltpu.roll(t, LANES - 16, 1), pltpu.roll(t, 16, 1))


def _inproj_kernel(x_ref, mod_ref, gpre_ref, w_ref, gq_ref, gk_ref, cos_ref, sin_ref,
                   gsg_ref, bsg_ref, ws_ref, bs_ref,
                   qt_ref, k_ref, vt_ref, mlp_ref, *, tm, use_rope):
    mod = mod_ref[0, 0]
    shift, scale = mod[0:1], mod[1:2]
    ts = min(tm, SUB_ROWS)

    ri = lax.broadcasted_iota(jnp.int32, (MXU_DIM, MXU_DIM), 0) // HEAD_DIM
    ci = lax.broadcasted_iota(jnp.int32, (MXU_DIM, MXU_DIM), 1) // HEAD_DIM
    ones_bd = jnp.where(ri == ci, 1.0, 0.0).astype(BF16)

    def project(r):
        h = (_rms(x_ref[0, r], gpre_ref[...]) * (1.0 + scale) + shift).astype(BF16)
        return jnp.dot(h, w_ref[0], preferred_element_type=F32)

    def head_norm(t, g):
        w = t.shape[1]
        ss = _group_sum(t * t, ones_bd[:w, :w])
        return t * lax.rsqrt(ss * (1.0 / HEAD_DIM) + EPS) * g

    def finish(r, p):
        def rope(t):
            if not use_rope:
                return t
            return t * cos_ref[r, :] + _swap_halves16(t) * sin_ref[r, :]

        q_scale = LOG2E / math.sqrt(HEAD_DIM)
        for half in range(ATTN_WIDTH // MXU_DIM):
            qn = head_norm(p[:, half * MXU_DIM:(half + 1) * MXU_DIM], gq_ref[...])
            for pair in range(MXU_DIM // LANES):
                tt = (rope(qn[:, pair * LANES:(pair + 1) * LANES]) * q_scale).T
                h0 = half * (MXU_DIM // HEAD_DIM) + pair * 2
                qt_ref[0, h0, :, r] = tt[:HEAD_DIM].astype(BF16)
                qt_ref[0, h0 + 1, :, r] = tt[HEAD_DIM:].astype(BF16)

        kn = rope(head_norm(p[:, ATTN_WIDTH:ATTN_WIDTH + KV_WIDTH], gk_ref[...]))
        k_ref[0, 0, r, :] = kn[:, :HEAD_DIM].astype(BF16)
        k_ref[0, 1, r, :] = pltpu.roll(kn, HEAD_DIM, 1)[:, :HEAD_DIM].astype(BF16)

        vvt = p[:, ATTN_WIDTH + KV_WIDTH:ATTN_WIDTH + 2 * KV_WIDTH].T
        for kvh in range(KV_HEADS):
            vt_ref[0, kvh, :, r] = vvt[kvh * HEAD_DIM:(kvh + 1) * HEAD_DIM].astype(BF16)

        z = p[:, ATTN_WIDTH + 2 * KV_WIDTH:]
        gc = math.sqrt(2.0 / math.pi)
        hz = 0.5 * z
        z = hz + hz * jnp.tanh(z * (gc + (gc * 0.044715) * (z * z)))
        u = z[:, :MLP_WIDTH]
        v2 = z[:, MLP_WIDTH:]
        mu = jnp.mean(v2, axis=-1, keepdims=True)
        vc = v2 - mu
        var = jnp.mean(vc * vc, axis=-1, keepdims=True)
        vln = vc * lax.rsqrt(var + EPS) * gsg_ref[...] + bsg_ref[...]
        lane_w = lax.broadcasted_iota(jnp.int32, vln.shape, 1)
        even_head = (lane_w % LANES) < HEAD_DIM
        v_even = jnp.where(even_head, vln, 0.0).astype(BF16)
        v_odd = jnp.where(even_head, 0.0, vln).astype(BF16)
        return u, v_even, v_odd

    def spatial_mix(r, u, v_even, v_odd):
        for ch in range(u.shape[0] // CHUNK):
            rows = slice(ch * CHUNK, (ch + 1) * CHUNK)
            out_rows = slice(r.start + ch * CHUNK, r.start + (ch + 1) * CHUNK)
            for jb in range(MLP_WIDTH // LANES):
                cols = slice(jb * LANES, (jb + 1) * LANES)
                rhs = jnp.concatenate([v_even[rows, cols], v_odd[rows, cols]], axis=0)
                s = jnp.dot(ws_ref[jb], rhs, preferred_element_type=F32)
                mlp_ref[0, out_rows, cols] = (u[rows, cols] * (s + bs_ref[:, cols])).astype(BF16)

    subs = [slice(i, i + ts) for i in range(0, tm, ts)]
    pending = [project(r) for r in subs[:PROJ_AHEAD]]
    for i, r in enumerate(subs):
        p_cur = pending.pop(0)
        if i + PROJ_AHEAD < len(subs):
            pending.append(project(subs[i + PROJ_AHEAD]))
        spatial_mix(r, *finish(r, p_cur))


def _inproj(x, mods, layer, stream_row, g_pre, w_in, gq_t, gk_t, cos_t, sin_t,
            g_sg, b_sg, ws_cat, bs_full, *, tm, use_rope):
    b, n, _ = x.shape
    if stream_row is None:
        mod_map = lambda i, j: (layer, i, 0, 0)
    else:
        mod_map = lambda i, j: (layer, stream_row, 0, 0)
    kern = functools.partial(_inproj_kernel, tm=tm, use_rope=use_rope)
    return pl.pallas_call(
        kern,
        grid=(b, n // tm),
        in_specs=[
            pl.BlockSpec((1, tm, D_MODEL), lambda i, j: (i, j, 0)),
            pl.BlockSpec((1, 1, MOD_SLOTS, D_MODEL), mod_map),
            _const_spec((1, D_MODEL)),
            _layer_spec(layer, (D_MODEL, IN_WIDTH)),
            _const_spec((1, MXU_DIM)),
            _const_spec((1, LANES)),
            pl.BlockSpec((tm, LANES), lambda i, j: (j, 0)),
            pl.BlockSpec((tm, LANES), lambda i, j: (j, 0)),
            _const_spec((1, MLP_WIDTH)),
            _const_spec((1, MLP_WIDTH)),
            _const_spec((MLP_WIDTH // LANES, CHUNK, 2 * CHUNK)),
            _const_spec((CHUNK, MLP_WIDTH)),
        ],
        out_specs=[
            pl.BlockSpec((1, ATTN_HEADS, HEAD_DIM, tm), lambda i, j: (i, 0, 0, j)),
            pl.BlockSpec((1, KV_HEADS, tm, HEAD_DIM), lambda i, j: (i, 0, j, 0)),
            pl.BlockSpec((1, KV_HEADS, HEAD_DIM, tm), lambda i, j: (i, 0, 0, j)),
            pl.BlockSpec((1, tm, MLP_WIDTH), lambda i, j: (i, j, 0)),
        ],
        out_shape=[
            jax.ShapeDtypeStruct((b, ATTN_HEADS, HEAD_DIM, n), BF16),
            jax.ShapeDtypeStruct((b, KV_HEADS, n, HEAD_DIM), BF16),
            jax.ShapeDtypeStruct((b, KV_HEADS, HEAD_DIM, n), BF16),
            jax.ShapeDtypeStruct((b, n, MLP_WIDTH), BF16),
        ],
        compiler_params=pltpu.CompilerParams(
            dimension_semantics=("arbitrary", "arbitrary"), vmem_limit_bytes=VMEM_LIMIT),
        name="inproj",
    )(x, mods, g_pre, w_in, gq_t, gk_t, cos_t, sin_t, g_sg, b_sg, ws_cat, bs_full)


def _attn_kernel(*refs, tq, seg_lens, bk):
    n_seg = len(seg_lens)
    bound_ref, qt_ref = refs[0], refs[1]
    segments = list(zip(refs[2:2 + n_seg], refs[2 + n_seg:2 + 2 * n_seg], seg_lens))
    o_ref = refs[2 + 2 * n_seg]
    ts = min(tq, SUB_ROWS)

    def key_sum(pf):
        return jnp.sum(pf.reshape(pf.shape[0] // SUBLANES, SUBLANES, pf.shape[1]), axis=0)

    def normalised(acc, den):
        pairs = []
        for pair in range(GQA_GROUP // 2):
            ot = [acc[g] / jnp.sum(den[g], axis=0, keepdims=True) for g in (2 * pair, 2 * pair + 1)]
            pairs.append(jnp.concatenate(ot, axis=0).T)
        return jnp.concatenate(pairs, axis=1).astype(BF16)

    def fixed_shift_path():
        subs = [slice(i, i + ts) for i in range(0, tq, ts)]
        acc = {(t, h): jnp.zeros((HEAD_DIM, ts), F32) for t in range(len(subs)) for h in range(GQA_GROUP)}
        den = {(t, h): jnp.zeros((SUBLANES, ts), F32) for t in range(len(subs)) for h in range(GQA_GROUP)}

        def finish(t, h, s, vt_ref, start, size, last):
            pf = jnp.exp2(s - bound_ref[0])
            vt_blk = vt_ref[0, 0, :, start:start + size]
            acc[t, h] = acc[t, h] + jnp.dot(vt_blk, pf.astype(BF16), preferred_element_type=F32)
            den[t, h] = den[t, h] + key_sum(pf)
            if last:
                o_ref[0, subs[t], :] = normalised([acc[t, g] for g in range(GQA_GROUP)],
                                                  [den[t, g] for g in range(GQA_GROUP)])

        blocks = [(k_ref, vt_ref, start, min(bk, seg - start))
                  for k_ref, vt_ref, seg in segments for start in range(0, seg, bk)]
        pending = []
        for t, r in enumerate(subs):
            for bi, (k_ref, vt_ref, start, size) in enumerate(blocks):
                for h in range(GQA_GROUP):
                    k_blk = k_ref[0, 0, start:start + size, :]
                    s = jnp.dot(k_blk, qt_ref[0, h, :, r], preferred_element_type=F32)
                    last = bi == len(blocks) - 1 and h == GQA_GROUP - 1
                    pending.append((t, h, s, vt_ref, start, size, last))
                    if len(pending) > QK_AHEAD:
                        finish(*pending.pop(0))
        while pending:
            finish(*pending.pop(0))

    def online_max_path():
        def sub_tile(t, carry):
            q0 = pl.multiple_of(t * ts, ts)
            qs = [qt_ref[0, h, :, pl.ds(q0, ts)] for h in range(GQA_GROUP)]
            state = tuple((jnp.full((1, ts), -jnp.inf, F32), jnp.zeros((SUBLANES, ts), F32),
                           jnp.zeros((HEAD_DIM, ts), F32)) for _ in range(GQA_GROUP))
            for k_ref, vt_ref, seg in segments:
                def key_block(j, st, k_ref=k_ref, vt_ref=vt_ref):
                    k0 = pl.multiple_of(j * bk, bk)
                    k_blk = k_ref[0, 0, pl.ds(k0, bk), :]
                    vt_blk = vt_ref[0, 0, :, pl.ds(k0, bk)]
                    new = []
                    for h in range(GQA_GROUP):
                        m, den, acc = st[h]
                        s = jnp.dot(k_blk, qs[h], preferred_element_type=F32)
                        m_new = jnp.maximum(m, jnp.max(s, axis=0, keepdims=True))
                        alpha = jnp.exp2(m - m_new)
                        pf = jnp.exp2(s - m_new)
                        new.append((m_new, alpha * den + key_sum(pf),
                                    alpha * acc + jnp.dot(vt_blk, pf.astype(BF16), preferred_element_type=F32)))
                    return tuple(new)
                state = lax.fori_loop(0, seg // bk, key_block, state)
            o_ref[0, pl.ds(q0, ts), :] = normalised([st[2] for st in state], [st[1] for st in state])
            return carry
        lax.fori_loop(0, tq // ts, sub_tile, 0)

    safe = 2.0 * bound_ref[0] <= SAFE_EXP2_RANGE
    pl.when(safe)(fixed_shift_path)
    pl.when(jnp.logical_not(safe))(online_max_path)


def _attention(bound, qt, ks, vts, *, tq, bk):
    b, _, _, n = qt.shape
    seg_lens = tuple(k.shape[2] for k in ks)
    assert all(s % bk == 0 for s in seg_lens)
    kern = functools.partial(_attn_kernel, tq=tq, seg_lens=seg_lens, bk=bk)
    in_specs = [pl.BlockSpec(memory_space=pltpu.SMEM),
                pl.BlockSpec((1, GQA_GROUP, HEAD_DIM, tq), lambda i, g, j: (i, g, 0, j))]
    in_specs += [pl.BlockSpec((1, 1, s, HEAD_DIM), lambda i, g, j: (i, g, 0, 0)) for s in seg_lens]
    in_specs += [pl.BlockSpec((1, 1, HEAD_DIM, s), lambda i, g, j: (i, g, 0, 0)) for s in seg_lens]
    return pl.pallas_call(
        kern,
        grid=(b, KV_HEADS, n // tq),
        in_specs=in_specs,
        out_specs=pl.BlockSpec((1, tq, GQA_GROUP * HEAD_DIM), lambda i, g, j: (i, j, g)),
        out_shape=jax.ShapeDtypeStruct((b, n, ATTN_WIDTH), BF16),
        compiler_params=pltpu.CompilerParams(
            dimension_semantics=("arbitrary", "arbitrary", "arbitrary"), vmem_limit_bytes=VMEM_LIMIT),
        name="attention",
    )(bound, qt, *ks, *vts)


def _outffn_kernel(attn_ref, mlp_ref, x_ref, mod_ref, gpm_ref, gpf_ref, gqf_ref,
                   wo_ref, wgu_ref, wd_ref, o_ref, *, tm):
    mod = mod_ref[0, 0]
    gate_mix, shift, scale, gate_ffn = mod[2:3], mod[3:4], mod[4:5], mod[5:6]

    ts = min(tm, SUB_ROWS)
    subs = [slice(i, i + ts) for i in range(0, tm, ts)]

    def out_proj(r):
        return (jnp.dot(attn_ref[0, r], wo_ref[0, :ATTN_WIDTH], preferred_element_type=F32)
                + jnp.dot(mlp_ref[0, r], wo_ref[0, ATTN_WIDTH:], preferred_element_type=F32))

    def ffn(h):
        def gate_up(ci):
            cols = slice(ci * FFN_CHUNK, (ci + 1) * FFN_CHUNK)
            up_cols = slice(FFN_HIDDEN + ci * FFN_CHUNK, FFN_HIDDEN + (ci + 1) * FFN_CHUNK)
            return (jnp.dot(h, wgu_ref[0, :, cols], preferred_element_type=F32),
                    jnp.dot(h, wgu_ref[0, :, up_cols], preferred_element_type=F32))

        pending = [gate_up(ci) for ci in range(FFN_AHEAD)]
        y = None
        for ci in range(N_FFN_CHUNKS):
            g, u = pending.pop(0)
            if ci + FFN_AHEAD < N_FFN_CHUNKS:
                pending.append(gate_up(ci + FFN_AHEAD))
            a = (g * jax.nn.sigmoid(g) * u).astype(BF16)
            d = jnp.dot(a, wd_ref[0, ci * FFN_CHUNK:(ci + 1) * FFN_CHUNK], preferred_element_type=F32)
            y = d if y is None else y + d
        return y

    out_next = out_proj(subs[0])
    for i, r in enumerate(subs):
        out = out_next
        if i + 1 < len(subs):
            out_next = out_proj(subs[i + 1])
        x1 = x_ref[0, r] + gate_mix * _rms(out, gpm_ref[...])
        h = (_rms(x1, gpf_ref[...]) * (1.0 + scale) + shift).astype(BF16)
        o_ref[0, r] = x1 + gate_ffn * _rms(ffn(h), gqf_ref[...])


def _outffn(attn, mlp, x, mods, layer, stream_row, g_post_mix, g_pre_ffn, g_post_ffn,
            w_out, w_gu, w_down, *, tm):
    b, n, _ = x.shape
    if stream_row is None:
        mod_map = lambda i, j: (layer, i, 0, 0)
    else:
        mod_map = lambda i, j: (layer, stream_row, 0, 0)
    row_spec = lambda w: pl.BlockSpec((1, tm, w), lambda i, j: (i, j, 0))
    return pl.pallas_call(
        functools.partial(_outffn_kernel, tm=tm),
        grid=(b, n // tm),
        in_specs=[
            row_spec(ATTN_WIDTH),
            row_spec(MLP_WIDTH),
            row_spec(D_MODEL),
            pl.BlockSpec((1, 1, MOD_SLOTS, D_MODEL), mod_map),
            _const_spec((1, D_MODEL)),
            _const_spec((1, D_MODEL)),
            _const_spec((1, D_MODEL)),
            _layer_spec(layer, (ATTN_WIDTH + MLP_WIDTH, D_MODEL)),
            _layer_spec(layer, (D_MODEL, 2 * FFN_HIDDEN)),
            _layer_spec(layer, (FFN_HIDDEN, D_MODEL)),
        ],
        out_specs=row_spec(D_MODEL),
        out_shape=jax.ShapeDtypeStruct((b, n, D_MODEL), F32),
        compiler_params=pltpu.CompilerParams(
            dimension_semantics=("arbitrary", "arbitrary"), vmem_limit_bytes=VMEM_LIMIT),
        name="outffn",
    )(attn, mlp, x, mods, g_post_mix, g_pre_ffn, g_post_ffn, w_out, w_gu, w_down)


def _rope_tables(n):
    rows = n // GRID_W
    pos_row = jnp.broadcast_to(jnp.arange(rows, dtype=F32)[:, None], (rows, GRID_W)).reshape(-1)
    pos_col = jnp.broadcast_to(jnp.arange(GRID_W, dtype=F32)[None, :], (rows, GRID_W)).reshape(-1)
    inv = ROPE_THETA ** (-jnp.arange(0, ROPE_AXIS_DIM, 2, dtype=F32) / ROPE_AXIS_DIM)
    ang_r = pos_row[:, None] * inv
    ang_c = pos_col[:, None] * inv
    cos64 = jnp.concatenate([jnp.cos(ang_r)] * 2 + [jnp.cos(ang_c)] * 2, axis=-1)
    sin64 = jnp.concatenate([-jnp.sin(ang_r), jnp.sin(ang_r), -jnp.sin(ang_c), jnp.sin(ang_c)], axis=-1)
    return jnp.tile(cos64, (1, 2)), jnp.tile(sin64, (1, 2))


def kernel(x, c, ctx, c_ctx, w_mod, b_mod, g_pre_mix, g_post_mix, g_pre_ffn, g_post_ffn,
           w_in, g_q, g_k, g_sg, b_sg, w_s, b_s, w_out, w_ffn_in, w_ffn_out):
    b, n, _ = x.shape
    n_ctx = ctx.shape[1]
    assert n % ROW_TILE == 0 and n_ctx % SUB_ROWS == 0 and n_ctx <= ROW_TILE and b < MOD_ROWS
    cos_t, sin_t = _rope_tables(n)
    cos_c, sin_c = cos_t[:n_ctx], sin_t[:n_ctx]

    cvec = jnp.concatenate([c, c_ctx[None], jnp.zeros((MOD_ROWS - b - 1, D_MODEL), F32)], axis=0)
    mods = _modulation(cvec, w_mod, b_mod)
    mods = mods.reshape(DEPTH, MOD_ROWS, N_MOD, D_MODEL)
    mods = jnp.pad(mods, ((0, 0), (0, 0), (0, MOD_SLOTS - N_MOD), (0, 0)))

    w_in_b = w_in.astype(BF16)
    w_out_b = w_out.astype(BF16)
    w_gu = w_ffn_in.astype(BF16)
    w_down = w_ffn_out.astype(BF16)
    ws_cat = w_s.reshape(DEPTH, MLP_HEADS // 2, 2, CHUNK, CHUNK).transpose(0, 1, 3, 2, 4)
    ws_cat = ws_cat.reshape(DEPTH, MLP_HEADS // 2, CHUNK, 2 * CHUNK).astype(BF16)
    bs_full = jnp.repeat(b_s.transpose(0, 2, 1), HEAD_DIM, axis=2)
    gq_t = jnp.tile(g_q, (1, MXU_DIM // HEAD_DIM))[:, None]
    gk_t = jnp.tile(g_k, (1, LANES // HEAD_DIM))[:, None]

    xc = ctx
    for l in range(DEPTH):
        last = l == DEPTH - 1
        in_args = (g_pre_mix[l][None], w_in_b, gq_t[l], gk_t[l])
        mlp_args = (g_sg[l][None], b_sg[l][None], ws_cat[l], bs_full[l])
        ffn_args = (g_post_mix[l][None], g_pre_ffn[l][None], g_post_ffn[l][None], w_out_b, w_gu, w_down)

        qct, kc, vct, mlp_c = _inproj(xc, mods, l, b, *in_args, cos_c, sin_c, *mlp_args,
                                      tm=n_ctx, use_rope=False)
        qxt, kx, vxt, mlp_x = _inproj(x, mods, l, None, *in_args, cos_t, sin_t, *mlp_args,
                                      tm=2 * ROW_TILE, use_rope=True)
        bound = (BOUND_SLACK * LOG2E * math.sqrt(HEAD_DIM)) * jnp.max(jnp.abs(g_q[l])) * jnp.max(jnp.abs(g_k[l]))
        bound = bound.reshape(1)
        attn_x = _attention(bound, qxt, (kc, kx), (vct, vxt), tq=ROW_TILE, bk=KEY_BLOCK)
        x = _outffn(attn_x, mlp_x, x, mods, l, None, *ffn_args, tm=ROW_TILE)
        if not last:
            attn_c = _attention(bound, qct, (kc,), (vct,), tq=n_ctx, bk=KEY_BLOCK)
            xc = _outffn(attn_c, mlp_c, xc, mods, l, b, *ffn_args, tm=n_ctx)
    return x
```

```python
import functools
import math

import jax
import jax.numpy as jnp
from jax import lax
from jax.experimental import pallas as pl
from jax.experimental.pallas import tpu as pltpu

D_MODEL = 1024
DEPTH = 4
GRID_W = 64
HEAD_DIM = 64
ATTN_HEADS = 8
KV_HEADS = 2
GQA_GROUP = ATTN_HEADS // KV_HEADS
ATTN_WIDTH = ATTN_HEADS * HEAD_DIM
KV_WIDTH = KV_HEADS * HEAD_DIM
MLP_HEADS = 8
MLP_WIDTH = MLP_HEADS * HEAD_DIM
CHUNK = 128
IN_WIDTH = ATTN_WIDTH + 2 * KV_WIDTH + 2 * MLP_WIDTH
FFN_HIDDEN = 2816
N_MOD = 6
ROPE_THETA = 10000.0
ROPE_AXIS_DIM = HEAD_DIM // 2
EPS = 1e-6

LANES = 128
MXU_DIM = 256
MOD_ROWS = 16
MOD_SLOTS = 8
SUBLANES = 8
FFN_CHUNK = MXU_DIM
N_FFN_CHUNKS = FFN_HIDDEN // FFN_CHUNK
ROW_TILE = 4 * MXU_DIM
SUB_ROWS = MXU_DIM
KEY_BLOCK = MXU_DIM
PROJ_AHEAD = 2
FFN_AHEAD = 2
QK_AHEAD = 3
LOG2E = math.log2(math.e)
BOUND_SLACK = 1.02
SAFE_EXP2_RANGE = 120.0
VMEM_LIMIT = 56 * 1024 * 1024

F32 = jnp.float32
BF16 = jnp.bfloat16


def _const_spec(shape):
    zeros = (0,) * len(shape)
    return pl.BlockSpec(shape, lambda *_: zeros, pipeline_mode=pl.Buffered(1))


def _layer_spec(layer, shape):
    index = (layer,) + (0,) * len(shape)
    return pl.BlockSpec((1,) + tuple(shape), lambda *_: index, pipeline_mode=pl.Buffered(1))


def _rms(t, g):
    return t * lax.rsqrt(jnp.mean(t * t, axis=-1, keepdims=True) + EPS) * g


def _mod_kernel(c_ref, w_ref, b_ref, o_ref):
    cv = c_ref[...]
    act = (cv * jax.nn.sigmoid(cv)).astype(BF16)
    o_ref[0] = jnp.dot(act, w_ref[0].astype(BF16), preferred_element_type=F32) + b_ref[0]


def _modulation(cvec, w_mod, b_mod):
    tn = 1536
    n_out = N_MOD * D_MODEL
    return pl.pallas_call(
        _mod_kernel,
        grid=(DEPTH, n_out // tn),
        in_specs=[
            pl.BlockSpec((MOD_ROWS, D_MODEL), lambda l, j: (0, 0)),
            pl.BlockSpec((1, D_MODEL, tn), lambda l, j: (l, 0, j)),
            pl.BlockSpec((1, 1, tn), lambda l, j: (l, 0, j)),
        ],
        out_specs=pl.BlockSpec((1, MOD_ROWS, tn), lambda l, j: (l, 0, j)),
        out_shape=jax.ShapeDtypeStruct((DEPTH, MOD_ROWS, n_out), F32),
        compiler_params=pltpu.CompilerParams(
            dimension_semantics=("arbitrary", "arbitrary"), vmem_limit_bytes=VMEM_LIMIT),
        name="modulation",
    )(cvec, w_mod, b_mod.reshape(DEPTH, 1, n_out))


def _group_sum(t2, ones_bd):
    return jnp.dot(t2.astype(BF16), ones_bd, preferred_element_type=F32)


def _swap_halves16(t):
    lane = lax.broadcasted_iota(jnp.int32, t.shape, 1)
    return jnp.where((lane % 32) < 16, pltpu.roll(t, LANES - 16, 1), pltpu.roll(t, 16, 1))


def _inproj_kernel(x_ref, mod_ref, gpre_ref, w_ref, gq_ref, gk_ref, cos_ref, sin_ref,
                   gsg_ref, bsg_ref, ws_ref, bs_ref,
                   qt_ref, k_ref, vt_ref, mlp_ref, *, tm, use_rope):
    mod = mod_ref[0, 0]
    shift, scale = mod[0:1], mod[1:2]
    ts = min(tm, SUB_ROWS)

    ri = lax.broadcasted_iota(jnp.int32, (MXU_DIM, MXU_DIM), 0) // HEAD_DIM
    ci = lax.broadcasted_iota(jnp.int32, (MXU_DIM, MXU_DIM), 1) // HEAD_DIM
    ones_bd = jnp.where(ri == ci, 1.0, 0.0).astype(BF16)

    def project(r):
        h = (_rms(x_ref[0, r], gpre_ref[...]) * (1.0 + scale) + shift).astype(BF16)
        return jnp.dot(h, w_ref[0], preferred_element_type=F32)

    def head_norm(t, g):
        w = t.shape[1]
        ss = _group_sum(t * t, ones_bd[:w, :w])
        return t * lax.rsqrt(ss * (1.0 / HEAD_DIM) + EPS) * g

    def finish(r, p):
        def rope(t):
            if not use_rope:
                return t
            return t * cos_ref[r, :] + _swap_halves16(t) * sin_ref[r, :]

        q_scale = LOG2E / math.sqrt(HEAD_DIM)
        for half in range(ATTN_WIDTH // MXU_DIM):
            qn = head_norm(p[:, half * MXU_DIM:(half + 1) * MXU_DIM], gq_ref[...])
            for pair in range(MXU_DIM // LANES):
                tt = (rope(qn[:, pair * LANES:(pair + 1) * LANES]) * q_scale).T
                h0 = half * (MXU_DIM // HEAD_DIM) + pair * 2
                qt_ref[0, h0, :, r] = tt[:HEAD_DIM].astype(BF16)
                qt_ref[0, h0 + 1, :, r] = tt[HEAD_DIM:].astype(BF16)

        kn = rope(head_norm(p[:, ATTN_WIDTH:ATTN_WIDTH + KV_WIDTH], gk_ref[...]))
        k_ref[0, 0, r, :] = kn[:, :HEAD_DIM].astype(BF16)
        k_ref[0, 1, r, :] = pltpu.roll(kn, HEAD_DIM, 1)[:, :HEAD_DIM].astype(BF16)

        vvt = p[:, ATTN_WIDTH + KV_WIDTH:ATTN_WIDTH + 2 * KV_WIDTH].T
        for kvh in range(KV_HEADS):
            vt_ref[0, kvh, :, r] = vvt[kvh * HEAD_DIM:(kvh + 1) * HEAD_DIM].astype(BF16)

        z = p[:, ATTN_WIDTH + 2 * KV_WIDTH:]
        gc = math.sqrt(2.0 / math.pi)
        hz = 0.5 * z
        z = hz + hz * jnp.tanh(z * (gc + (gc * 0.044715) * (z * z)))
        u = z[:, :MLP_WIDTH]
        v2 = z[:, MLP_WIDTH:]
        mu = jnp.mean(v2, axis=-1, keepdims=True)
        vc = v2 - mu
        var = jnp.mean(vc * vc, axis=-1, keepdims=True)
        vln = vc * lax.rsqrt(var + EPS) * gsg_ref[...] + bsg_ref[...]
        lane_w = lax.broadcasted_iota(jnp.int32, vln.shape, 1)
        even_head = (lane_w % LANES) < HEAD_DIM
        v_even = jnp.where(even_head, vln, 0.0).astype(BF16)
        v_odd = jnp.where(even_head, 0.0, vln).astype(BF16)
        return u, v_even, v_odd

    def spatial_mix(r, u, v_even, v_odd):
        for ch in range(u.shape[0] // CHUNK):
            rows = slice(ch * CHUNK, (ch + 1) * CHUNK)
            out_rows = slice(r.start + ch * CHUNK, r.start + (ch + 1) * CHUNK)
            for jb in range(MLP_WIDTH // LANES):
                cols = slice(jb * LANES, (jb + 1) * LANES)
                rhs = jnp.concatenate([v_even[rows, cols], v_odd[rows, cols]], axis=0)
                s = jnp.dot(ws_ref[jb], rhs, preferred_element_type=F32)
                mlp_ref[0, out_rows, cols] = (u[rows, cols] * (s + bs_ref[:, cols])).astype(BF16)

    subs = [slice(i, i + ts) for i in range(0, tm, ts)]
    pending = [project(r) for r in subs[:PROJ_AHEAD]]
    for i, r in enumerate(subs):
        p_cur = pending.pop(0)
        if i + PROJ_AHEAD < len(subs):
            pending.append(project(subs[i + PROJ_AHEAD]))
        spatial_mix(r, *finish(r, p_cur))


def _inproj(x, mods, layer, stream_row, g_pre, w_in, gq_t, gk_t, cos_t, sin_t,
            g_sg, b_sg, ws_cat, bs_full, *, tm, use_rope):
    b, n, _ = x.shape
    if stream_row is None:
        mod_map = lambda i, j: (layer, i, 0, 0)
    else:
        mod_map = lambda i, j: (layer, stream_row, 0, 0)
    kern = functools.partial(_inproj_kernel, tm=tm, use_rope=use_rope)
    return pl.pallas_call(
        kern,
        grid=(b, n // tm),
        in_specs=[
            pl.BlockSpec((1, tm, D_MODEL), lambda i, j: (i, j, 0)),
            pl.BlockSpec((1, 1, MOD_SLOTS, D_MODEL), mod_map),
            _const_spec((1, D_MODEL)),
            _layer_spec(layer, (D_MODEL, IN_WIDTH)),
            _const_spec((1, MXU_DIM)),
            _const_spec((1, LANES)),
            pl.BlockSpec((tm, LANES), lambda i, j: (j, 0)),
            pl.BlockSpec((tm, LANES), lambda i, j: (j, 0)),
            _const_spec((1, MLP_WIDTH)),
            _const_spec((1, MLP_WIDTH)),
            _const_spec((MLP_WIDTH // LANES, CHUNK, 2 * CHUNK)),
            _const_spec((CHUNK, MLP_WIDTH)),
        ],
        out_specs=[
            pl.BlockSpec((1, ATTN_HEADS, HEAD_DIM, tm), lambda i, j: (i, 0, 0, j)),
            pl.BlockSpec((1, KV_HEADS, tm, HEAD_DIM), lambda i, j: (i, 0, j, 0)),
            pl.BlockSpec((1, KV_HEADS, HEAD_DIM, tm), lambda i, j: (i, 0, 0, j)),
            pl.BlockSpec((1, tm, MLP_WIDTH), lambda i, j: (i, j, 0)),
        ],
        out_shape=[
            jax.ShapeDtypeStruct((b, ATTN_HEADS, HEAD_DIM, n), BF16),
            jax.ShapeDtypeStruct((b, KV_HEADS, n, HEAD_DIM), BF16),
            jax.ShapeDtypeStruct((b, KV_HEADS, HEAD_DIM, n), BF16),
            jax.ShapeDtypeStruct((b, n, MLP_WIDTH), BF16),
        ],
        compiler_params=pltpu.CompilerParams(
            dimension_semantics=("arbitrary", "arbitrary"), vmem_limit_bytes=VMEM_LIMIT),
        name="inproj",
    )(x, mods, g_pre, w_in, gq_t, gk_t, cos_t, sin_t, g_sg, b_sg, ws_cat, bs_full)


def _attn_kernel(*refs, tq, seg_lens, bk):
    n_seg = len(seg_lens)
    bound_ref, qt_ref = refs[0], refs[1]
    segments = list(zip(refs[2:2 + n_seg], refs[2 + n_seg:2 + 2 * n_seg], seg_lens))
    o_ref = refs[2 + 2 * n_seg]
    ts = min(tq, SUB_ROWS)

    def key_sum(pf):
        return jnp.sum(pf.reshape(pf.shape[0] // SUBLANES, SUBLANES, pf.shape[1]), axis=0)

    def normalised(acc, den):
        pairs = []
        for pair in range(GQA_GROUP // 2):
            ot = [acc[g] / jnp.sum(den[g], axis=0, keepdims=True) for g in (2 * pair, 2 * pair + 1)]
            pairs.append(jnp.concatenate(ot, axis=0).T)
        return jnp.concatenate(pairs, axis=1).astype(BF16)

    def fixed_shift_path():
        subs = [slice(i, i + ts) for i in range(0, tq, ts)]
        acc = {(t, h): jnp.zeros((HEAD_DIM, ts), F32) for t in range(len(subs)) for h in range(GQA_GROUP)}
        den = {(t, h): jnp.zeros((SUBLANES, ts), F32) for t in range(len(subs)) for h in range(GQA_GROUP)}

        def finish(t, h, s, vt_ref, start, size, last):
            pf = jnp.exp2(s - bound_ref[0])
            vt_blk = vt_ref[0, 0, :, start:start + size]
            acc[t, h] = acc[t, h] + jnp.dot(vt_blk, pf.astype(BF16), preferred_element_type=F32)
            den[t, h] = den[t, h] + key_sum(pf)
            if last:
                o_ref[0, subs[t], :] = normalised([acc[t, g] for g in range(GQA_GROUP)],
                                                  [den[t, g] for g in range(GQA_GROUP)])

        blocks = [(k_ref, vt_ref, start, min(bk, seg - start))
                  for k_ref, vt_ref, seg in segments for start in range(0, seg, bk)]
        pending = []
        for t, r in enumerate(subs):
            for bi, (k_ref, vt_ref, start, size) in enumerate(blocks):
                for h in range(GQA_GROUP):
                    k_blk = k_ref[0, 0, start:start + size, :]
                    s = jnp.dot(k_blk, qt_ref[0, h, :, r], preferred_element_type=F32)
                    last = bi == len(blocks) - 1 and h == GQA_GROUP - 1
                    pending.append((t, h, s, vt_ref, start, size, last))
                    if len(pending) > QK_AHEAD:
                        finish(*pending.pop(0))
        while pending:
            finish(*pending.pop(0))

    def online_max_path():
        def sub_tile(t, carry):
            q0 = pl.multiple_of(t * ts, ts)
            qs = [qt_ref[0, h, :, pl.ds(q0, ts)] for h in range(GQA_GROUP)]
            state = tuple((jnp.full((1, ts), -jnp.inf, F32), jnp.zeros((SUBLANES, ts), F32),
                           jnp.zeros((HEAD_DIM, ts), F32)) for _ in range(GQA_GROUP))
            for k_ref, vt_ref, seg in segments:
                def key_block(j, st, k_ref=k_ref, vt_ref=vt_ref):
                    k0 = pl.multiple_of(j * bk, bk)
                    k_blk = k_ref[0, 0, pl.ds(k0, bk), :]
                    vt_blk = vt_ref[0, 0, :, pl.ds(k0, bk)]
                    new = []
                    for h in range(GQA_GROUP):
                        m, den, acc = st[h]
                        s = jnp.dot(k_blk, qs[h], preferred_element_type=F32)
                        m_new = jnp.maximum(m, jnp.max(s, axis=0, keepdims=True))
                        alpha = jnp.exp2(m - m_new)
                        pf = jnp.exp2(s - m_new)
                        new.append((m_new, alpha * den + key_sum(pf),
                                    alpha * acc + jnp.dot(vt_blk, pf.astype(BF16), preferred_element_type=F32)))
                    return tuple(new)
                state = lax.fori_loop(0, seg // bk, key_block, state)
            o_ref[0, pl.ds(q0, ts), :] = normalised([st[2] for st in state], [st[1] for st in state])
            return carry
        lax.fori_loop(0, tq // ts, sub_tile, 0)

    safe = 2.0 * bound_ref[0] <= SAFE_EXP2_RANGE
    pl.when(safe)(fixed_shift_path)
    pl.when(jnp.logical_not(safe))(online_max_path)


def _attention(bound, qt, ks, vts, *, tq, bk):
    b, _, _, n = qt.shape
    seg_lens = tuple(k.shape[2] for k in ks)
    assert all(s % bk == 0 for s in seg_lens)
    kern = functools.partial(_attn_kernel, tq=tq, seg_lens=seg_lens, bk=bk)
    in_specs = [pl.BlockSpec(memory_space=pltpu.SMEM),
                pl.BlockSpec((1, GQA_GROUP, HEAD_DIM, tq), lambda i, g, j: (i, g, 0, j))]
    in_specs += [pl.BlockSpec((1, 1, s, HEAD_DIM), lambda i, g, j: (i, g, 0, 0)) for s in seg_lens]
    in_specs += [pl.BlockSpec((1, 1, HEAD_DIM, s), lambda i, g, j: (i, g, 0, 0)) for s in seg_lens]
    return pl.pallas_call(
        kern,
        grid=(b, KV_HEADS, n // tq),
        in_specs=in_specs,
        out_specs=pl.BlockSpec((1, tq, GQA_GROUP * HEAD_DIM), lambda i, g, j: (i, j, g)),
        out_shape=jax.ShapeDtypeStruct((b, n, ATTN_WIDTH), BF16),
        compiler_params=pltpu.CompilerParams(
            dimension_semantics=("arbitrary", "arbitrary", "arbitrary"), vmem_limit_bytes=VMEM_LIMIT),
        name="attention",
    )(bound, qt, *ks, *vts)


def _outffn_kernel(attn_ref, mlp_ref, x_ref, mod_ref, gpm_ref, gpf_ref, gqf_ref,
                   wo_ref, wgu_ref, wd_ref, o_ref, *, tm):
    mod = mod_ref[0, 0]
    gate_mix, shift, scale, gate_ffn = mod[2:3], mod[3:4], mod[4:5], mod[5:6]

    ts = min(tm, SUB_ROWS)
    subs = [slice(i, i + ts) for i in range(0, tm, ts)]

    def out_proj(r):
        return (jnp.dot(attn_ref[0, r], wo_ref[0, :ATTN_WIDTH], preferred_element_type=F32)
                + jnp.dot(mlp_ref[0, r], wo_ref[0, ATTN_WIDTH:], preferred_element_type=F32))

    def ffn(h):
        def gate_up(ci):
            cols = slice(ci * FFN_CHUNK, (ci + 1) * FFN_CHUNK)
            up_cols = slice(FFN_HIDDEN + ci * FFN_CHUNK, FFN_HIDDEN + (ci + 1) * FFN_CHUNK)
            return (jnp.dot(h, wgu_ref[0, :, cols], preferred_element_type=F32),
                    jnp.dot(h, wgu_ref[0, :, up_cols], preferred_element_type=F32))

        pending = [gate_up(ci) for ci in range(FFN_AHEAD)]
        y = None
        for ci in range(N_FFN_CHUNKS):
            g, u = pending.pop(0)
            if ci + FFN_AHEAD < N_FFN_CHUNKS:
                pending.append(gate_up(ci + FFN_AHEAD))
            a = (g * jax.nn.sigmoid(g) * u).astype(BF16)
            d = jnp.dot(a, wd_ref[0, ci * FFN_CHUNK:(ci + 1) * FFN_CHUNK], preferred_element_type=F32)
            y = d if y is None else y + d
        return y

    out_next = out_proj(subs[0])
    for i, r in enumerate(subs):
        out = out_next
        if i + 1 < len(subs):
            out_next = out_proj(subs[i + 1])
        x1 = x_ref[0, r] + gate_mix * _rms(out, gpm_ref[...])
        h = (_rms(x1, gpf_ref[...]) * (1.0 + scale) + shift).astype(BF16)
        o_ref[0, r] = x1 + gate_ffn * _rms(ffn(h), gqf_ref[...])


def _outffn(attn, mlp, x, mods, layer, stream_row, g_post_mix, g_pre_ffn, g_post_ffn,
            w_out, w_gu, w_down, *, tm):
    b, n, _ = x.shape
    if stream_row is None:
        mod_map = lambda i, j: (layer, i, 0, 0)
    else:
        mod_map = lambda i, j: (layer, stream_row, 0, 0)
    row_spec = lambda w: pl.BlockSpec((1, tm, w), lambda i, j: (i, j, 0))
    return pl.pallas_call(
        functools.partial(_outffn_kernel, tm=tm),
        grid=(b, n // tm),
        in_specs=[
            row_spec(ATTN_WIDTH),
            row_spec(MLP_WIDTH),
            row_spec(D_MODEL),
            pl.BlockSpec((1, 1, MOD_SLOTS, D_MODEL), mod_map),
            _const_spec((1, D_MODEL)),
            _const_spec((1, D_MODEL)),
            _const_spec((1, D_MODEL)),
            _layer_spec(layer, (ATTN_WIDTH + MLP_WIDTH, D_MODEL)),
            _layer_spec(layer, (D_MODEL, 2 * FFN_HIDDEN)),
            _layer_spec(layer, (FFN_HIDDEN, D_MODEL)),
        ],
        out_specs=row_spec(D_MODEL),
        out_shape=jax.ShapeDtypeStruct((b, n, D_MODEL), F32),
        compiler_params=pltpu.CompilerParams(
            dimension_semantics=("arbitrary", "arbitrary"), vmem_limit_bytes=VMEM_LIMIT),
        name="outffn",
    )(attn, mlp, x, mods, g_post_mix, g_pre_ffn, g_post_ffn, w_out, w_gu, w_down)


def _rope_tables(n):
    rows = n // GRID_W
    pos_row = jnp.broadcast_to(jnp.arange(rows, dtype=F32)[:, None], (rows, GRID_W)).reshape(-1)
    pos_col = jnp.broadcast_to(jnp.arange(GRID_W, dtype=F32)[None, :], (rows, GRID_W)).reshape(-1)
    inv = ROPE_THETA ** (-jnp.arange(0, ROPE_AXIS_DIM, 2, dtype=F32) / ROPE_AXIS_DIM)
    ang_r = pos_row[:, None] * inv
    ang_c = pos_col[:, None] * inv
    cos64 = jnp.concatenate([jnp.cos(ang_r)] * 2 + [jnp.cos(ang_c)] * 2, axis=-1)
    sin64 = jnp.concatenate([-jnp.sin(ang_r), jnp.sin(ang_r), -jnp.sin(ang_c), jnp.sin(ang_c)], axis=-1)
    return jnp.tile(cos64, (1, 2)), jnp.tile(sin64, (1, 2))


def kernel(x, c, ctx, c_ctx, w_mod, b_mod, g_pre_mix, g_post_mix, g_pre_ffn, g_post_ffn,
           w_in, g_q, g_k, g_sg, b_sg, w_s, b_s, w_out, w_ffn_in, w_ffn_out):
    b, n, _ = x.shape
    n_ctx = ctx.shape[1]
    assert n % ROW_TILE == 0 and n_ctx % SUB_ROWS == 0 and n_ctx <= ROW_TILE and b < MOD_ROWS
    cos_t, sin_t = _rope_tables(n)
    cos_c, sin_c = cos_t[:n_ctx], sin_t[:n_ctx]

    cvec = jnp.concatenate([c, c_ctx[None], jnp.zeros((MOD_ROWS - b - 1, D_MODEL), F32)], axis=0)
    mods = _modulation(cvec, w_mod, b_mod)
    mods = mods.reshape(DEPTH, MOD_ROWS, N_MOD, D_MODEL)
    mods = jnp.pad(mods, ((0, 0), (0, 0), (0, MOD_SLOTS - N_MOD), (0, 0)))

    w_in_b = w_in.astype(BF16)
    w_out_b = w_out.astype(BF16)
    w_gu = w_ffn_in.astype(BF16)
    w_down = w_ffn_out.astype(BF16)
    ws_cat = w_s.reshape(DEPTH, MLP_HEADS // 2, 2, CHUNK, CHUNK).transpose(0, 1, 3, 2, 4)
    ws_cat = ws_cat.reshape(DEPTH, MLP_HEADS // 2, CHUNK, 2 * CHUNK).astype(BF16)
    bs_full = jnp.repeat(b_s.transpose(0, 2, 1), HEAD_DIM, axis=2)
    gq_t = jnp.tile(g_q, (1, MXU_DIM // HEAD_DIM))[:, None]
    gk_t = jnp.tile(g_k, (1, LANES // HEAD_DIM))[:, None]

    xc = ctx
    for l in range(DEPTH):
        last = l == DEPTH - 1
        in_args = (g_pre_mix[l][None], w_in_b, gq_t[l], gk_t[l])
        mlp_args = (g_sg[l][None], b_sg[l][None], ws_cat[l], bs_full[l])
        ffn_args = (g_post_mix[l][None], g_pre_ffn[l][None], g_post_ffn[l][None], w_out_b, w_gu, w_down)

        qct, kc, vct, mlp_c = _inproj(xc, mods, l, b, *in_args, cos_c, sin_c, *mlp_args,
                                      tm=n_ctx, use_rope=False)
        qxt, kx, vxt, mlp_x = _inproj(x, mods, l, None, *in_args, cos_t, sin_t, *mlp_args,
                                      tm=2 * ROW_TILE, use_rope=True)
        bound = (BOUND_SLACK * LOG2E * math.sqrt(HEAD_DIM)) * jnp.max(jnp.abs(g_q[l])) * jnp.max(jnp.abs(g_k[l]))
        bound = bound.reshape(1)
        attn_x = _attention(bound, qxt, (kc, kx), (vct, vxt), tq=ROW_TILE, bk=KEY_BLOCK)
        x = _outffn(attn_x, mlp_x, x, mods, l, None, *ffn_args, tm=ROW_TILE)
        if not last:
            attn_c = _attention(bound, qct, (kc,), (vct,), tq=n_ctx, bk=KEY_BLOCK)
            xc = _outffn(attn_c, mlp_c, xc, mods, l, b, *ffn_args, tm=n_ctx)
    return x
```

```python
import functools
import math

import jax
import jax.numpy as jnp
from jax import lax
from jax.experimental import pallas as pl
from jax.experimental.pallas import tpu as pltpu

D_MODEL = 1024
DEPTH = 4
GRID_W = 64
HEAD_DIM = 64
ATTN_HEADS = 8
KV_HEADS = 2
GQA_GROUP = ATTN_HEADS // KV_HEADS
ATTN_WIDTH = ATTN_HEADS * HEAD_DIM
KV_WIDTH = KV_HEADS * HEAD_DIM
MLP_HEADS = 8
MLP_WIDTH = MLP_HEADS * HEAD_DIM
CHUNK = 128
IN_WIDTH = ATTN_WIDTH + 2 * KV_WIDTH + 2 * MLP_WIDTH
FFN_HIDDEN = 2816
N_MOD = 6
ROPE_THETA = 10000.0
ROPE_AXIS_DIM = HEAD_DIM // 2
EPS = 1e-6

LANES = 128
MXU_DIM = 256
MOD_ROWS = 16
MOD_SLOTS = 8
SUBLANES = 8
FFN_CHUNK = MXU_DIM
N_FFN_CHUNKS = FFN_HIDDEN // FFN_CHUNK
ROW_TILE = 4 * MXU_DIM
SUB_ROWS = MXU_DIM
KEY_BLOCK = MXU_DIM
PROJ_AHEAD = 2
FFN_AHEAD = 2
QK_AHEAD = 4
LOG2E = math.log2(math.e)
BOUND_SLACK = 1.02
SAFE_EXP2_RANGE = 120.0
VMEM_LIMIT = 56 * 1024 * 1024

F32 = jnp.float32
BF16 = jnp.bfloat16


def _const_spec(shape):
    zeros = (0,) * len(shape)
    return pl.BlockSpec(shape, lambda *_: zeros, pipeline_mode=pl.Buffered(1))


def _layer_spec(layer, shape):
    index = (layer,) + (0,) * len(shape)
    return pl.BlockSpec((1,) + tuple(shape), lambda *_: index, pipeline_mode=pl.Buffered(1))


def _rms(t, g):
    return t * lax.rsqrt(jnp.mean(t * t, axis=-1, keepdims=True) + EPS) * g


def _mod_kernel(c_ref, w_ref, b_ref, o_ref):
    cv = c_ref[...]
    act = (cv * jax.nn.sigmoid(cv)).astype(BF16)
    o_ref[0] = jnp.dot(act, w_ref[0].astype(BF16), preferred_element_type=F32) + b_ref[0]


def _modulation(cvec, w_mod, b_mod):
    tn = 1536
    n_out = N_MOD * D_MODEL
    return pl.pallas_call(
        _mod_kernel,
        grid=(DEPTH, n_out // tn),
        in_specs=[
            pl.BlockSpec((MOD_ROWS, D_MODEL), lambda l, j: (0, 0)),
            pl.BlockSpec((1, D_MODEL, tn), lambda l, j: (l, 0, j)),
            pl.BlockSpec((1, 1, tn), lambda l, j: (l, 0, j)),
        ],
        out_specs=pl.BlockSpec((1, MOD_ROWS, tn), lambda l, j: (l, 0, j)),
        out_shape=jax.ShapeDtypeStruct((DEPTH, MOD_ROWS, n_out), F32),
        compiler_params=pltpu.CompilerParams(
            dimension_semantics=("arbitrary", "arbitrary"), vmem_limit_bytes=VMEM_LIMIT),
        name="modulation",
    )(cvec, w_mod, b_mod.reshape(DEPTH, 1, n_out))


def _group_sum(t2, ones_bd):
    return jnp.dot(t2.astype(BF16), ones_bd, preferred_element_type=F32)


def _swap_halves16(t):
    lane = lax.broadcasted_iota(jnp.int32, t.shape, 1)
    return jnp.where((lane % 32) < 16, pltpu.roll(t, LANES - 16, 1), pltpu.roll(t, 16, 1))


def _inproj_kernel(x_ref, mod_ref, gpre_ref, w_ref, gq_ref, gk_ref, cos_ref, sin_ref,
                   gsg_ref, bsg_ref, ws_ref, bs_ref,
                   qt_ref, k_ref, vt_ref, mlp_ref, *, tm, use_rope):
    mod = mod_ref[0, 0]
    shift, scale = mod[0:1], mod[1:2]
    ts = min(tm, SUB_ROWS)

    ri = lax.broadcasted_iota(jnp.int32, (MXU_DIM, MXU_DIM), 0) // HEAD_DIM
    ci = lax.broadcasted_iota(jnp.int32, (MXU_DIM, MXU_DIM), 1) // HEAD_DIM
    ones_bd = jnp.where(ri == ci, 1.0, 0.0).astype(BF16)

    def project(r):
        h = (_rms(x_ref[0, r], gpre_ref[...]) * (1.0 + scale) + shift).astype(BF16)
        return jnp.dot(h, w_ref[0], preferred_element_type=F32)

    def head_norm(t, g):
        w = t.shape[1]
        ss = _group_sum(t * t, ones_bd[:w, :w])
        return t * lax.rsqrt(ss * (1.0 / HEAD_DIM) + EPS) * g

    def finish(r, p):
        def rope(t):
            if not use_rope:
                return t
            return t * cos_ref[r, :] + _swap_halves16(t) * sin_ref[r, :]

        q_scale = LOG2E / math.sqrt(HEAD_DIM)
        for half in range(ATTN_WIDTH // MXU_DIM):
            qn = head_norm(p[:, half * MXU_DIM:(half + 1) * MXU_DIM], gq_ref[...])
            for pair in range(MXU_DIM // LANES):
                tt = (rope(qn[:, pair * LANES:(pair + 1) * LANES]) * q_scale).T
                h0 = half * (MXU_DIM // HEAD_DIM) + pair * 2
                qt_ref[0, h0, :, r] = tt[:HEAD_DIM].astype(BF16)
                qt_ref[0, h0 + 1, :, r] = tt[HEAD_DIM:].astype(BF16)

        kn = rope(head_norm(p[:, ATTN_WIDTH:ATTN_WIDTH + KV_WIDTH], gk_ref[...]))
        k_ref[0, 0, r, :] = kn[:, :HEAD_DIM].astype(BF16)
        k_ref[0, 1, r, :] = pltpu.roll(kn, HEAD_DIM, 1)[:, :HEAD_DIM].astype(BF16)

        vvt = p[:, ATTN_WIDTH + KV_WIDTH:ATTN_WIDTH + 2 * KV_WIDTH].T
        for kvh in range(KV_HEADS):
            vt_ref[0, kvh, :, r] = vvt[kvh * HEAD_DIM:(kvh + 1) * HEAD_DIM].astype(BF16)

        z = p[:, ATTN_WIDTH + 2 * KV_WIDTH:]
        gc = math.sqrt(2.0 / math.pi)
        hz = 0.5 * z
        z = hz + hz * jnp.tanh(z * (gc + (gc * 0.044715) * (z * z)))
        u = z[:, :MLP_WIDTH]
        v2 = z[:, MLP_WIDTH:]
        mu = jnp.mean(v2, axis=-1, keepdims=True)
        vc = v2 - mu
        var = jnp.mean(vc * vc, axis=-1, keepdims=True)
        vln = vc * lax.rsqrt(var + EPS) * gsg_ref[...] + bsg_ref[...]
        lane_w = lax.broadcasted_iota(jnp.int32, vln.shape, 1)
        even_head = (lane_w % LANES) < HEAD_DIM
        v_even = jnp.where(even_head, vln, 0.0).astype(BF16)
        v_odd = jnp.where(even_head, 0.0, vln).astype(BF16)
        return u, v_even, v_odd

    def spatial_mix(r, u, v_even, v_odd):
        for ch in range(u.shape[0] // CHUNK):
            rows = slice(ch * CHUNK, (ch + 1) * CHUNK)
            out_rows = slice(r.start + ch * CHUNK, r.start + (ch + 1) * CHUNK)
            for jb in range(MLP_WIDTH // LANES):
                cols = slice(jb * LANES, (jb + 1) * LANES)
                rhs = jnp.concatenate([v_even[rows, cols], v_odd[rows, cols]], axis=0)
                s = jnp.dot(ws_ref[jb], rhs, preferred_element_type=F32)
                mlp_ref[0, out_rows, cols] = (u[rows, cols] * (s + bs_ref[:, cols])).astype(BF16)

    subs = [slice(i, i + ts) for i in range(0, tm, ts)]
    pending = [project(r) for r in subs[:PROJ_AHEAD]]
    for i, r in enumerate(subs):
        p_cur = pending.pop(0)
        if i + PROJ_AHEAD < len(subs):
            pending.append(project(subs[i + PROJ_AHEAD]))
        spatial_mix(r, *finish(r, p_cur))


def _inproj(x, mods, layer, stream_row, g_pre, w_in, gq_t, gk_t, cos_t, sin_t,
            g_sg, b_sg, ws_cat, bs_full, *, tm, use_rope):
    b, n, _ = x.shape
    if stream_row is None:
        mod_map = lambda i, j: (layer, i, 0, 0)
    else:
        mod_map = lambda i, j: (layer, stream_row, 0, 0)
    kern = functools.partial(_inproj_kernel, tm=tm, use_rope=use_rope)
    return pl.pallas_call(
        kern,
        grid=(b, n // tm),
        in_specs=[
            pl.BlockSpec((1, tm, D_MODEL), lambda i, j: (i, j, 0)),
            pl.BlockSpec((1, 1, MOD_SLOTS, D_MODEL), mod_map),
            _const_spec((1, D_MODEL)),
            _layer_spec(layer, (D_MODEL, IN_WIDTH)),
            _const_spec((1, MXU_DIM)),
            _const_spec((1, LANES)),
            pl.BlockSpec((tm, LANES), lambda i, j: (j, 0)),
            pl.BlockSpec((tm, LANES), lambda i, j: (j, 0)),
            _const_spec((1, MLP_WIDTH)),
            _const_spec((1, MLP_WIDTH)),
            _const_spec((MLP_WIDTH // LANES, CHUNK, 2 * CHUNK)),
            _const_spec((CHUNK, MLP_WIDTH)),
        ],
        out_specs=[
            pl.BlockSpec((1, ATTN_HEADS, HEAD_DIM, tm), lambda i, j: (i, 0, 0, j)),
            pl.BlockSpec((1, KV_HEADS, tm, HEAD_DIM), lambda i, j: (i, 0, j, 0)),
            pl.BlockSpec((1, KV_HEADS, HEAD_DIM, tm), lambda i, j: (i, 0, 0, j)),
            pl.BlockSpec((1, tm, MLP_WIDTH), lambda i, j: (i, j, 0)),
        ],
        out_shape=[
            jax.ShapeDtypeStruct((b, ATTN_HEADS, HEAD_DIM, n), BF16),
            jax.ShapeDtypeStruct((b, KV_HEADS, n, HEAD_DIM), BF16),
            jax.ShapeDtypeStruct((b, KV_HEADS, HEAD_DIM, n), BF16),
            jax.ShapeDtypeStruct((b, n, MLP_WIDTH), BF16),
        ],
        compiler_params=pltpu.CompilerParams(
            dimension_semantics=("arbitrary", "arbitrary"), vmem_limit_bytes=VMEM_LIMIT),
        name="inproj",
    )(x, mods, g_pre, w_in, gq_t, gk_t, cos_t, sin_t, g_sg, b_sg, ws_cat, bs_full)


def _attn_kernel(*refs, tq, seg_lens, bk):
    n_seg = len(seg_lens)
    bound_ref, qt_ref = refs[0], refs[1]
    segments = list(zip(refs[2:2 + n_seg], refs[2 + n_seg:2 + 2 * n_seg], seg_lens))
    o_ref = refs[2 + 2 * n_seg]
    ts = min(tq, SUB_ROWS)

    def key_sum(pf):
        return jnp.sum(pf.reshape(pf.shape[0] // SUBLANES, SUBLANES, pf.shape[1]), axis=0)

    def normalised(acc, den):
        pairs = []
        for pair in range(GQA_GROUP // 2):
            ot = [acc[g] / jnp.sum(den[g], axis=0, keepdims=True) for g in (2 * pair, 2 * pair + 1)]
            pairs.append(jnp.concatenate(ot, axis=0).T)
        return jnp.concatenate(pairs, axis=1).astype(BF16)

    def fixed_shift_path():
        subs = [slice(i, i + ts) for i in range(0, tq, ts)]
        acc = {(t, h): jnp.zeros((HEAD_DIM, ts), F32) for t in range(len(subs)) for h in range(GQA_GROUP)}
        den = {(t, h): jnp.zeros((SUBLANES, ts), F32) for t in range(len(subs)) for h in range(GQA_GROUP)}

        def finish(t, h, s, vt_ref, start, size, last):
            pf = jnp.exp2(s - bound_ref[0])
            vt_blk = vt_ref[0, 0, :, start:start + size]
            acc[t, h] = acc[t, h] + jnp.dot(vt_blk, pf.astype(BF16), preferred_element_type=F32)
            den[t, h] = den[t, h] + key_sum(pf)
            if last:
                o_ref[0, subs[t], :] = normalised([acc[t, g] for g in range(GQA_GROUP)],
                                                  [den[t, g] for g in range(GQA_GROUP)])

        blocks = [(k_ref, vt_ref, start, min(bk, seg - start))
                  for k_ref, vt_ref, seg in segments for start in range(0, seg, bk)]
        pending = []
        for t, r in enumerate(subs):
            for bi, (k_ref, vt_ref, start, size) in enumerate(blocks):
                for h in range(GQA_GROUP):
                    k_blk = k_ref[0, 0, start:start + size, :]
                    s = jnp.dot(k_blk, qt_ref[0, h, :, r], preferred_element_type=F32)
                    last = bi == len(blocks) - 1 and h == GQA_GROUP - 1
                    pending.append((t, h, s, vt_ref, start, size, last))
                    if len(pending) > QK_AHEAD:
                        finish(*pending.pop(0))
        while pending:
            finish(*pending.pop(0))

    def online_max_path():
        def sub_tile(t, carry):
            q0 = pl.multiple_of(t * ts, ts)
            qs = [qt_ref[0, h, :, pl.ds(q0, ts)] for h in range(GQA_GROUP)]
            state = tuple((jnp.full((1, ts), -jnp.inf, F32), jnp.zeros((SUBLANES, ts), F32),
                           jnp.zeros((HEAD_DIM, ts), F32)) for _ in range(GQA_GROUP))
            for k_ref, vt_ref, seg in segments:
                def key_block(j, st, k_ref=k_ref, vt_ref=vt_ref):
                    k0 = pl.multiple_of(j * bk, bk)
                    k_blk = k_ref[0, 0, pl.ds(k0, bk), :]
                    vt_blk = vt_ref[0, 0, :, pl.ds(k0, bk)]
                    new = []
                    for h in range(GQA_GROUP):
                        m, den, acc = st[h]
                        s = jnp.dot(k_blk, qs[h], preferred_element_type=F32)
                        m_new = jnp.maximum(m, jnp.max(s, axis=0, keepdims=True))
                        alpha = jnp.exp2(m - m_new)
                        pf = jnp.exp2(s - m_new)
                        new.append((m_new, alpha * den + key_sum(pf),
                                    alpha * acc + jnp.dot(vt_blk, pf.astype(BF16), preferred_element_type=F32)))
                    return tuple(new)
                state = lax.fori_loop(0, seg // bk, key_block, state)
            o_ref[0, pl.ds(q0, ts), :] = normalised([st[2] for st in state], [st[1] for st in state])
            return carry
        lax.fori_loop(0, tq // ts, sub_tile, 0)

    safe = 2.0 * bound_ref[0] <= SAFE_EXP2_RANGE
    pl.when(safe)(fixed_shift_path)
    pl.when(jnp.logical_not(safe))(online_max_path)


def _attention(bound, qt, ks, vts, *, tq, bk):
    b, _, _, n = qt.shape
    seg_lens = tuple(k.shape[2] for k in ks)
    assert all(s % bk == 0 for s in seg_lens)
    kern = functools.partial(_attn_kernel, tq=tq, seg_lens=seg_lens, bk=bk)
    in_specs = [pl.BlockSpec(memory_space=pltpu.SMEM),
                pl.BlockSpec((1, GQA_GROUP, HEAD_DIM, tq), lambda i, g, j: (i, g, 0, j))]
    in_specs += [pl.BlockSpec((1, 1, s, HEAD_DIM), lambda i, g, j: (i, g, 0, 0)) for s in seg_lens]
    in_specs += [pl.BlockSpec((1, 1, HEAD_DIM, s), lambda i, g, j: (i, g, 0, 0)) for s in seg_lens]
    return pl.pallas_call(
        kern,
        grid=(b, KV_HEADS, n // tq),
        in_specs=in_specs,
        out_specs=pl.BlockSpec((1, tq, GQA_GROUP * HEAD_DIM), lambda i, g, j: (i, j, g)),
        out_shape=jax.ShapeDtypeStruct((b, n, ATTN_WIDTH), BF16),
        compiler_params=pltpu.CompilerParams(
            dimension_semantics=("arbitrary", "arbitrary", "arbitrary"), vmem_limit_bytes=VMEM_LIMIT),
        name="attention",
    )(bound, qt, *ks, *vts)


def _outffn_kernel(attn_ref, mlp_ref, x_ref, mod_ref, gpm_ref, gpf_ref, gqf_ref,
                   wo_ref, wgu_ref, wd_ref, o_ref, *, tm):
    mod = mod_ref[0, 0]
    gate_mix, shift, scale, gate_ffn = mod[2:3], mod[3:4], mod[4:5], mod[5:6]

    ts = min(tm, 2 * SUB_ROWS)
    subs = [slice(i, i + ts) for i in range(0, tm, ts)]

    def out_proj(r):
        return (jnp.dot(attn_ref[0, r], wo_ref[0, :ATTN_WIDTH], preferred_element_type=F32)
                + jnp.dot(mlp_ref[0, r], wo_ref[0, ATTN_WIDTH:], preferred_element_type=F32))

    def ffn(h):
        def gate_up(ci):
            cols = slice(ci * FFN_CHUNK, (ci + 1) * FFN_CHUNK)
            up_cols = slice(FFN_HIDDEN + ci * FFN_CHUNK, FFN_HIDDEN + (ci + 1) * FFN_CHUNK)
            return (jnp.dot(h, wgu_ref[0, :, cols], preferred_element_type=F32),
                    jnp.dot(h, wgu_ref[0, :, up_cols], preferred_element_type=F32))

        pending = [gate_up(ci) for ci in range(FFN_AHEAD)]
        y = None
        for ci in range(N_FFN_CHUNKS):
            g, u = pending.pop(0)
            if ci + FFN_AHEAD < N_FFN_CHUNKS:
                pending.append(gate_up(ci + FFN_AHEAD))
            a = (g * jax.nn.sigmoid(g) * u).astype(BF16)
            d = jnp.dot(a, wd_ref[0, ci * FFN_CHUNK:(ci + 1) * FFN_CHUNK], preferred_element_type=F32)
            y = d if y is None else y + d
        return y

    out_next = out_proj(subs[0])
    for i, r in enumerate(subs):
        out = out_next
        if i + 1 < len(subs):
            out_next = out_proj(subs[i + 1])
        x1 = x_ref[0, r] + gate_mix * _rms(out, gpm_ref[...])
        h = (_rms(x1, gpf_ref[...]) * (1.0 + scale) + shift).astype(BF16)
        o_ref[0, r] = x1 + gate_ffn * _rms(ffn(h), gqf_ref[...])


def _outffn(attn, mlp, x, mods, layer, stream_row, g_post_mix, g_pre_ffn, g_post_ffn,
            w_out, w_gu, w_down, *, tm):
    b, n, _ = x.shape
    if stream_row is None:
        mod_map = lambda i, j: (layer, i, 0, 0)
    else:
        mod_map = lambda i, j: (layer, stream_row, 0, 0)
    row_spec = lambda w: pl.BlockSpec((1, tm, w), lambda i, j: (i, j, 0))
    return pl.pallas_call(
        functools.partial(_outffn_kernel, tm=tm),
        grid=(b, n // tm),
        in_specs=[
            row_spec(ATTN_WIDTH),
            row_spec(MLP_WIDTH),
            row_spec(D_MODEL),
            pl.BlockSpec((1, 1, MOD_SLOTS, D_MODEL), mod_map),
            _const_spec((1, D_MODEL)),
            _const_spec((1, D_MODEL)),
            _const_spec((1, D_MODEL)),
            _layer_spec(layer, (ATTN_WIDTH + MLP_WIDTH, D_MODEL)),
            _layer_spec(layer, (D_MODEL, 2 * FFN_HIDDEN)),
            _layer_spec(layer, (FFN_HIDDEN, D_MODEL)),
        ],
        out_specs=row_spec(D_MODEL),
        out_shape=jax.ShapeDtypeStruct((b, n, D_MODEL), F32),
        compiler_params=pltpu.CompilerParams(
            dimension_semantics=("arbitrary", "arbitrary"), vmem_limit_bytes=VMEM_LIMIT),
        name="outffn",
    )(attn, mlp, x, mods, g_post_mix, g_pre_ffn, g_post_ffn, w_out, w_gu, w_down)


def _rope_tables(n):
    rows = n // GRID_W
    pos_row = jnp.broadcast_to(jnp.arange(rows, dtype=F32)[:, None], (rows, GRID_W)).reshape(-1)
    pos_col = jnp.broadcast_to(jnp.arange(GRID_W, dtype=F32)[None, :], (rows, GRID_W)).reshape(-1)
    inv = ROPE_THETA ** (-jnp.arange(0, ROPE_AXIS_DIM, 2, dtype=F32) / ROPE_AXIS_DIM)
    ang_r = pos_row[:, None] * inv
    ang_c = pos_col[:, None] * inv
    cos64 = jnp.concatenate([jnp.cos(ang_r)] * 2 + [jnp.cos(ang_c)] * 2, axis=-1)
    sin64 = jnp.concatenate([-jnp.sin(ang_r), jnp.sin(ang_r), -jnp.sin(ang_c), jnp.sin(ang_c)], axis=-1)
    return jnp.tile(cos64, (1, 2)), jnp.tile(sin64, (1, 2))


def kernel(x, c, ctx, c_ctx, w_mod, b_mod, g_pre_mix, g_post_mix, g_pre_ffn, g_post_ffn,
           w_in, g_q, g_k, g_sg, b_sg, w_s, b_s, w_out, w_ffn_in, w_ffn_out):
    b, n, _ = x.shape
    n_ctx = ctx.shape[1]
    assert n % ROW_TILE == 0 and n_ctx % SUB_ROWS == 0 and n_ctx <= ROW_TILE and b < MOD_ROWS
    cos_t, sin_t = _rope_tables(n)
    cos_c, sin_c = cos_t[:n_ctx], sin_t[:n_ctx]

    cvec = jnp.concatenate([c, c_ctx[None], jnp.zeros((MOD_ROWS - b - 1, D_MODEL), F32)], axis=0)
    mods = _modulation(cvec, w_mod, b_mod)
    mods = mods.reshape(DEPTH, MOD_ROWS, N_MOD, D_MODEL)
    mods = jnp.pad(mods, ((0, 0), (0, 0), (0, MOD_SLOTS - N_MOD), (0, 0)))

    w_in_b = w_in.astype(BF16)
    w_out_b = w_out.astype(BF16)
    w_gu = w_ffn_in.astype(BF16)
    w_down = w_ffn_out.astype(BF16)
    ws_cat = w_s.reshape(DEPTH, MLP_HEADS // 2, 2, CHUNK, CHUNK).transpose(0, 1, 3, 2, 4)
    ws_cat = ws_cat.reshape(DEPTH, MLP_HEADS // 2, CHUNK, 2 * CHUNK).astype(BF16)
    bs_full = jnp.repeat(b_s.transpose(0, 2, 1), HEAD_DIM, axis=2)
    gq_t = jnp.tile(g_q, (1, MXU_DIM // HEAD_DIM))[:, None]
    gk_t = jnp.tile(g_k, (1, LANES // HEAD_DIM))[:, None]

    xc = ctx
    for l in range(DEPTH):
        last = l == DEPTH - 1
        in_args = (g_pre_mix[l][None], w_in_b, gq_t[l], gk_t[l])
        mlp_args = (g_sg[l][None], b_sg[l][None], ws_cat[l], bs_full[l])
        ffn_args = (g_post_mix[l][None], g_pre_ffn[l][None], g_post_ffn[l][None], w_out_b, w_gu, w_down)

        qct, kc, vct, mlp_c = _inproj(xc, mods, l, b, *in_args, cos_c, sin_c, *mlp_args,
                                      tm=n_ctx, use_rope=False)
        qxt, kx, vxt, mlp_x = _inproj(x, mods, l, None, *in_args, cos_t, sin_t, *mlp_args,
                                      tm=2 * ROW_TILE, use_rope=True)
        bound = (BOUND_SLACK * LOG2E * math.sqrt(HEAD_DIM)) * jnp.max(jnp.abs(g_q[l])) * jnp.max(jnp.abs(g_k[l]))
        bound = bound.reshape(1)
        attn_x = _attention(bound, qxt, (kc, kx), (vct, vxt), tq=ROW_TILE, bk=KEY_BLOCK)
        x = _outffn(attn_x, mlp_x, x, mods, l, None, *ffn_args, tm=ROW_TILE)
        if not last:
            attn_c = _attention(bound, qct, (kc,), (vct,), tq=n_ctx, bk=KEY_BLOCK)
            xc = _outffn(attn_c, mlp_c, xc, mods, l, b, *ffn_args, tm=n_ctx)
    return x
```

```python
import functools
import math

import jax
import jax.numpy as jnp
from jax import lax
from jax.experimental import pallas as pl
from jax.experimental.pallas import tpu as pltpu

D_MODEL = 1024
DEPTH = 4
GRID_W = 64
HEAD_DIM = 64
ATTN_HEADS = 8
KV_HEADS = 2
GQA_GROUP = ATTN_HEADS // KV_HEADS
ATTN_WIDTH = ATTN_HEADS * HEAD_DIM
KV_WIDTH = KV_HEADS * HEAD_DIM
MLP_HEADS = 8
MLP_WIDTH = MLP_HEADS * HEAD_DIM
CHUNK = 128
IN_WIDTH = ATTN_WIDTH + 2 * KV_WIDTH + 2 * MLP_WIDTH
FFN_HIDDEN = 2816
N_MOD = 6
ROPE_THETA = 10000.0
ROPE_AXIS_DIM = HEAD_DIM // 2
EPS = 1e-6

LANES = 128
MXU_DIM = 256
MOD_ROWS = 16
MOD_SLOTS = 8
SUBLANES = 8
FFN_CHUNK = MXU_DIM
N_FFN_CHUNKS = FFN_HIDDEN // FFN_CHUNK
ROW_TILE = 4 * MXU_DIM
SUB_ROWS = MXU_DIM
KEY_BLOCK = MXU_DIM
PROJ_AHEAD = 2
FFN_AHEAD = 1
QK_AHEAD = 4
LOG2E = math.log2(math.e)
BOUND_SLACK = 1.02
SAFE_EXP2_RANGE = 120.0
VMEM_LIMIT = 56 * 1024 * 1024

F32 = jnp.float32
BF16 = jnp.bfloat16


def _const_spec(shape):
    zeros = (0,) * len(shape)
    return pl.BlockSpec(shape, lambda *_: zeros, pipeline_mode=pl.Buffered(1))


def _layer_spec(layer, shape):
    index = (layer,) + (0,) * len(shape)
    return pl.BlockSpec((1,) + tuple(shape), lambda *_: index, pipeline_mode=pl.Buffered(1))


def _rms(t, g):
    return t * lax.rsqrt(jnp.mean(t * t, axis=-1, keepdims=True) + EPS) * g


def _mod_kernel(c_ref, w_ref, b_ref, o_ref):
    cv = c_ref[...]
    act = (cv * jax.nn.sigmoid(cv)).astype(BF16)
    o_ref[0] = jnp.dot(act, w_ref[0].astype(BF16), preferred_element_type=F32) + b_ref[0]


def _modulation(cvec, w_mod, b_mod):
    tn = 1536
    n_out = N_MOD * D_MODEL
    return pl.pallas_call(
        _mod_kernel,
        grid=(DEPTH, n_out // tn),
        in_specs=[
            pl.BlockSpec((MOD_ROWS, D_MODEL), lambda l, j: (0, 0)),
            pl.BlockSpec((1, D_MODEL, tn), lambda l, j: (l, 0, j)),
            pl.BlockSpec((1, 1, tn), lambda l, j: (l, 0, j)),
        ],
        out_specs=pl.BlockSpec((1, MOD_ROWS, tn), lambda l, j: (l, 0, j)),
        out_shape=jax.ShapeDtypeStruct((DEPTH, MOD_ROWS, n_out), F32),
        compiler_params=pltpu.CompilerParams(
            dimension_semantics=("arbitrary", "arbitrary"), vmem_limit_bytes=VMEM_LIMIT),
        name="modulation",
    )(cvec, w_mod, b_mod.reshape(DEPTH, 1, n_out))


def _group_sum(t2, ones_bd):
    return jnp.dot(t2.astype(BF16), ones_bd, preferred_element_type=F32)


def _swap_halves16(t):
    lane = lax.broadcasted_iota(jnp.int32, t.shape, 1)
    return jnp.where((lane % 32) < 16, pltpu.roll(t, LANES - 16, 1), pltpu.roll(t, 16, 1))


def _inproj_kernel(x_ref, mod_ref, gpre_ref, w_ref, gq_ref, gk_ref, cos_ref, sin_ref,
                   gsg_ref, bsg_ref, ws_ref, bs_ref,
                   qt_ref, k_ref, vt_ref, mlp_ref, *, tm, use_rope):
    mod = mod_ref[0, 0]
    shift, scale = mod[0:1], mod[1:2]
    ts = min(tm, SUB_ROWS)

    ri = lax.broadcasted_iota(jnp.int32, (MXU_DIM, MXU_DIM), 0) // HEAD_DIM
    ci = lax.broadcasted_iota(jnp.int32, (MXU_DIM, MXU_DIM), 1) // HEAD_DIM
    ones_bd = jnp.where(ri == ci, 1.0, 0.0).astype(BF16)

    def project(r):
        h = (_rms(x_ref[0, r], gpre_ref[...]) * (1.0 + scale) + shift).astype(BF16)
        return jnp.dot(h, w_ref[0], preferred_element_type=F32)

    def head_norm(t, g):
        w = t.shape[1]
        ss = _group_sum(t * t, ones_bd[:w, :w])
        return t * lax.rsqrt(ss * (1.0 / HEAD_DIM) + EPS) * g

    def finish(r, p):
        def rope(t):
            if not use_rope:
                return t
            return t * cos_ref[r, :] + _swap_halves16(t) * sin_ref[r, :]

        q_scale = LOG2E / math.sqrt(HEAD_DIM)
        for half in range(ATTN_WIDTH // MXU_DIM):
            qn = head_norm(p[:, half * MXU_DIM:(half + 1) * MXU_DIM], gq_ref[...])
            for pair in range(MXU_DIM // LANES):
                tt = (rope(qn[:, pair * LANES:(pair + 1) * LANES]) * q_scale).T
                h0 = half * (MXU_DIM // HEAD_DIM) + pair * 2
                qt_ref[0, h0, :, r] = tt[:HEAD_DIM].astype(BF16)
                qt_ref[0, h0 + 1, :, r] = tt[HEAD_DIM:].astype(BF16)

        kn = rope(head_norm(p[:, ATTN_WIDTH:ATTN_WIDTH + KV_WIDTH], gk_ref[...]))
        k_ref[0, 0, r, :] = kn[:, :HEAD_DIM].astype(BF16)
        k_ref[0, 1, r, :] = pltpu.roll(kn, HEAD_DIM, 1)[:, :HEAD_DIM].astype(BF16)

        vvt = p[:, ATTN_WIDTH + KV_WIDTH:ATTN_WIDTH + 2 * KV_WIDTH].T
        for kvh in range(KV_HEADS):
            vt_ref[0, kvh, :, r] = vvt[kvh * HEAD_DIM:(kvh + 1) * HEAD_DIM].astype(BF16)

        z = p[:, ATTN_WIDTH + 2 * KV_WIDTH:]
        gc = math.sqrt(2.0 / math.pi)
        hz = 0.5 * z
        z = hz + hz * jnp.tanh(z * (gc + (gc * 0.044715) * (z * z)))
        u = z[:, :MLP_WIDTH]
        v2 = z[:, MLP_WIDTH:]
        mu = jnp.mean(v2, axis=-1, keepdims=True)
        vc = v2 - mu
        var = jnp.mean(vc * vc, axis=-1, keepdims=True)
        vln = vc * lax.rsqrt(var + EPS) * gsg_ref[...] + bsg_ref[...]
        lane_w = lax.broadcasted_iota(jnp.int32, vln.shape, 1)
        even_head = (lane_w % LANES) < HEAD_DIM
        v_even = jnp.where(even_head, vln, 0.0).astype(BF16)
        v_odd = jnp.where(even_head, 0.0, vln).astype(BF16)
        return u, v_even, v_odd

    def spatial_mix(r, u, v_even, v_odd):
        for ch in range(u.shape[0] // CHUNK):
            rows = slice(ch * CHUNK, (ch + 1) * CHUNK)
            out_rows = slice(r.start + ch * CHUNK, r.start + (ch + 1) * CHUNK)
            for jb in range(MLP_WIDTH // LANES):
                cols = slice(jb * LANES, (jb + 1) * LANES)
                rhs = jnp.concatenate([v_even[rows, cols], v_odd[rows, cols]], axis=0)
                s = jnp.dot(ws_ref[jb], rhs, preferred_element_type=F32)
                mlp_ref[0, out_rows, cols] = (u[rows, cols] * (s + bs_ref[:, cols])).astype(BF16)

    subs = [slice(i, i + ts) for i in range(0, tm, ts)]
    pending = [project(r) for r in subs[:PROJ_AHEAD]]
    for i, r in enumerate(subs):
        p_cur = pending.pop(0)
        if i + PROJ_AHEAD < len(subs):
            pending.append(project(subs[i + PROJ_AHEAD]))
        spatial_mix(r, *finish(r, p_cur))


def _inproj(x, mods, layer, stream_row, g_pre, w_in, gq_t, gk_t, cos_t, sin_t,
            g_sg, b_sg, ws_cat, bs_full, *, tm, use_rope):
    b, n, _ = x.shape
    if stream_row is None:
        mod_map = lambda i, j: (layer, i, 0, 0)
    else:
        mod_map = lambda i, j: (layer, stream_row, 0, 0)
    kern = functools.partial(_inproj_kernel, tm=tm, use_rope=use_rope)
    return pl.pallas_call(
        kern,
        grid=(b, n // tm),
        in_specs=[
            pl.BlockSpec((1, tm, D_MODEL), lambda i, j: (i, j, 0)),
            pl.BlockSpec((1, 1, MOD_SLOTS, D_MODEL), mod_map),
            _const_spec((1, D_MODEL)),
            _layer_spec(layer, (D_MODEL, IN_WIDTH)),
            _const_spec((1, MXU_DIM)),
            _const_spec((1, LANES)),
            pl.BlockSpec((tm, LANES), lambda i, j: (j, 0)),
            pl.BlockSpec((tm, LANES), lambda i, j: (j, 0)),
            _const_spec((1, MLP_WIDTH)),
            _const_spec((1, MLP_WIDTH)),
            _const_spec((MLP_WIDTH // LANES, CHUNK, 2 * CHUNK)),
            _const_spec((CHUNK, MLP_WIDTH)),
        ],
        out_specs=[
            pl.BlockSpec((1, ATTN_HEADS, HEAD_DIM, tm), lambda i, j: (i, 0, 0, j)),
            pl.BlockSpec((1, KV_HEADS, tm, HEAD_DIM), lambda i, j: (i, 0, j, 0)),
            pl.BlockSpec((1, KV_HEADS, HEAD_DIM, tm), lambda i, j: (i, 0, 0, j)),
            pl.BlockSpec((1, tm, MLP_WIDTH), lambda i, j: (i, j, 0)),
        ],
        out_shape=[
            jax.ShapeDtypeStruct((b, ATTN_HEADS, HEAD_DIM, n), BF16),
            jax.ShapeDtypeStruct((b, KV_HEADS, n, HEAD_DIM), BF16),
            jax.ShapeDtypeStruct((b, KV_HEADS, HEAD_DIM, n), BF16),
            jax.ShapeDtypeStruct((b, n, MLP_WIDTH), BF16),
        ],
        compiler_params=pltpu.CompilerParams(
            dimension_semantics=("arbitrary", "arbitrary"), vmem_limit_bytes=VMEM_LIMIT),
        name="inproj",
    )(x, mods, g_pre, w_in, gq_t, gk_t, cos_t, sin_t, g_sg, b_sg, ws_cat, bs_full)


def _attn_kernel(*refs, tq, seg_lens, bk):
    n_seg = len(seg_lens)
    bound_ref, qt_ref = refs[0], refs[1]
    segments = list(zip(refs[2:2 + n_seg], refs[2 + n_seg:2 + 2 * n_seg], seg_lens))
    o_ref = refs[2 + 2 * n_seg]
    ts = min(tq, SUB_ROWS)

    def key_sum(pf):
        return jnp.sum(pf.reshape(pf.shape[0] // SUBLANES, SUBLANES, pf.shape[1]), axis=0)

    def normalised(acc, den):
        pairs = []
        for pair in range(GQA_GROUP // 2):
            ot = [acc[g] / jnp.sum(den[g], axis=0, keepdims=True) for g in (2 * pair, 2 * pair + 1)]
            pairs.append(jnp.concatenate(ot, axis=0).T)
        return jnp.concatenate(pairs, axis=1).astype(BF16)

    def fixed_shift_path():
        subs = [slice(i, i + ts) for i in range(0, tq, ts)]
        acc = {(t, h): jnp.zeros((HEAD_DIM, ts), F32) for t in range(len(subs)) for h in range(GQA_GROUP)}
        den = {(t, h): jnp.zeros((SUBLANES, ts), F32) for t in range(len(subs)) for h in range(GQA_GROUP)}

        def finish(t, h, s, vt_ref, start, size, last):
            pf = jnp.exp2(s - bound_ref[0])
            vt_blk = vt_ref[0, 0, :, start:start + size]
            acc[t, h] = acc[t, h] + jnp.dot(vt_blk, pf.astype(BF16), preferred_element_type=F32)
            den[t, h] = den[t, h] + key_sum(pf)
            if last:
                o_ref[0, subs[t], :] = normalised([acc[t, g] for g in range(GQA_GROUP)],
                                                  [den[t, g] for g in range(GQA_GROUP)])

        blocks = [(k_ref, vt_ref, start, min(bk, seg - start))
                  for k_ref, vt_ref, seg in segments for start in range(0, seg, bk)]
        pending = []
        for t, r in enumerate(subs):
            for bi, (k_ref, vt_ref, start, size) in enumerate(blocks):
                for h in range(GQA_GROUP):
                    k_blk = k_ref[0, 0, start:start + size, :]
                    s = jnp.dot(k_blk, qt_ref[0, h, :, r], preferred_element_type=F32)
                    last = bi == len(blocks) - 1 and h == GQA_GROUP - 1
                    pending.append((t, h, s, vt_ref, start, size, last))
                    if len(pending) > QK_AHEAD:
                        finish(*pending.pop(0))
        while pending:
            finish(*pending.pop(0))

    def online_max_path():
        def sub_tile(t, carry):
            q0 = pl.multiple_of(t * ts, ts)
            qs = [qt_ref[0, h, :, pl.ds(q0, ts)] for h in range(GQA_GROUP)]
            state = tuple((jnp.full((1, ts), -jnp.inf, F32), jnp.zeros((SUBLANES, ts), F32),
                           jnp.zeros((HEAD_DIM, ts), F32)) for _ in range(GQA_GROUP))
            for k_ref, vt_ref, seg in segments:
                def key_block(j, st, k_ref=k_ref, vt_ref=vt_ref):
                    k0 = pl.multiple_of(j * bk, bk)
                    k_blk = k_ref[0, 0, pl.ds(k0, bk), :]
                    vt_blk = vt_ref[0, 0, :, pl.ds(k0, bk)]
                    new = []
                    for h in range(GQA_GROUP):
                        m, den, acc = st[h]
                        s = jnp.dot(k_blk, qs[h], preferred_element_type=F32)
                        m_new = jnp.maximum(m, jnp.max(s, axis=0, keepdims=True))
                        alpha = jnp.exp2(m - m_new)
                        pf = jnp.exp2(s - m_new)
                        new.append((m_new, alpha * den + key_sum(pf),
                                    alpha * acc + jnp.dot(vt_blk, pf.astype(BF16), preferred_element_type=F32)))
                    return tuple(new)
                state = lax.fori_loop(0, seg // bk, key_block, state)
            o_ref[0, pl.ds(q0, ts), :] = normalised([st[2] for st in state], [st[1] for st in state])
            return carry
        lax.fori_loop(0, tq // ts, sub_tile, 0)

    safe = 2.0 * bound_ref[0] <= SAFE_EXP2_RANGE
    pl.when(safe)(fixed_shift_path)
    pl.when(jnp.logical_not(safe))(online_max_path)


def _attention(bound, qt, ks, vts, *, tq, bk):
    b, _, _, n = qt.shape
    seg_lens = tuple(k.shape[2] for k in ks)
    assert all(s % bk == 0 for s in seg_lens)
    kern = functools.partial(_attn_kernel, tq=tq, seg_lens=seg_lens, bk=bk)
    in_specs = [pl.BlockSpec(memory_space=pltpu.SMEM),
                pl.BlockSpec((1, GQA_GROUP, HEAD_DIM, tq), lambda i, g, j: (i, g, 0, j))]
    in_specs += [pl.BlockSpec((1, 1, s, HEAD_DIM), lambda i, g, j: (i, g, 0, 0)) for s in seg_lens]
    in_specs += [pl.BlockSpec((1, 1, HEAD_DIM, s), lambda i, g, j: (i, g, 0, 0)) for s in seg_lens]
    return pl.pallas_call(
        kern,
        grid=(b, KV_HEADS, n // tq),
        in_specs=in_specs,
        out_specs=pl.BlockSpec((1, tq, GQA_GROUP * HEAD_DIM), lambda i, g, j: (i, j, g)),
        out_shape=jax.ShapeDtypeStruct((b, n, ATTN_WIDTH), BF16),
        compiler_params=pltpu.CompilerParams(
            dimension_semantics=("arbitrary", "arbitrary", "arbitrary"), vmem_limit_bytes=VMEM_LIMIT),
        name="attention",
    )(bound, qt, *ks, *vts)


def _outffn_kernel(attn_ref, mlp_ref, x_ref, mod_ref, gpm_ref, gpf_ref, gqf_ref,
                   wo_ref, wgu_ref, wd_ref, o_ref, *, tm):
    mod = mod_ref[0, 0]
    gate_mix, shift, scale, gate_ffn = mod[2:3], mod[3:4], mod[4:5], mod[5:6]

    ts = min(tm, 2 * SUB_ROWS)
    subs = [slice(i, i + ts) for i in range(0, tm, ts)]

    def out_proj(r):
        return (jnp.dot(attn_ref[0, r], wo_ref[0, :ATTN_WIDTH], preferred_element_type=F32)
                + jnp.dot(mlp_ref[0, r], wo_ref[0, ATTN_WIDTH:], preferred_element_type=F32))

    def ffn(h):
        def gate_up(ci):
            cols = slice(ci * FFN_CHUNK, (ci + 1) * FFN_CHUNK)
            up_cols = slice(FFN_HIDDEN + ci * FFN_CHUNK, FFN_HIDDEN + (ci + 1) * FFN_CHUNK)
            return (jnp.dot(h, wgu_ref[0, :, cols], preferred_element_type=F32),
                    jnp.dot(h, wgu_ref[0, :, up_cols], preferred_element_type=F32))

        pending = [gate_up(ci) for ci in range(FFN_AHEAD)]
        y = None
        for ci in range(N_FFN_CHUNKS):
            g, u = pending.pop(0)
            if ci + FFN_AHEAD < N_FFN_CHUNKS:
                pending.append(gate_up(ci + FFN_AHEAD))
            a = (g * jax.nn.sigmoid(g) * u).astype(BF16)
            d = jnp.dot(a, wd_ref[0, ci * FFN_CHUNK:(ci + 1) * FFN_CHUNK], preferred_element_type=F32)
            y = d if y is None else y + d
        return y

    out_next = out_proj(subs[0])
    for i, r in enumerate(subs):
        out = out_next
        if i + 1 < len(subs):
            out_next = out_proj(subs[i + 1])
        x1 = x_ref[0, r] + gate_mix * _rms(out, gpm_ref[...])
        h = (_rms(x1, gpf_ref[...]) * (1.0 + scale) + shift).astype(BF16)
        o_ref[0, r] = x1 + gate_ffn * _rms(ffn(h), gqf_ref[...])


def _outffn(attn, mlp, x, mods, layer, stream_row, g_post_mix, g_pre_ffn, g_post_ffn,
            w_out, w_gu, w_down, *, tm):
    b, n, _ = x.shape
    if stream_row is None:
        mod_map = lambda i, j: (layer, i, 0, 0)
    else:
        mod_map = lambda i, j: (layer, stream_row, 0, 0)
    row_spec = lambda w: pl.BlockSpec((1, tm, w), lambda i, j: (i, j, 0))
    return pl.pallas_call(
        functools.partial(_outffn_kernel, tm=tm),
        grid=(b, n // tm),
        in_specs=[
            row_spec(ATTN_WIDTH),
            row_spec(MLP_WIDTH),
            row_spec(D_MODEL),
            pl.BlockSpec((1, 1, MOD_SLOTS, D_MODEL), mod_map),
            _const_spec((1, D_MODEL)),
            _const_spec((1, D_MODEL)),
            _const_spec((1, D_MODEL)),
            _layer_spec(layer, (ATTN_WIDTH + MLP_WIDTH, D_MODEL)),
            _layer_spec(layer, (D_MODEL, 2 * FFN_HIDDEN)),
            _layer_spec(layer, (FFN_HIDDEN, D_MODEL)),
        ],
        out_specs=row_spec(D_MODEL),
        out_shape=jax.ShapeDtypeStruct((b, n, D_MODEL), F32),
        compiler_params=pltpu.CompilerParams(
            dimension_semantics=("arbitrary", "arbitrary"), vmem_limit_bytes=VMEM_LIMIT),
        name="outffn",
    )(attn, mlp, x, mods, g_post_mix, g_pre_ffn, g_post_ffn, w_out, w_gu, w_down)


def _rope_tables(n):
    rows = n // GRID_W
    pos_row = jnp.broadcast_to(jnp.arange(rows, dtype=F32)[:, None], (rows, GRID_W)).reshape(-1)
    pos_col = jnp.broadcast_to(jnp.arange(GRID_W, dtype=F32)[None, :], (rows, GRID_W)).reshape(-1)
    inv = ROPE_THETA ** (-jnp.arange(0, ROPE_AXIS_DIM, 2, dtype=F32) / ROPE_AXIS_DIM)
    ang_r = pos_row[:, None] * inv
    ang_c = pos_col[:, None] * inv
    cos64 = jnp.concatenate([jnp.cos(ang_r)] * 2 + [jnp.cos(ang_c)] * 2, axis=-1)
    sin64 = jnp.concatenate([-jnp.sin(ang_r), jnp.sin(ang_r), -jnp.sin(ang_c), jnp.sin(ang_c)], axis=-1)
    return jnp.tile(cos64, (1, 2)), jnp.tile(sin64, (1, 2))


def kernel(x, c, ctx, c_ctx, w_mod, b_mod, g_pre_mix, g_post_mix, g_pre_ffn, g_post_ffn,
           w_in, g_q, g_k, g_sg, b_sg, w_s, b_s, w_out, w_ffn_in, w_ffn_out):
    b, n, _ = x.shape
    n_ctx = ctx.shape[1]
    assert n % ROW_TILE == 0 and n_ctx % SUB_ROWS == 0 and n_ctx <= ROW_TILE and b < MOD_ROWS
    cos_t, sin_t = _rope_tables(n)
    cos_c, sin_c = cos_t[:n_ctx], sin_t[:n_ctx]

    cvec = jnp.concatenate([c, c_ctx[None], jnp.zeros((MOD_ROWS - b - 1, D_MODEL), F32)], axis=0)
    mods = _modulation(cvec, w_mod, b_mod)
    mods = mods.reshape(DEPTH, MOD_ROWS, N_MOD, D_MODEL)
    mods = jnp.pad(mods, ((0, 0), (0, 0), (0, MOD_SLOTS - N_MOD), (0, 0)))

    w_in_b = w_in.astype(BF16)
    w_out_b = w_out.astype(BF16)
    w_gu = w_ffn_in.astype(BF16)
    w_down = w_ffn_out.astype(BF16)
    ws_cat = w_s.reshape(DEPTH, MLP_HEADS // 2, 2, CHUNK, CHUNK).transpose(0, 1, 3, 2, 4)
    ws_cat = ws_cat.reshape(DEPTH, MLP_HEADS // 2, CHUNK, 2 * CHUNK).astype(BF16)
    bs_full = jnp.repeat(b_s.transpose(0, 2, 1), HEAD_DIM, axis=2)
    gq_t = jnp.tile(g_q, (1, MXU_DIM // HEAD_DIM))[:, None]
    gk_t = jnp.tile(g_k, (1, LANES // HEAD_DIM))[:, None]

    xc = ctx
    for l in range(DEPTH):
        last = l == DEPTH - 1
        in_args = (g_pre_mix[l][None], w_in_b, gq_t[l], gk_t[l])
        mlp_args = (g_sg[l][None], b_sg[l][None], ws_cat[l], bs_full[l])
        ffn_args = (g_post_mix[l][None], g_pre_ffn[l][None], g_post_ffn[l][None], w_out_b, w_gu, w_down)

        qct, kc, vct, mlp_c = _inproj(xc, mods, l, b, *in_args, cos_c, sin_c, *mlp_args,
                                      tm=n_ctx, use_rope=False)
        qxt, kx, vxt, mlp_x = _inproj(x, mods, l, None, *in_args, cos_t, sin_t, *mlp_args,
                                      tm=2 * ROW_TILE, use_rope=True)
        bound = (BOUND_SLACK * LOG2E * math.sqrt(HEAD_DIM)) * jnp.max(jnp.abs(g_q[l])) * jnp.max(jnp.abs(g_k[l]))
        bound = bound.reshape(1)
        attn_x = _attention(bound, qxt, (kc, kx), (vct, vxt), tq=ROW_TILE, bk=KEY_BLOCK)
        x = _outffn(attn_x, mlp_x, x, mods, l, None, *ffn_args, tm=ROW_TILE)
        if not last:
            attn_c = _attention(bound, qct, (kc,), (vct,), tq=n_ctx, bk=KEY_BLOCK)
            xc = _outffn(attn_c, mlp_c, xc, mods, l, b, *ffn_args, tm=n_ctx)
    return x
```
